```python
import math
import jax
import jax.numpy as jnp
from jax import lax
from jax.lax import linalg as lax_linalg
import numpy as np

D_MODEL = 2048
BATCH = 4
SEQ = 2048
DEPTH = 4
DEC_BATCH = 128
DEC_SEQ = 1
PAST_LEN = 16384
PAGE_SIZE = 128

BRANCH_W = D_MODEL // 4
N_BRANCH = 3
S5_W = BRANCH_W
S5_GROUP = 16
S5_GROUPS = S5_W // S5_GROUP
S5_STATE = 64
DN_HEADS = 4
DN_HEAD_DIM = 128
DN_W = DN_HEADS * DN_HEAD_DIM
DN_CHUNK = 64
CONV_W = 4
LRU_W = BRANCH_W
LRU_BLOCKS = 4
LRU_BLOCK_W = LRU_W // LRU_BLOCKS
LRU_C = 8.0
N_GROUPS = 4
EXPERTS_PER_GROUP = 8
N_EXPERTS = N_GROUPS * EXPERTS_PER_GROUP
TOP_K = 2
D_EXPERT = D_MODEL // 4
ROW_BLOCK = 128
NORM_EPS = 1e-6
IN_SPLITS = (S5_W, 3 * DN_W, DN_HEADS, DN_HEADS, DN_W, LRU_W, LRU_W, N_BRANCH * D_MODEL)
IN_W = S5_W + 3 * DN_W + 2 * DN_HEADS + DN_W + 2 * LRU_W + N_BRANCH * D_MODEL

kernel_name = 'hybrid_s5_gdn_rglru_hmoe_step'


def rmsnorm(x, g):
    xf = x.astype(jnp.float32)
    y = xf * lax.rsqrt(jnp.mean(xf * xf, axis=-1, keepdims=True) + NORM_EPS)
    return (y * g.astype(jnp.float32)).astype(x.dtype)


def l2norm(x):
    return x * lax.rsqrt(jnp.sum(x * x, axis=-1, keepdims=True) + NORM_EPS)


def causal_conv(x, buf, w, b=None):
    L = x.shape[1]
    xp = jnp.concatenate([buf.astype(x.dtype), x], axis=1)
    y = sum(xp[:, j:j + L] * w[j] for j in range(CONV_W))
    if b is not None:
        y = y + b
    return y, xp[:, L:]


def linear_scan(a, b):
    def combine(l, r):
        return (l[0] * r[0], r[0] * l[1] + r[1])
    return lax.associative_scan(combine, (a, b), axis=1)[1]


def s5_ssm(u, h0_re, h0_im, log_dt, a_re, a_im, b_re, b_im, c_re, c_im, d, w_glu, b_glu):
    f32 = jnp.float32
    bsz, L, _ = u.shape
    uf = u.astype(f32)
    ug = uf.reshape(bsz, L, S5_GROUPS, S5_GROUP)
    dt = jnp.exp(log_dt.astype(f32))[:, None]
    ar = a_re.astype(f32)
    ai = a_im.astype(f32)
    mag = jnp.exp(ar * dt)
    lr = mag * jnp.cos(ai * dt)
    li = mag * jnp.sin(ai * dt)
    den = ar * ar + ai * ai
    fr = ((lr - 1.0) * ar + li * ai) / den
    fi = (li * ar - (lr - 1.0) * ai) / den
    br_ = b_re.astype(f32)
    bi_ = b_im.astype(f32)
    bbr = fr[..., None] * br_ - fi[..., None] * bi_
    bbi = fr[..., None] * bi_ + fi[..., None] * br_
    xr = jnp.einsum('blgn,gpn->blgp', ug, bbr)
    xi = jnp.einsum('blgn,gpn->blgp', ug, bbi)
    h0r = h0_re.astype(f32)
    h0i = h0_im.astype(f32)
    xr = xr.at[:, 0].add(lr * h0r - li * h0i)
    xi = xi.at[:, 0].add(lr * h0i + li * h0r)
    ar_t = jnp.broadcast_to(lr, xr.shape)
    ai_t = jnp.broadcast_to(li, xi.shape)

    def combine(l, r):
        a1r, a1i, b1r, b1i = l
        a2r, a2i, b2r, b2i = r
        return (a1r * a2r - a1i * a2i, a1r * a2i + a1i * a2r,
                a2r * b1r - a2i * b1i + b2r, a2r * b1i + a2i * b1r + b2i)

    _, _, hr, hi = lax.associative_scan(combine, (ar_t, ai_t, xr, xi), axis=1)
    y = (jnp.einsum('blgp,gnp->blgn', hr, c_re.astype(f32))
         - jnp.einsum('blgp,gnp->blgn', hi, c_im.astype(f32)))
    y = y.reshape(bsz, L, S5_W) + d.astype(f32) * uf
    y = jax.nn.gelu(y)
    y = y * jax.nn.sigmoid(y @ w_glu.astype(f32) + b_glu.astype(f32))
    return y.astype(u.dtype), hr[:, -1], hi[:, -1]


def chunk_gated_delta_rule(q, k, v, g, beta, s0):
    bsz, L, H, dk = q.shape
    dv = v.shape[-1]
    C = min(DN_CHUNK, L)
    n = -(-L // C)
    pad = n * C - L
    if pad:
        pw = ((0, 0), (0, pad), (0, 0), (0, 0))
        q = jnp.pad(q, pw)
        k = jnp.pad(k, pw)
        v = jnp.pad(v, pw)
        g = jnp.pad(g, pw[:3])
        beta = jnp.pad(beta, pw[:3])

    def chunks(t):
        t = t.reshape((bsz, n, C, H) + t.shape[3:])
        return jnp.moveaxis(t, (1, 3), (0, 2))

    q = chunks(q) * (dk ** -0.5)
    k = chunks(k)
    v = chunks(v)
    g = chunks(g)
    beta = chunks(beta)
    gc = jnp.cumsum(g, axis=-1)
    idx = jnp.arange(C)
    causal = idx[:, None] >= idx[None, :]
    strict = idx[:, None] > idx[None, :]
    decay = jnp.exp(jnp.where(causal, gc[..., :, None] - gc[..., None, :], -jnp.inf))
    kb = k * beta[..., None]
    m = jnp.where(strict, jnp.einsum('nbhid,nbhjd->nbhij', kb, k) * decay, 0.0)
    tmat = m + jnp.eye(C, dtype=m.dtype)
    u = lax_linalg.triangular_solve(tmat, v * beta[..., None], left_side=True, lower=True, unit_diagonal=True)
    w = lax_linalg.triangular_solve(tmat, kb * jnp.exp(gc)[..., None], left_side=True, lower=True, unit_diagonal=True)
    att = jnp.where(causal, jnp.einsum('nbhid,nbhjd->nbhij', q, k) * decay, 0.0)

    def step(s, inp):
        qi, ki, ui, wi, gi, ai = inp
        v_new = ui - jnp.einsum('bhcd,bhde->bhce', wi, s)
        o = (jnp.einsum('bhcd,bhde->bhce', qi * jnp.exp(gi)[..., None], s)
             + jnp.einsum('bhij,bhje->bhie', ai, v_new))
        g_last = gi[..., -1:]
        s = (s * jnp.exp(g_last)[..., None]
             + jnp.einsum('bhcd,bhce->bhde', ki * jnp.exp(g_last - gi)[..., None], v_new))
        return s, o

    s, o = lax.scan(step, s0, (q, k, u, w, gc, att))
    o = jnp.moveaxis(o, (0, 2), (1, 3)).reshape(bsz, n * C, H, dv)[:, :L]
    return o, s


def gated_deltanet(qkv, a_logit, b_logit, z, s0, conv_buf, conv_w, a_log, dt_bias, norm_g):
    f32 = jnp.float32
    bsz, L, _ = qkv.shape
    y, new_buf = causal_conv(qkv, conv_buf, conv_w)
    y = jax.nn.silu(y.astype(f32)).reshape(bsz, L, 3, DN_HEADS, DN_HEAD_DIM)
    q = l2norm(y[:, :, 0])
    k = l2norm(y[:, :, 1])
    v = y[:, :, 2]
    beta = jax.nn.sigmoid(b_logit.astype(f32))
    g = -jnp.exp(a_log.astype(f32)) * jax.nn.softplus(a_logit.astype(f32) + dt_bias.astype(f32))
    o, s = chunk_gated_delta_rule(q, k, v, g, beta, s0.astype(f32))
    o = rmsnorm(o, norm_g) * jax.nn.silu(z.astype(f32).reshape(bsz, L, DN_HEADS, DN_HEAD_DIM))
    return o.reshape(bsz, L, DN_W).astype(qkv.dtype), s, new_buf


def rg_lru(xb, gb, h0, conv_buf, pos, conv_w, conv_b, w_a, b_a, w_x, b_x, lam):
    f32 = jnp.float32
    bsz, L, _ = xb.shape
    xc, new_buf = causal_conv(xb, conv_buf, conv_w, conv_b)
    xf = xc.astype(f32)
    xblk = xf.reshape(bsz, L, LRU_BLOCKS, LRU_BLOCK_W)
    r = jax.nn.sigmoid(jnp.einsum('blnw,nwv->blnv', xblk, w_a.astype(f32)).reshape(bsz, L, LRU_W) + b_a.astype(f32))
    i = jax.nn.sigmoid(jnp.einsum('blnw,nwv->blnv', xblk, w_x.astype(f32)).reshape(bsz, L, LRU_W) + b_x.astype(f32))
    log_a = -LRU_C * r * jax.nn.softplus(-lam.astype(f32))
    reset = (pos == 0)[None, :, None]
    a = jnp.where(reset, 0.0, jnp.exp(log_a))
    mult = jnp.where(reset, 1.0, jnp.sqrt(-jnp.expm1(2.0 * log_a)))
    b = mult * (i * xf)
    b = b.at[:, 0].add(a[:, 0] * h0.astype(f32))
    h = linear_scan(a, b)
    y = h * jax.nn.gelu(gb.astype(f32))
    return y.astype(xb.dtype), h[:, -1], new_buf


def token_mixers(h, pos, st, p):
    bsz, L, _ = h.shape
    s5_re, s5_im, dn_s, dn_conv, lru_h, lru_conv = st
    proj = h @ p['w_in']
    offsets = np.cumsum(IN_SPLITS)[:-1].tolist()
    u_a, qkv, dn_a, dn_b, dn_z, lru_x, lru_g, gate_logits = jnp.split(proj, offsets, axis=-1)
    ya, n_s5_re, n_s5_im = s5_ssm(u_a, s5_re, s5_im, p['s5_log_dt'], p['s5_a_re'], p['s5_a_im'],
                                  p['s5_b_re'], p['s5_b_im'], p['s5_c_re'], p['s5_c_im'],
                                  p['s5_d'], p['s5_w_glu'], p['s5_b_glu'])
    yb, n_dn_s, n_dn_conv = gated_deltanet(qkv, dn_a, dn_b, dn_z, dn_s, dn_conv, p['dn_conv_w'],
                                           p['dn_a_log'], p['dn_dt_bias'], p['dn_norm_g'])
    yc, n_lru_h, n_lru_conv = rg_lru(lru_x, lru_g, lru_h, lru_conv, pos, p['lru_conv_w'], p['lru_conv_b'],
                                     p['lru_w_a'], p['lru_b_a'], p['lru_w_x'], p['lru_b_x'], p['lru_lambda'])
    branches = jnp.stack([ya, yb, yc], axis=2)
    up = jnp.einsum('blnw,nwd->blnd', branches, p['w_branch'])
    gates = jax.nn.sigmoid(gate_logits.reshape(bsz, L, N_BRANCH, D_MODEL))
    merged = jnp.sum(gates * up, axis=2)
    return merged @ p['w_out'], (n_s5_re, n_s5_im, n_dn_s, n_dn_conv, n_lru_h, n_lru_conv)


def routed_experts(t, eidx, wts, w_g, w_u, w_d):
    n_tok, D = t.shape
    n_assign = n_tok * TOP_K
    flat_e = eidx.reshape(-1)
    order = jnp.argsort(flat_e)
    sorted_e = flat_e[order]
    counts = jnp.bincount(flat_e, length=N_EXPERTS)
    padded = (counts + ROW_BLOCK - 1) // ROW_BLOCK * ROW_BLOCK
    pad_end = jnp.cumsum(padded)
    pad_start = pad_end - padded
    start = jnp.cumsum(counts) - counts
    dest = pad_start[sorted_e] + jnp.arange(n_assign) - start[sorted_e]
    n_blocks = -(-(n_assign + N_EXPERTS * (ROW_BLOCK - 1)) // ROW_BLOCK)
    n_rows = n_blocks * ROW_BLOCK
    row_token = jnp.full((n_rows,), n_tok, jnp.int32).at[dest].set((order // TOP_K).astype(jnp.int32))
    t_pad = jnp.concatenate([t, jnp.zeros((1, D), t.dtype)], axis=0)
    x_rows = t_pad[row_token].reshape(n_blocks, ROW_BLOCK, D)
    block_e = jnp.minimum(jnp.searchsorted(pad_end, jnp.arange(n_blocks) * ROW_BLOCK, side='right'), N_EXPERTS - 1)

    def run(args):
        xb, e = args
        hid = jax.nn.silu(xb @ w_g[e]) * (xb @ w_u[e])
        return hid @ w_d[e]

    y_rows = lax.map(run, (x_rows, block_e)).reshape(n_rows, D)
    y_assign = jnp.zeros((n_assign, D), y_rows.dtype).at[order].set(y_rows[dest])
    return jnp.sum(y_assign.reshape(n_tok, TOP_K, D) * wts[..., None].astype(y_rows.dtype), axis=1)


def hier_moe(h, p):
    bsz, L, D = h.shape
    t = h.reshape(bsz * L, D)
    n_tok = t.shape[0]
    lg = (t @ p['w_rg'] + p['b_rg']).astype(jnp.float32)
    pg = jax.nn.softmax(lg, axis=-1)
    gsel = jnp.argmax(lg, axis=-1).astype(jnp.int32)
    le = (t @ p['w_re'] + p['b_re']).astype(jnp.float32).reshape(n_tok, N_GROUPS, EXPERTS_PER_GROUP)
    le = jnp.take_along_axis(le, gsel[:, None, None], axis=1)[:, 0]
    top_v, top_i = lax.top_k(le, TOP_K)
    wts = jax.nn.softmax(top_v, axis=-1) * jnp.take_along_axis(pg, gsel[:, None], axis=1)
    eidx = gsel[:, None] * EXPERTS_PER_GROUP + top_i.astype(jnp.int32)
    y = routed_experts(t, eidx, wts, p['w_e_gate'], p['w_e_up'], p['w_e_down'])
    return y.reshape(bsz, L, D)


def decoder_layer(x, c, pos, st, p):
    mod = jax.nn.silu(c) @ p['w_ada'] + p['b_ada']
    sh1, sc1, g1, sh2, sc2, g2 = jnp.split(mod[:, None, :], 6, axis=-1)
    h = rmsnorm(x, p['norm1_g']) * (1.0 + sc1) + sh1
    mix, new_st = token_mixers(h, pos, st, p)
    x = x + g1 * mix
    h = rmsnorm(x, p['norm2_g']) * (1.0 + sc2) + sh2
    x = x + g2 * hier_moe(h, p)
    return x, new_st


def run_trunk(x, c, pos, states, params, final_g):
    per_layer = []
    for l in range(DEPTH):
        p = {name: arr[l] for name, arr in params.items()}
        st = tuple(s[l] for s in states)
        x, ns = decoder_layer(x, c, pos, st, p)
        per_layer.append(ns)
    new_states = tuple(jnp.stack([ns[i] for ns in per_layer]) for i in range(len(states)))
    return rmsnorm(x, final_g), new_states


def zero_states(bsz):
    f32 = jnp.float32
    return (jnp.zeros((DEPTH, bsz, S5_GROUPS, S5_STATE), f32),
            jnp.zeros((DEPTH, bsz, S5_GROUPS, S5_STATE), f32),
            jnp.zeros((DEPTH, bsz, DN_HEADS, DN_HEAD_DIM, DN_HEAD_DIM), f32),
            jnp.zeros((DEPTH, bsz, CONV_W - 1, 3 * DN_W), f32),
            jnp.zeros((DEPTH, bsz, LRU_W), f32),
            jnp.zeros((DEPTH, bsz, CONV_W - 1, LRU_W), f32))


def setup_inputs(seed: int = 0) -> dict:
    key = jax.random.key(seed)
    keys = jax.random.split(key, 64)
    counter = [0]
    f32 = jnp.float32

    def nxt():
        k = keys[counter[0]]
        counter[0] += 1
        return k

    def nrm(shape, scale=1.0):
        return jax.random.normal(nxt(), shape, f32) * scale

    def unif(shape, lo, hi):
        return jax.random.uniform(nxt(), shape, f32, lo, hi)

    D = D_MODEL
    x_prompt = nrm((BATCH, SEQ, D))
    x_sample = nrm((DEC_BATCH, DEC_SEQ, D))
    c_prompt = nrm((BATCH, D))
    c_sample = nrm((DEC_BATCH, D))
    state_s5_re = nrm((DEPTH, DEC_BATCH, S5_GROUPS, S5_STATE), 0.5)
    state_s5_im = nrm((DEPTH, DEC_BATCH, S5_GROUPS, S5_STATE), 0.5)
    state_dn = nrm((DEPTH, DEC_BATCH, DN_HEADS, DN_HEAD_DIM, DN_HEAD_DIM), 0.1)
    state_dn_conv = nrm((DEPTH, DEC_BATCH, CONV_W - 1, 3 * DN_W))
    state_lru = nrm((DEPTH, DEC_BATCH, LRU_W), 0.5)
    state_lru_conv = nrm((DEPTH, DEC_BATCH, CONV_W - 1, LRU_W))
    final_g = 1.0 + nrm((D,), 0.05)
    norm1_g = 1.0 + nrm((DEPTH, D), 0.05)
    norm2_g = 1.0 + nrm((DEPTH, D), 0.05)
    w_ada = nrm((DEPTH, D, 6 * D), 0.3 * D ** -0.5)
    b_ada = nrm((DEPTH, 6 * D), 0.02)
    w_in = nrm((DEPTH, D, IN_W), D ** -0.5)
    s5_log_dt = unif((DEPTH, S5_GROUPS), math.log(1e-3), math.log(1e-1))
    s5_a_re = -0.5 + nrm((DEPTH, S5_GROUPS, S5_STATE), 0.01)
    s5_a_im = math.pi * jnp.arange(S5_STATE, dtype=f32) + nrm((DEPTH, S5_GROUPS, S5_STATE), 0.01)
    s5_b_re = nrm((DEPTH, S5_GROUPS, S5_STATE, S5_GROUP), (2 * S5_GROUP) ** -0.5)
    s5_b_im = nrm((DEPTH, S5_GROUPS, S5_STATE, S5_GROUP), (2 * S5_GROUP) ** -0.5)
    s5_c_re = nrm((DEPTH, S5_GROUPS, S5_GROUP, S5_STATE), (2 * S5_STATE) ** -0.5)
    s5_c_im = nrm((DEPTH, S5_GROUPS, S5_GROUP, S5_STATE), (2 * S5_STATE) ** -0.5)
    s5_d = nrm((DEPTH, S5_W))
    s5_w_glu = nrm((DEPTH, S5_W, S5_W), S5_W ** -0.5)
    s5_b_glu = nrm((DEPTH, S5_W), 0.01)
    dn_conv_w = nrm((DEPTH, CONV_W, 3 * DN_W), 0.5)
    dn_a_log = jnp.log(unif((DEPTH, DN_HEADS), 1.0, 16.0))
    dt = jnp.exp(unif((DEPTH, DN_HEADS), math.log(1e-3), math.log(1e-1)))
    dn_dt_bias = dt + jnp.log(-jnp.expm1(-dt))
    dn_norm_g = 1.0 + nrm((DEPTH, DN_HEAD_DIM), 0.05)
    lru_conv_w = nrm((DEPTH, CONV_W, LRU_W), 0.5)
    lru_conv_b = nrm((DEPTH, LRU_W), 0.01)
    lru_w_a = nrm((DEPTH, LRU_BLOCKS, LRU_BLOCK_W, LRU_BLOCK_W), LRU_BLOCK_W ** -0.5)
    lru_b_a = nrm((DEPTH, LRU_W), 0.01)
    lru_w_x = nrm((DEPTH, LRU_BLOCKS, LRU_BLOCK_W, LRU_BLOCK_W), LRU_BLOCK_W ** -0.5)
    lru_b_x = nrm((DEPTH, LRU_W), 0.01)
    a8 = unif((DEPTH, LRU_W), 0.9, 0.999)
    sig = a8 ** (1.0 / LRU_C)
    lru_lambda = jnp.log(sig) - jnp.log1p(-sig)
    w_branch = nrm((DEPTH, N_BRANCH, BRANCH_W, D), BRANCH_W ** -0.5)
    w_out = nrm((DEPTH, D, D), D ** -0.5)
    w_rg = nrm((DEPTH, D, N_GROUPS), D ** -0.5)
    b_rg = nrm((DEPTH, N_GROUPS), 0.01)
    w_re = nrm((DEPTH, D, N_EXPERTS), D ** -0.5)
    b_re = nrm((DEPTH, N_EXPERTS), 0.01)
    w_e_gate = nrm((DEPTH, N_EXPERTS, D, D_EXPERT), D ** -0.5)
    w_e_up = nrm((DEPTH, N_EXPERTS, D, D_EXPERT), D ** -0.5)
    w_e_down = nrm((DEPTH, N_EXPERTS, D_EXPERT, D), D_EXPERT ** -0.5)
    return {'x_prompt': x_prompt, 'x_sample': x_sample, 'c_prompt': c_prompt, 'c_sample': c_sample,
            'state_s5_re': state_s5_re, 'state_s5_im': state_s5_im, 'state_dn': state_dn,
            'state_dn_conv': state_dn_conv, 'state_lru': state_lru, 'state_lru_conv': state_lru_conv,
            'final_g': final_g, 'norm1_g': norm1_g, 'norm2_g': norm2_g, 'w_ada': w_ada, 'b_ada': b_ada,
            'w_in': w_in, 's5_log_dt': s5_log_dt, 's5_a_re': s5_a_re, 's5_a_im': s5_a_im,
            's5_b_re': s5_b_re, 's5_b_im': s5_b_im, 's5_c_re': s5_c_re, 's5_c_im': s5_c_im,
            's5_d': s5_d, 's5_w_glu': s5_w_glu, 's5_b_glu': s5_b_glu,
            'dn_conv_w': dn_conv_w, 'dn_a_log': dn_a_log, 'dn_dt_bias': dn_dt_bias, 'dn_norm_g': dn_norm_g,
            'lru_conv_w': lru_conv_w, 'lru_conv_b': lru_conv_b, 'lru_w_a': lru_w_a, 'lru_b_a': lru_b_a,
            'lru_w_x': lru_w_x, 'lru_b_x': lru_b_x, 'lru_lambda': lru_lambda,
            'w_branch': w_branch, 'w_out': w_out, 'w_rg': w_rg, 'b_rg': b_rg, 'w_re': w_re, 'b_re': b_re,
            'w_e_gate': w_e_gate, 'w_e_up': w_e_up, 'w_e_down': w_e_down}


def reference(x_prompt, x_sample, c_prompt, c_sample, state_s5_re, state_s5_im, state_dn, state_dn_conv,
              state_lru, state_lru_conv, final_g, norm1_g, norm2_g, w_ada, b_ada, w_in,
              s5_log_dt, s5_a_re, s5_a_im, s5_b_re, s5_b_im, s5_c_re, s5_c_im, s5_d, s5_w_glu, s5_b_glu,
              dn_conv_w, dn_a_log, dn_dt_bias, dn_norm_g,
              lru_conv_w, lru_conv_b, lru_w_a, lru_b_a, lru_w_x, lru_b_x, lru_lambda,
              w_branch, w_out, w_rg, b_rg, w_re, b_re, w_e_gate, w_e_up, w_e_down):
    params = {'norm1_g': norm1_g, 'norm2_g': norm2_g, 'w_ada': w_ada, 'b_ada': b_ada, 'w_in': w_in,
              's5_log_dt': s5_log_dt, 's5_a_re': s5_a_re, 's5_a_im': s5_a_im, 's5_b_re': s5_b_re,
              's5_b_im': s5_b_im, 's5_c_re': s5_c_re, 's5_c_im': s5_c_im, 's5_d': s5_d,
              's5_w_glu': s5_w_glu, 's5_b_glu': s5_b_glu, 'dn_conv_w': dn_conv_w, 'dn_a_log': dn_a_log,
              'dn_dt_bias': dn_dt_bias, 'dn_norm_g': dn_norm_g, 'lru_conv_w': lru_conv_w,
              'lru_conv_b': lru_conv_b, 'lru_w_a': lru_w_a, 'lru_b_a': lru_b_a, 'lru_w_x': lru_w_x,
              'lru_b_x': lru_b_x, 'lru_lambda': lru_lambda, 'w_branch': w_branch, 'w_out': w_out,
              'w_rg': w_rg, 'b_rg': b_rg, 'w_re': w_re, 'b_re': b_re, 'w_e_gate': w_e_gate,
              'w_e_up': w_e_up, 'w_e_down': w_e_down}
    pos_prompt = jnp.arange(x_prompt.shape[1])
    y_prompt, new_prompt = run_trunk(x_prompt, c_prompt, pos_prompt, zero_states(x_prompt.shape[0]),
                                     params, final_g)
    pos_sample = PAST_LEN + jnp.arange(x_sample.shape[1])
    sample_states = (state_s5_re, state_s5_im, state_dn, state_dn_conv, state_lru, state_lru_conv)
    y_sample, new_sample = run_trunk(x_sample, c_sample, pos_sample, sample_states, params, final_g)
    p_s5_re, p_s5_im, p_dn, p_dn_conv, p_lru, p_lru_conv = new_prompt
    s_s5_re, s_s5_im, s_dn, s_dn_conv, s_lru, s_lru_conv = new_sample
    return (y_prompt, y_sample, p_s5_re, p_s5_im, p_dn, p_dn_conv, p_lru, p_lru_conv,
            s_s5_re, s_s5_im, s_dn, s_dn_conv, s_lru, s_lru_conv)
```

```python
import functools
import math
from typing import NamedTuple

import jax
import jax.numpy as jnp
from jax import lax
from jax.experimental import pallas as pl
from jax.experimental.pallas import tpu as pltpu

F32 = jnp.float32
MXU_DTYPE = jnp.bfloat16

D_MODEL = 2048
BRANCH_W = D_MODEL // 4
S5_GROUP = 16
S5_GROUPS = BRANCH_W // S5_GROUP
S5_STATE = 64
S5_N = S5_GROUPS * S5_STATE
DN_HEADS = 4
DN_HEAD_DIM = 128
DN_W = DN_HEADS * DN_HEAD_DIM
DN_CHUNK = 64
CONV_W = 4
LRU_W = BRANCH_W
LRU_BLOCKS = 4
LRU_C = 8.0
N_GROUPS = 4
EXPERTS_PER_GROUP = 8
N_EXPERTS = N_GROUPS * EXPERTS_PER_GROUP
TOP_K = 2
D_EXPERT = D_MODEL // 4
NORM_EPS = 1e-6
PAST_LEN = 16384

COL_QKV = BRANCH_W
COL_AB = COL_QKV + 3 * DN_W
COL_REST = COL_AB + 2 * DN_HEADS
REST_W = DN_W + 2 * LRU_W + 3 * D_MODEL
REST_GATE = DN_W + 2 * LRU_W

LANE = 128
SUBLANE = 8
HALF = SUBLANE // 2
VMEM_LIMIT = 56 * 1024 * 1024
MOE_ROWS = 256
COMBINE_ROWS = 128


class _Group(NamedTuple):
    B: int
    L: int
    tm: int

    @property
    def T(self):
        return self.B * self.L

    @property
    def nt(self):
        return max(self.L // self.tm, 1)

    @property
    def n_tiles(self):
        return self.T // self.tm


def _params(*sem):
    return pltpu.CompilerParams(dimension_semantics=sem, vmem_limit_bytes=VMEM_LIMIT)


def _mm(a, b):
    return jnp.dot(a.astype(MXU_DTYPE), b.astype(MXU_DTYPE), preferred_element_type=F32)


def _mm_nt(a, b):
    return lax.dot_general(a.astype(MXU_DTYPE), b.astype(MXU_DTYPE), (((1,), (1,)), ((), ())),
                           preferred_element_type=F32)


def _mm_tn(a, b):
    return lax.dot_general(a.astype(MXU_DTYPE), b.astype(MXU_DTYPE), (((0,), (0,)), ((), ())),
                           preferred_element_type=F32)


def _silu(x):
    return x * jax.nn.sigmoid(x)


def _gelu_tanh(x):
    return 0.5 * x * (1.0 + jnp.tanh(math.sqrt(2.0 / math.pi) * (x + 0.044715 * (x * x * x))))


def _softplus(x):
    return jnp.maximum(x, 0.0) + jnp.log1p(jnp.exp(-jnp.abs(x)))


def _nat_spec(g, tn, col0=0):
    return pl.BlockSpec((g.tm, tn), lambda i, j: (i, col0 + j))


def _tb_shape(g, w):
    return (g.B, w) if g.L == 1 else (g.L, g.B * w)


def _tb_spec(g, w, tn):
    if g.L == 1:
        return pl.BlockSpec((g.tm, tn), lambda i, j: (0, j))
    nb, nt = w // tn, g.nt
    return pl.BlockSpec((g.tm, tn), lambda i, j: (i % nt, (i // nt) * nb + j))


def _mod_spec(g, k, tn):
    nb, nt = D_MODEL // tn, g.nt
    if g.L == 1:
        return pl.BlockSpec((g.tm, tn), lambda i, j: (0, k * nb + j))
    return pl.BlockSpec((None, 1, tn), lambda i, j: (i // nt, 0, k * nb + j))


def _ada_kernel(c_ref, w_ref, b_ref, o_ref):
    o_ref[...] = _mm(_silu(c_ref[...]), w_ref[...]) + b_ref[...]


def _ada_mod(c_all, w_ada, b_ada):
    depth, _, n6 = w_ada.shape
    rows = c_all.shape[0]
    tn = 1024
    return pl.pallas_call(
        _ada_kernel,
        grid=(depth, n6 // tn),
        in_specs=[pl.BlockSpec((rows, D_MODEL), lambda l, j: (0, 0)),
                  pl.BlockSpec((None, D_MODEL, tn), lambda l, j: (l, 0, j)),
                  pl.BlockSpec((None, 1, tn), lambda l, j: (l, 0, j))],
        out_specs=pl.BlockSpec((None, rows, tn), lambda l, j: (l, 0, j)),
        out_shape=jax.ShapeDtypeStruct((depth, rows, n6), F32),
        compiler_params=_params("arbitrary", "arbitrary"),
        name="ada_mod",
    )(c_all, w_ada, b_ada.reshape(depth, 1, n6))


def _rms(x, gain):
    return x * lax.rsqrt(jnp.mean(x * x, axis=-1, keepdims=True) + NORM_EPS) * gain


def _norm_mod_kernel(x_ref, g_ref, sc_ref, sh_ref, o_ref):
    y = _rms(x_ref[...], g_ref[...])
    o_ref[...] = (y * (1.0 + sc_ref[...]) + sh_ref[...]).astype(o_ref.dtype)


def _norm_mod(g, x, gain, mod, k_shift, k_scale):
    gm = g._replace(tm=min(g.tm, 512))
    return pl.pallas_call(
        _norm_mod_kernel,
        grid=(gm.n_tiles, 1),
        in_specs=[_nat_spec(gm, D_MODEL),
                  pl.BlockSpec((1, D_MODEL), lambda i, j: (0, 0)),
                  _mod_spec(gm, k_scale, D_MODEL),
                  _mod_spec(gm, k_shift, D_MODEL)],
        out_specs=_nat_spec(gm, D_MODEL),
        out_shape=jax.ShapeDtypeStruct((g.T, D_MODEL), MXU_DTYPE),
        compiler_params=_params("arbitrary", "arbitrary"),
        name="norm_mod",
    )(x, gain.reshape(1, D_MODEL), mod, mod)


def _final_norm_kernel(x_ref, g_ref, o_ref):
    o_ref[...] = _rms(x_ref[...], g_ref[...])


def _final_norm(g, x, gain):
    gm = g._replace(tm=min(g.tm, 512))
    return pl.pallas_call(
        _final_norm_kernel,
        grid=(gm.n_tiles, 1),
        in_specs=[_nat_spec(gm, D_MODEL), pl.BlockSpec((1, D_MODEL), lambda i, j: (0, 0))],
        out_specs=_nat_spec(gm, D_MODEL),
        out_shape=jax.ShapeDtypeStruct((g.T, D_MODEL), F32),
        compiler_params=_params("arbitrary", "arbitrary"),
        name="final_norm",
    )(x, gain.reshape(1, D_MODEL))


def _proj_kernel(a_ref, w_ref, o_ref):
    o_ref[...] = _mm(a_ref[...], w_ref[...]).astype(o_ref.dtype)


def _proj(g, a, w, layer, col0, n, tn, tb=False):
    out_shape = _tb_shape(g, n) if tb else (g.T, n)
    out_spec = _tb_spec(g, n, tn) if tb else _nat_spec(g, tn)
    cb = col0 // tn
    return pl.pallas_call(
        _proj_kernel,
        grid=(g.n_tiles, n // tn),
        in_specs=[pl.BlockSpec((g.tm, D_MODEL), lambda i, j: (i, 0)),
                  pl.BlockSpec((None, D_MODEL, tn), lambda i, j: (layer, 0, cb + j))],
        out_specs=out_spec,
        out_shape=jax.ShapeDtypeStruct(out_shape, F32),
        compiler_params=_params("arbitrary", "arbitrary"),
        name="in_proj",
    )(a, w)


def _out_proj_kernel(a_ref, w_ref, x_ref, g_ref, o_ref):
    o_ref[...] = x_ref[...] + g_ref[...] * _mm(a_ref[...], w_ref[...])


def _out_proj(g, merged, w_out, layer, x, mod, k_gate):
    tn = 512
    return pl.pallas_call(
        _out_proj_kernel,
        grid=(g.n_tiles, D_MODEL // tn),
        in_specs=[pl.BlockSpec((g.tm, D_MODEL), lambda i, j: (i, 0)),
                  pl.BlockSpec((None, D_MODEL, tn), lambda i, j: (layer, 0, j)),
                  _nat_spec(g, tn),
                  _mod_spec(g, k_gate, tn)],
        out_specs=_nat_spec(g, tn),
        out_shape=jax.ShapeDtypeStruct((g.T, D_MODEL), F32),
        compiler_params=_params("arbitrary", "arbitrary"),
        name="out_proj",
    )(merged, w_out, x, mod)


def _merge_kernel(h_ref, ya_ref, yb_ref, yc_ref, wg0, wg1, wg2, wb0, wb1, wb2, o_ref):
    h = h_ref[...]
    acc = None
    for y_ref, wg, wb in ((ya_ref, wg0, wb0), (yb_ref, wg1, wb1), (yc_ref, wg2, wb2)):
        t = jax.nn.sigmoid(_mm(h, wg[...])) * _mm(y_ref[...], wb[...])
        acc = t if acc is None else acc + t
    o_ref[...] = acc.astype(o_ref.dtype)


def _merge(g, h, ya_tb, yb, yc_tb, w_rest, w_branch, layer):
    tn = 256
    gate_specs = [pl.BlockSpec((None, D_MODEL, tn),
                               functools.partial(lambda i, j, c: (layer, 0, c + j), c=(REST_GATE + n * D_MODEL) // tn))
                  for n in range(3)]
    br_specs = [pl.BlockSpec((None, None, BRANCH_W, tn), functools.partial(lambda i, j, n: (layer, n, 0, j), n=n))
                for n in range(3)]
    if g.L == 1:
        y_tb_spec = pl.BlockSpec((g.tm, BRANCH_W), lambda i, j: (0, 0))
    else:
        nt = g.nt
        y_tb_spec = pl.BlockSpec((g.tm, BRANCH_W), lambda i, j: (i % nt, i // nt))
    return pl.pallas_call(
        _merge_kernel,
        grid=(g.n_tiles, D_MODEL // tn),
        in_specs=[pl.BlockSpec((g.tm, D_MODEL), lambda i, j: (i, 0)),
                  y_tb_spec,
                  pl.BlockSpec((g.tm, BRANCH_W), lambda i, j: (i, 0)),
                  y_tb_spec] + gate_specs + br_specs,
        out_specs=_nat_spec(g, tn),
        out_shape=jax.ShapeDtypeStruct((g.T, D_MODEL), MXU_DTYPE),
        compiler_params=_params("arbitrary", "arbitrary"),
        name="merge",
    )(h, ya_tb, yb, yc_tb, w_rest, w_rest, w_rest, w_branch, w_branch, w_branch)


def _s5_disc_kernel(ldt_ref, ar_ref, ai_ref, br_ref, bi_ref, lr_ref, li_ref, bbr_ref, bbi_ref):
    dt = jnp.exp(ldt_ref[...])
    ar, ai = ar_ref[...], ai_ref[...]
    mag = jnp.exp(ar * dt)
    lr = mag * jnp.cos(ai * dt)
    li = mag * jnp.sin(ai * dt)
    den = ar * ar + ai * ai
    fr = ((lr - 1.0) * ar + li * ai) / den
    fi = (li * ar - (lr - 1.0) * ai) / den
    lr_ref[...] = lr
    li_ref[...] = li
    br, bi = br_ref[...], bi_ref[...]
    bbr_ref[...] = fr * br - fi * bi
    bbi_ref[...] = fr * bi + fi * br


def _s5_discretize(log_dt, a_re, a_im, b_re, b_im):
    depth = log_dt.shape[0]
    n = depth * S5_GROUPS
    rows = n * S5_GROUP
    rep = lambda v, w: jnp.repeat(v.reshape(n, w), S5_GROUP, axis=0)
    brt = jnp.swapaxes(b_re, -1, -2).reshape(rows, S5_STATE)
    bit = jnp.swapaxes(b_im, -1, -2).reshape(rows, S5_STATE)
    sds = jax.ShapeDtypeStruct
    lr, li, bbr, bbi = pl.pallas_call(
        _s5_disc_kernel,
        out_shape=tuple(sds((rows, S5_STATE), F32) for _ in range(4)),
        name="s5_discretize",
    )(rep(log_dt, 1), rep(a_re, S5_STATE), rep(a_im, S5_STATE), brt, bit)
    eye = jnp.eye(S5_GROUPS, dtype=F32)

    def block_diag_in(m):
        m = m.reshape(depth, S5_GROUPS, S5_GROUP, S5_STATE)
        return jnp.einsum('lgnp,gh->lgnhp', m, eye).reshape(depth, BRANCH_W, S5_N).astype(MXU_DTYPE)

    lam = lambda v: v.reshape(n, S5_GROUP, S5_STATE)[:, 0].reshape(depth, 1, S5_N)
    return lam(lr), lam(li), block_diag_in(bbr), block_diag_in(bbi)


def _s5_out_matrix(c):
    depth = c.shape[0]
    eye = jnp.eye(S5_GROUPS, dtype=F32)
    return jnp.einsum('lgnp,gh->lgphn', c, eye).reshape(depth, S5_N, BRANCH_W).astype(MXU_DTYPE)


def _scan_tiles(n_tiles, cw, width, tile_fn, load_state, store_state):
    for c in range(width // cw):
        cs = pl.ds(c * cw, cw)

        def body(k, carry, cs=cs):
            return tile_fn(pl.ds(pl.multiple_of(k * SUBLANE, SUBLANE), SUBLANE), cs, carry)

        store_state(cs, lax.fori_loop(0, n_tiles, body, load_state(cs)))


def _top_half(shape):
    return lax.broadcasted_iota(jnp.int32, shape, 0) < HALF


def _s5_kernel(u_ref, h0r_ref, h0i_ref, bbr_ref, bbi_ref, lr_ref, li_ref, cr_ref, ci_ref, d_ref, wglu_ref,
               bglu_ref, y_ref, hlr_ref, hli_ref, xr_s, xi_s, sr_s, si_s, *, bsz, lc):
    i = pl.program_id(0)

    @pl.when(i == 0)
    def _():
        sr_s[...] = jnp.zeros_like(sr_s)
        si_s[...] = jnp.zeros_like(si_s)
        sr_s[pl.ds(0, bsz), :] = h0r_ref[...]
        si_s[pl.ds(0, bsz), :] = h0i_ref[...]

    u = u_ref[...]
    xr_s[...] = _mm(u, bbr_ref[...])
    xi_s[...] = _mm(u, bbi_ref[...])

    if lc == 1:
        lr, li, sr, si = lr_ref[...], li_ref[...], sr_s[...], si_s[...]
        hr = lr * sr - li * si + xr_s[...]
        hi = lr * si + li * sr + xi_s[...]
        xr_s[...] = hr
        xi_s[...] = hi
        sr_s[...] = hr
        si_s[...] = hi
    else:
        def tile(rows, cs, carry):
            sr, si = carry
            lr, li = lr_ref[:, cs], li_ref[:, cs]
            xr, xi = xr_s[rows, cs], xi_s[rows, cs]
            top = _top_half(xr.shape)
            h1r = lr * sr - li * si + xr
            h1i = lr * si + li * sr + xi
            pr, pi = pltpu.roll(h1r, HALF, 0), pltpu.roll(h1i, HALF, 0)
            h2r = lr * pr - li * pi + xr
            h2i = lr * pi + li * pr + xi
            xr_s[rows, cs] = jnp.where(top, h1r, h2r)
            xi_s[rows, cs] = jnp.where(top, h1i, h2i)
            return pltpu.roll(h2r, HALF, 0), pltpu.roll(h2i, HALF, 0)

        def load(cs):
            return sr_s[:, cs], si_s[:, cs]

        def store(cs, carry):
            sr_s[:, cs], si_s[:, cs] = carry

        _scan_tiles(lc * bsz // SUBLANE, 1024, S5_N, tile, load, store)

    y = _mm(xr_s[...], cr_ref[...]) - _mm(xi_s[...], ci_ref[...])
    y = _gelu_tanh(y + d_ref[...] * u)
    y = y * jax.nn.sigmoid(_mm(y, wglu_ref[...]) + bglu_ref[...])
    y_ref[...] = y.astype(y_ref.dtype)

    @pl.when(i == pl.num_programs(0) - 1)
    def _():
        hlr_ref[...] = sr_s[pl.ds(0, bsz), :]
        hli_ref[...] = si_s[pl.ds(0, bsz), :]


def _s5(g, u_tb, h0r, h0i, disc, cr, ci, d, w_glu, b_glu, layer):
    lam_r, lam_i, bbr, bbi = disc
    bsz = g.B
    lc = 1 if g.L == 1 else min(g.L, 256)
    rows = lc * bsz
    u_rows = u_tb.reshape(g.L * bsz, BRANCH_W)
    const2 = lambda i: (0, 0)
    lay3 = lambda i: (layer, 0, 0)
    sds = jax.ShapeDtypeStruct
    y, hlr, hli = pl.pallas_call(
        functools.partial(_s5_kernel, bsz=bsz, lc=lc),
        grid=(g.L // lc,),
        in_specs=[pl.BlockSpec((rows, BRANCH_W), lambda i: (i, 0)),
                  pl.BlockSpec((bsz, S5_N), const2), pl.BlockSpec((bsz, S5_N), const2),
                  pl.BlockSpec((None, BRANCH_W, S5_N), lay3), pl.BlockSpec((None, BRANCH_W, S5_N), lay3),
                  pl.BlockSpec((None, 1, S5_N), lay3), pl.BlockSpec((None, 1, S5_N), lay3),
                  pl.BlockSpec((None, S5_N, BRANCH_W), lay3), pl.BlockSpec((None, S5_N, BRANCH_W), lay3),
                  pl.BlockSpec((None, 1, BRANCH_W), lay3),
                  pl.BlockSpec((None, BRANCH_W, BRANCH_W), lay3),
                  pl.BlockSpec((None, 1, BRANCH_W), lay3)],
        out_specs=(pl.BlockSpec((rows, BRANCH_W), lambda i: (i, 0)),
                   pl.BlockSpec((bsz, S5_N), const2), pl.BlockSpec((bsz, S5_N), const2)),
        out_shape=(sds((g.L * bsz, BRANCH_W), MXU_DTYPE), sds((bsz, S5_N), F32), sds((bsz, S5_N), F32)),
        scratch_shapes=[pltpu.VMEM((rows, S5_N), F32), pltpu.VMEM((rows, S5_N), F32),
                        pltpu.VMEM((max(bsz, SUBLANE), S5_N), F32), pltpu.VMEM((max(bsz, SUBLANE), S5_N), F32)],
        compiler_params=_params("arbitrary"),
        name="s5_ssm",
    )(u_rows, h0r, h0i, bbr, bbi, lam_r, lam_i, cr, ci, d.reshape(-1, 1, BRANCH_W), w_glu,
      b_glu.reshape(-1, 1, BRANCH_W))
    return y.reshape(_tb_shape(g, BRANCH_W)), hlr, hli


def _lru_kernel(xg_ref, cb_ref, h0_ref, cw_ref, cbias_ref, wa_ref, ba_ref, wx_ref, bx_ref, lam_ref,
                y_ref, hl_ref, xbuf, a_s, b_s, h_s, *, bsz, lc, reset_first):
    i = pl.program_id(0)
    rows = lc * bsz
    tail = (CONV_W - 1) * bsz

    @pl.when(i == 0)
    def _():
        xbuf[pl.ds(0, tail), :] = cb_ref[...]
        h_s[...] = jnp.zeros_like(h_s)
        h_s[pl.ds(0, bsz), :] = h0_ref[...]

    xbuf[pl.ds(tail, rows), :] = xg_ref[:, pl.ds(0, LRU_W)]
    xc = cbias_ref[...]
    for j in range(CONV_W):
        xc = xc + cw_ref[pl.ds(j, 1), :] * xbuf[pl.ds(j * bsz, rows), :]
    new_tail = xbuf[pl.ds(rows, tail), :]
    xbuf[pl.ds(0, tail), :] = new_tail

    r = jax.nn.sigmoid(_mm(xc, wa_ref[...]) + ba_ref[...])
    ig = jax.nn.sigmoid(_mm(xc, wx_ref[...]) + bx_ref[...])
    log_a = -LRU_C * r * _softplus(-lam_ref[...])
    a = jnp.exp(log_a)
    mult = jnp.sqrt(1.0 - jnp.exp(2.0 * log_a))
    if reset_first:
        first = jnp.logical_and(i == 0, lax.broadcasted_iota(jnp.int32, (rows, LRU_W), 0) < bsz)
        a = jnp.where(first, 0.0, a)
        mult = jnp.where(first, 1.0, mult)
    bv = mult * (ig * xc)

    if lc == 1:
        h = a * h_s[...] + bv
        h_s[...] = h
        hs = h
    else:
        a_s[...] = a
        b_s[...] = bv

        def tile(rws, cs, h):
            a, b = a_s[rws, cs], b_s[rws, cs]
            h1 = a * h + b
            h2 = a * pltpu.roll(h1, HALF, 0) + b
            b_s[rws, cs] = jnp.where(_top_half(h1.shape), h1, h2)
            return pltpu.roll(h2, HALF, 0)

        def store(cs, h):
            h_s[:, cs] = h

        _scan_tiles(rows // SUBLANE, LRU_W, LRU_W, tile, lambda cs: h_s[:, cs], store)
        hs = b_s[...]

    y_ref[...] = (hs * _gelu_tanh(xg_ref[:, pl.ds(LRU_W, LRU_W)])).astype(y_ref.dtype)

    @pl.when(i == pl.num_programs(0) - 1)
    def _():
        hl_ref[...] = h_s[pl.ds(0, bsz), :]


def _block_diag(w):
    depth, nb, k, _ = w.shape
    eye = jnp.eye(nb, dtype=w.dtype)
    return jnp.einsum('lnwv,nm->lnwmv', w, eye).reshape(depth, nb * k, nb * k).astype(MXU_DTYPE)


def _lru(g, xg_tb, cb_rows, h0, p, layer, reset_first):
    bsz = g.B
    lc = 1 if g.L == 1 else min(g.L, 256)
    rows = lc * bsz
    tail = (CONV_W - 1) * bsz
    xg_rows = xg_tb.reshape(g.L * bsz, 2 * LRU_W)
    const2 = lambda i: (0, 0)
    lay3 = lambda i: (layer, 0, 0)
    sds = jax.ShapeDtypeStruct
    vec = lambda v: v.reshape(-1, 1, LRU_W)
    y, hl = pl.pallas_call(
        functools.partial(_lru_kernel, bsz=bsz, lc=lc, reset_first=reset_first),
        grid=(g.L // lc,),
        in_specs=[pl.BlockSpec((rows, 2 * LRU_W), lambda i: (i, 0)),
                  pl.BlockSpec((tail, LRU_W), const2),
                  pl.BlockSpec((bsz, LRU_W), const2),
                  pl.BlockSpec((None, CONV_W, LRU_W), lay3),
                  pl.BlockSpec((None, 1, LRU_W), lay3),
                  pl.BlockSpec((None, LRU_W, LRU_W), lay3), pl.BlockSpec((None, 1, LRU_W), lay3),
                  pl.BlockSpec((None, LRU_W, LRU_W), lay3), pl.BlockSpec((None, 1, LRU_W), lay3),
                  pl.BlockSpec((None, 1, LRU_W), lay3)],
        out_specs=(pl.BlockSpec((rows, LRU_W), lambda i: (i, 0)), pl.BlockSpec((bsz, LRU_W), const2)),
        out_shape=(sds((g.L * bsz, LRU_W), MXU_DTYPE), sds((bsz, LRU_W), F32)),
        scratch_shapes=[pltpu.VMEM((rows + tail, LRU_W), F32), pltpu.VMEM((rows, LRU_W), F32),
                        pltpu.VMEM((rows, LRU_W), F32), pltpu.VMEM((max(bsz, SUBLANE), LRU_W), F32)],
        compiler_params=_params("arbitrary"),
        name="rg_lru",
    )(xg_rows, cb_rows, h0, p['lru_conv_w'], vec(p['lru_conv_b']), p['lru_wa_bd'], vec(p['lru_b_a']),
      p['lru_wx_bd'], vec(p['lru_b_x']), vec(p['lru_lambda']))
    return y.reshape(_tb_shape(g, LRU_W)), hl


def _l2norm(x):
    return x * lax.rsqrt(jnp.sum(x * x, axis=-1, keepdims=True) + NORM_EPS)


def _dn_gates(ab, prm):
    g = -jnp.exp(prm[0:1, :]) * _softplus(ab + prm[1:2, :])
    return g, jax.nn.sigmoid(ab)


def _dn_seq_kernel(qkv_ref, z_ref, ab_ref, cw_ref, prm_ref, ng_ref, y_ref, sfin_ref, xbuf, s_s):
    c = pl.program_id(1)
    cs = DN_CHUNK
    hd = DN_HEAD_DIM

    @pl.when(c == 0)
    def _():
        xbuf[pl.ds(0, SUBLANE), :] = jnp.zeros((SUBLANE, 3 * DN_W), F32)
        s_s[...] = jnp.zeros_like(s_s)

    xbuf[pl.ds(SUBLANE, cs), :] = qkv_ref[...]
    conv = None
    for j in range(CONV_W):
        t = cw_ref[pl.ds(j, 1), :] * xbuf[pl.ds(SUBLANE - (CONV_W - 1) + j, cs), :]
        conv = t if conv is None else conv + t
    new_tail = xbuf[pl.ds(cs, SUBLANE), :]
    xbuf[pl.ds(0, SUBLANE), :] = new_tail
    act = _silu(conv)

    glog, beta_all = _dn_gates(ab_ref[...], prm_ref[...])
    row = lax.broadcasted_iota(jnp.int32, (cs, cs), 0)
    col = lax.broadcasted_iota(jnp.int32, (cs, cs), 1)
    causal = row >= col
    strict = row > col
    tri = causal.astype(F32)
    gc = jnp.dot(tri, glog, precision=lax.Precision.HIGHEST, preferred_element_type=F32)
    lane = lax.broadcasted_iota(jnp.int32, (cs, LANE), 1)
    eye = (row == col).astype(F32)
    ng = ng_ref[...]

    for h in range(DN_HEADS):
        q = _l2norm(act[:, h * hd:(h + 1) * hd]) * (hd ** -0.5)
        k = _l2norm(act[:, DN_W + h * hd:DN_W + (h + 1) * hd])
        v = act[:, 2 * DN_W + h * hd:2 * DN_W + (h + 1) * hd]
        beta = beta_all[:, DN_HEADS + h:DN_HEADS + h + 1]
        gcol = gc[:, h:h + 1]
        sel = (lane == h).astype(F32)
        grow = lax.dot_general(sel, gc, (((1,), (1,)), ((), ())), precision=lax.Precision.HIGHEST,
                               preferred_element_type=F32)
        decay = jnp.exp(jnp.where(causal, gcol - grow, -jnp.inf))
        kb = k * beta
        nmat = jnp.where(strict, -(_mm_nt(kb, k) * decay), 0.0)
        rinv = eye + nmat
        xpow = nmat
        for _ in range(5):
            xpow = _mm(xpow, xpow)
            rinv = rinv + _mm(rinv, xpow)
        eg = jnp.exp(gcol)
        uw = _mm(rinv, jnp.concatenate([v * beta, kb * eg], axis=1))
        u, w = uw[:, :hd], uw[:, hd:]
        att = jnp.where(causal, _mm_nt(q, k) * decay, 0.0)
        s = s_s[h]
        v_new = u - _mm(w, s)
        o = _mm(q * eg, s) + _mm(att, v_new)
        glast = gc[cs - 1:cs, h:h + 1]
        s_s[h] = s * jnp.exp(glast) + _mm_tn(k * jnp.exp(glast - gcol), v_new)
        o = _rms(o, ng) * _silu(z_ref[:, h * hd:(h + 1) * hd])
        y_ref[:, h * hd:(h + 1) * hd] = o.astype(y_ref.dtype)

    @pl.when(c == pl.num_programs(1) - 1)
    def _():
        sfin_ref[...] = s_s[...]


def _dn_seq(g, qkv, z, ab, conv_w, prm, norm_g, layer):
    cs = DN_CHUNK
    nc = g.L // cs
    lay3 = lambda b, c: (layer, 0, 0)
    sds = jax.ShapeDtypeStruct
    return pl.pallas_call(
        _dn_seq_kernel,
        grid=(g.B, nc),
        in_specs=[pl.BlockSpec((cs, 3 * DN_W), lambda b, c: (b * nc + c, 0)),
                  pl.BlockSpec((cs, DN_W), lambda b, c: (b * nc + c, 0)),
                  pl.BlockSpec((cs, LANE), lambda b, c: (b * nc + c, 0)),
                  pl.BlockSpec((None, CONV_W, 3 * DN_W), lay3),
                  pl.BlockSpec((None, 2, LANE), lay3),
                  pl.BlockSpec((None, 1, DN_HEAD_DIM), lay3)],
        out_specs=(pl.BlockSpec((cs, DN_W), lambda b, c: (b * nc + c, 0)),
                   pl.BlockSpec((None, DN_HEADS, DN_HEAD_DIM, DN_HEAD_DIM), lambda b, c: (b, 0, 0, 0))),
        out_shape=(sds((g.T, DN_W), MXU_DTYPE), sds((g.B, DN_HEADS, DN_HEAD_DIM, DN_HEAD_DIM), F32)),
        scratch_shapes=[pltpu.VMEM((cs + SUBLANE, 3 * DN_W), F32),
                        pltpu.VMEM((DN_HEADS, DN_HEAD_DIM, DN_HEAD_DIM), F32)],
        compiler_params=_params("arbitrary", "arbitrary"),
        name="deltanet_seq",
    )(qkv, z, ab, conv_w, prm, norm_g.reshape(-1, 1, DN_HEAD_DIM))


def _dn_step_kernel(q_ref, k_ref, v_ref, cq_ref, ck_ref, cv_ref, z_ref, ab_ref, cwq_ref, cwk_ref, cwv_ref,
                    prm_ref, ng_ref, s_ref, y_ref, so_ref, o_s, *, bsz):
    h = pl.program_id(0)
    hd = DN_HEAD_DIM

    def conv(x_ref, c_ref, w_ref):
        acc = w_ref[pl.ds(CONV_W - 1, 1), :] * x_ref[...]
        for j in range(CONV_W - 1):
            acc = acc + w_ref[pl.ds(j, 1), :] * c_ref[j]
        return _silu(acc)

    q = _l2norm(conv(q_ref, cq_ref, cwq_ref)) * (hd ** -0.5)
    k = _l2norm(conv(k_ref, ck_ref, cwk_ref))
    v = conv(v_ref, cv_ref, cwv_ref)
    glog, beta_all = _dn_gates(ab_ref[...], prm_ref[...])
    lane = lax.broadcasted_iota(jnp.int32, (bsz, LANE), 1)
    eg = jnp.exp(jnp.sum(jnp.where(lane == h, glog, 0.0), axis=-1, keepdims=True))
    beta = jnp.sum(jnp.where(lane == DN_HEADS + h, beta_all, 0.0), axis=-1, keepdims=True)
    att = jnp.sum(q * k, axis=-1, keepdims=True)
    kt = k.T
    qt = q.T
    for b in range(bsz):
        s = s_ref[b]
        kcol = kt[:, b:b + 1]
        ks = jnp.sum(s * kcol, axis=0, keepdims=True)
        qs = jnp.sum(s * qt[:, b:b + 1], axis=0, keepdims=True)
        eg_b, beta_b = eg[b:b + 1, :], beta[b:b + 1, :]
        v_new = beta_b * v[b:b + 1, :] - (beta_b * eg_b) * ks
        o_s[pl.ds(b, 1), :] = eg_b * qs + att[b:b + 1, :] * v_new
        so_ref[b] = s * eg_b + kcol * v_new
    y_ref[...] = (_rms(o_s[...], ng_ref[...]) * _silu(z_ref[...])).astype(y_ref.dtype)


def _dn_step(g, qkv, conv_state, z, ab, conv_w, prm, norm_g, s0, layer):
    bsz = g.B
    hd = DN_HEAD_DIM
    conv_t = jnp.swapaxes(conv_state, 0, 1)
    sec = lambda s: (lambda h: (0, s * DN_HEADS + h))
    sec3 = lambda s: (lambda h: (0, 0, s * DN_HEADS + h))
    secw = lambda s: (lambda h: (layer, 0, s * DN_HEADS + h))
    lay3 = lambda h: (layer, 0, 0)
    sds = jax.ShapeDtypeStruct
    return pl.pallas_call(
        functools.partial(_dn_step_kernel, bsz=bsz),
        grid=(DN_HEADS,),
        in_specs=[pl.BlockSpec((bsz, hd), sec(s)) for s in range(3)]
                 + [pl.BlockSpec((CONV_W - 1, bsz, hd), sec3(s)) for s in range(3)]
                 + [pl.BlockSpec((bsz, hd), lambda h: (0, h)),
                    pl.BlockSpec((bsz, LANE), lambda h: (0, 0))]
                 + [pl.BlockSpec((None, CONV_W, hd), secw(s)) for s in range(3)]
                 + [pl.BlockSpec((None, 2, LANE), lay3),
                    pl.BlockSpec((None, 1, hd), lay3),
                    pl.BlockSpec((bsz, None, hd, hd), lambda h: (0, h, 0, 0))],
        out_specs=(pl.BlockSpec((bsz, hd), lambda h: (0, h)),
                   pl.BlockSpec((bsz, None, hd, hd), lambda h: (0, h, 0, 0))),
        out_shape=(sds((bsz, DN_W), MXU_DTYPE), sds((bsz, DN_HEADS, hd, hd), F32)),
        scratch_shapes=[pltpu.VMEM((bsz, hd), F32)],
        compiler_params=_params("arbitrary"),
        name="deltanet_step",
    )(qkv, qkv, qkv, conv_t, conv_t, conv_t, z, ab, conv_w, conv_w, conv_w, prm,
      norm_g.reshape(-1, 1, hd), s0)


def _router_kernel(x_ref, g_ref, sc_ref, sh_ref, wr_ref, br_ref, h_ref, r_ref):
    h = _rms(x_ref[...], g_ref[...]) * (1.0 + sc_ref[...]) + sh_ref[...]
    h_ref[...] = h
    logits = jnp.dot(h, wr_ref[...], precision=lax.Precision.HIGHEST, preferred_element_type=F32) + br_ref[...]
    lane = lax.broadcasted_iota(jnp.int32, logits.shape, 1).astype(F32)
    ninf = -jnp.inf
    big = float(LANE)
    lg = jnp.where(lane < N_GROUPS, logits, ninf)
    gmax = jnp.max(lg, axis=-1, keepdims=True)
    gsel = jnp.min(jnp.where(lg == gmax, lane, big), axis=-1, keepdims=True)
    pg = 1.0 / jnp.sum(jnp.where(lane < N_GROUPS, jnp.exp(logits - gmax), 0.0), axis=-1, keepdims=True)
    lo = N_GROUPS + EXPERTS_PER_GROUP * gsel
    le = jnp.where(jnp.logical_and(lane >= lo, lane < lo + EXPERTS_PER_GROUP), logits, ninf)
    v1 = jnp.max(le, axis=-1, keepdims=True)
    i1 = jnp.min(jnp.where(le == v1, lane, big), axis=-1, keepdims=True)
    le2 = jnp.where(lane == i1, ninf, le)
    v2 = jnp.max(le2, axis=-1, keepdims=True)
    i2 = jnp.min(jnp.where(le2 == v2, lane, big), axis=-1, keepdims=True)
    e2 = jnp.exp(v2 - v1)
    w1 = pg / (1.0 + e2)
    w2 = pg * e2 / (1.0 + e2)
    r_ref[...] = jnp.where(lane == 0, i1 - N_GROUPS,
                           jnp.where(lane == 1, i2 - N_GROUPS,
                                     jnp.where(lane == 2, w1, jnp.where(lane == 3, w2, 0.0))))


def _router(g, x, gain, mod, w_route, b_route, layer):
    gm = g._replace(tm=min(g.tm, 256))
    sds = jax.ShapeDtypeStruct
    return pl.pallas_call(
        _router_kernel,
        grid=(gm.n_tiles, 1),
        in_specs=[_nat_spec(gm, D_MODEL),
                  pl.BlockSpec((1, D_MODEL), lambda i, j: (0, 0)),
                  _mod_spec(gm, 4, D_MODEL),
                  _mod_spec(gm, 3, D_MODEL),
                  pl.BlockSpec((None, D_MODEL, LANE), lambda i, j: (layer, 0, 0)),
                  pl.BlockSpec((None, 1, LANE), lambda i, j: (layer, 0, 0))],
        out_specs=(_nat_spec(gm, D_MODEL), _nat_spec(gm, LANE)),
        out_shape=(sds((g.T, D_MODEL), F32), sds((g.T, LANE), F32)),
        compiler_params=_params("arbitrary", "arbitrary"),
        name="moe_router",
    )(x, gain.reshape(1, D_MODEL), mod, mod, w_route, b_route)


def _row_copy(src_hbm, row, dst, dst_row, sem):
    return pltpu.make_async_copy(src_hbm.at[pl.ds(row, 1)], dst.at[pl.ds(dst_row, 1)], sem)


def _experts_kernel(be_ref, tok_ref, nact_ref, h_hbm, wg_ref, wu_ref, wd_ref, y_ref, xbuf, sem):
    b = pl.program_id(0)
    nact = nact_ref[0]
    rb = MOE_ROWS

    def gather(blk, slot):
        for r in range(rb):
            _row_copy(h_hbm, tok_ref[blk * rb + r], xbuf.at[slot], r, sem.at[slot]).start()

    @pl.when(b == 0)
    def _():
        gather(0, 0)

    @pl.when(b + 1 < nact)
    def _():
        gather(b + 1, (b + 1) % 2)

    @pl.when(b < nact)
    def _():
        slot = b % 2
        for r in range(rb):
            _row_copy(h_hbm, 0, xbuf.at[slot], r, sem.at[slot]).wait()
        x = xbuf[slot]
        hid = _silu(_mm(x, wg_ref[...])) * _mm(x, wu_ref[...])
        y_ref[...] = _mm(hid, wd_ref[...])

    @pl.when(b >= nact)
    def _():
        y_ref[...] = jnp.zeros_like(y_ref)


def _experts(h_all, block_e, row_token, n_active, w_gate, w_up, w_down, layer):
    n_blocks = block_e.shape[0]
    rb = MOE_ROWS
    wspec = lambda shape: pl.BlockSpec((None, None) + shape, lambda b, be, tok, na: (layer, be[b], 0, 0))
    return pl.pallas_call(
        _experts_kernel,
        grid_spec=pltpu.PrefetchScalarGridSpec(
            num_scalar_prefetch=3,
            grid=(n_blocks,),
            in_specs=[pl.BlockSpec(memory_space=pl.ANY),
                      wspec((D_MODEL, D_EXPERT)), wspec((D_MODEL, D_EXPERT)), wspec((D_EXPERT, D_MODEL))],
            out_specs=pl.BlockSpec((rb, D_MODEL), lambda b, be, tok, na: (b, 0)),
            scratch_shapes=[pltpu.VMEM((2, rb, D_MODEL), F32), pltpu.SemaphoreType.DMA((2,))]),
        out_shape=jax.ShapeDtypeStruct((n_blocks * rb, D_MODEL), F32),
        compiler_params=_params("arbitrary"),
        name="moe_experts",
    )(block_e, row_token, n_active, h_all, w_gate, w_up, w_down)


def _combine_kernel(pos_ref, y_hbm, x_ref, r_ref, g_ref, o_ref, buf, sem, *, tok0, tm):
    i = pl.program_id(0)
    n = pl.num_programs(0)

    def gather(tile, slot):
        for r in range(tm):
            for k in range(TOP_K):
                p = pos_ref[(tok0 + tile * tm + r) * TOP_K + k]
                _row_copy(y_hbm, p, buf.at[slot], k * tm + r, sem.at[slot]).start()

    @pl.when(i == 0)
    def _():
        gather(0, 0)

    @pl.when(i + 1 < n)
    def _():
        gather(i + 1, (i + 1) % 2)

    slot = i % 2
    for r in range(TOP_K * tm):
        _row_copy(y_hbm, 0, buf.at[slot], r, sem.at[slot]).wait()
    route = r_ref[...]
    moe = buf[slot, pl.ds(0, tm), :] * route[:, 2:3] + buf[slot, pl.ds(tm, tm), :] * route[:, 3:4]
    o_ref[...] = x_ref[...] + g_ref[...] * moe


def _combine(g, tok0, pos, y_rows, x, route, mod):
    tm = min(COMBINE_ROWS, g.tm)
    gc = g._replace(tm=tm)
    nt = gc.nt
    if g.L == 1:
        gate_spec = pl.BlockSpec((tm, D_MODEL), lambda i, pos: (i, 5))
    else:
        gate_spec = pl.BlockSpec((None, 1, D_MODEL), lambda i, pos: (i // nt, 0, 5))
    return pl.pallas_call(
        functools.partial(_combine_kernel, tok0=tok0, tm=tm),
        grid_spec=pltpu.PrefetchScalarGridSpec(
            num_scalar_prefetch=1,
            grid=(gc.n_tiles,),
            in_specs=[pl.BlockSpec(memory_space=pl.ANY),
                      pl.BlockSpec((tm, D_MODEL), lambda i, pos: (i, 0)),
                      pl.BlockSpec((tm, LANE), lambda i, pos: (i, 0)),
                      gate_spec],
            out_specs=pl.BlockSpec((tm, D_MODEL), lambda i, pos: (i, 0)),
            scratch_shapes=[pltpu.VMEM((2, TOP_K * tm, D_MODEL), F32), pltpu.SemaphoreType.DMA((2,))]),
        out_shape=jax.ShapeDtypeStruct((g.T, D_MODEL), F32),
        compiler_params=_params("arbitrary"),
        name="moe_combine",
    )(pos, y_rows, x, route, mod)


def _dispatch_plan(route_all):
    rb = MOE_ROWS
    n_tok = route_all.shape[0]
    n_assign = n_tok * TOP_K
    n_blocks = -(-(n_assign + N_EXPERTS * (rb - 1)) // rb)
    flat_e = route_all[:, :TOP_K].astype(jnp.int32).reshape(-1)
    order = jnp.argsort(flat_e).astype(jnp.int32)
    sorted_e = flat_e[order]
    counts = jnp.bincount(flat_e, length=N_EXPERTS).astype(jnp.int32)
    padded = (counts + rb - 1) // rb * rb
    pad_end = jnp.cumsum(padded)
    pad_start = pad_end - padded
    start = jnp.cumsum(counts) - counts
    dest = pad_start[sorted_e] + jnp.arange(n_assign, dtype=jnp.int32) - start[sorted_e]
    row_token = jnp.zeros((n_blocks * rb,), jnp.int32).at[dest].set(order // TOP_K)
    pos = jnp.zeros((n_assign,), jnp.int32).at[order].set(dest)
    n_active = (pad_end[-1] // rb).astype(jnp.int32)
    blk = jnp.minimum(jnp.arange(n_blocks, dtype=jnp.int32), n_active - 1)
    block_e = jnp.minimum(jnp.searchsorted(pad_end, blk * rb, side='right'), N_EXPERTS - 1).astype(jnp.int32)
    return block_e, row_token, n_active.reshape(1), pos


def _layer(layer, groups, xs, mods, states, p):
    w_in = p['w_in']
    hs, mixes, new_states = [], [], []
    for g, x, mod, st in zip(groups, xs, mods, states):
        s5_re, s5_im, dn_s, dn_conv, lru_h, lru_conv = st
        single = g.L == 1
        h = _norm_mod(g, x, p['norm1_g'][layer], mod, 0, 1)
        u_tb = _proj(g, h, w_in, layer, 0, BRANCH_W, 512, tb=True)
        qkv = _proj(g, h, w_in, layer, COL_QKV, 3 * DN_W, 512)
        ab = _proj(g, h, w_in, layer, COL_AB, LANE, LANE)
        z = _proj(g, h, p['w_rest'], layer, 0, DN_W, 512)
        xg_tb = _proj(g, h, p['w_rest'], layer, DN_W, 2 * LRU_W, 512, tb=True)

        ya, n_re, n_im = _s5(g, u_tb, s5_re, s5_im, p['s5_disc'], p['s5_cr'], p['s5_ci'], p['s5_d'],
                             p['s5_w_glu'], p['s5_b_glu'], layer)
        if single:
            yb, n_dn = _dn_step(g, qkv, dn_conv, z, ab, p['dn_conv_w'], p['dn_prm'], p['dn_norm_g'], dn_s, layer)
            n_dn_conv = jnp.concatenate([dn_conv[:, 1:], qkv[:, None, :]], axis=1)
            lru_cb = jnp.swapaxes(lru_conv, 0, 1).reshape((CONV_W - 1) * g.B, LRU_W)
            n_lru_conv = jnp.concatenate([lru_conv[:, 1:], xg_tb[:, None, :LRU_W]], axis=1)
        else:
            yb, n_dn = _dn_seq(g, qkv, z, ab, p['dn_conv_w'], p['dn_prm'], p['dn_norm_g'], layer)
            n_dn_conv = qkv.reshape(g.B, g.L, 3 * DN_W)[:, g.L - (CONV_W - 1):]
            lru_cb = jnp.zeros(((CONV_W - 1) * g.B, LRU_W), F32)
            n_lru_conv = jnp.swapaxes(
                xg_tb.reshape(g.L, g.B, 2 * LRU_W)[g.L - (CONV_W - 1):, :, :LRU_W], 0, 1)
        yc, n_lru = _lru(g, xg_tb, lru_cb, lru_h, p, layer, reset_first=not single)

        merged = _merge(g, h, ya, yb, yc, p['w_rest'], p['w_branch'], layer)
        x1 = _out_proj(g, merged, p['w_out'], layer, x, mod, 2)
        h2, route = _router(g, x1, p['norm2_g'][layer], mod, p['w_route'], p['b_route'], layer)
        hs.append((x1, h2, route))
        new_states.append((n_re.reshape(g.B, S5_GROUPS, S5_STATE), n_im.reshape(g.B, S5_GROUPS, S5_STATE),
                           n_dn, n_dn_conv, n_lru, n_lru_conv))

    h_all = jnp.concatenate([t[1] for t in hs], axis=0)
    route_all = jnp.concatenate([t[2] for t in hs], axis=0)
    block_e, row_token, n_active, pos = _dispatch_plan(route_all)
    y_rows = _experts(h_all, block_e, row_token, n_active, p['w_e_gate'], p['w_e_up'], p['w_e_down'], layer)
    outs, tok0 = [], 0
    for g, (x1, _, route), mod in zip(groups, hs, mods):
        outs.append(_combine(g, tok0, pos, y_rows, x1, route, mod))
        tok0 += g.T
    return outs, new_states


def kernel(x_prompt, x_sample, c_prompt, c_sample, state_s5_re, state_s5_im, state_dn, state_dn_conv, state_lru, state_lru_conv, final_g, norm1_g, norm2_g, w_ada, b_ada, w_in, s5_log_dt, s5_a_re, s5_a_im, s5_b_re, s5_b_im, s5_c_re, s5_c_im, s5_d, s5_w_glu, s5_b_glu, dn_conv_w, dn_a_log, dn_dt_bias, dn_norm_g, lru_conv_w, lru_conv_b, lru_w_a, lru_b_a, lru_w_x, lru_b_x, lru_lambda, w_branch, w_out, w_rg, b_rg, w_re, b_re, w_e_gate, w_e_up, w_e_down):
    depth = w_in.shape[0]
    bp, lp, _ = x_prompt.shape
    bs, ls, _ = x_sample.shape
    assert ls == 1 and bp == HALF and bs % SUBLANE == 0 and lp % 256 == 0
    gp = _Group(bp, lp, min(lp, 1024))
    gs = _Group(bs, 1, bs)
    groups = (gp, gs)

    pad = (-bp) % SUBLANE
    c_all = jnp.concatenate([c_prompt, jnp.zeros((pad, D_MODEL), F32), c_sample], axis=0)
    mod_all = _ada_mod(c_all, w_ada, b_ada)

    prm = jnp.zeros((depth, 2, LANE), F32)
    prm = prm.at[:, 0, :DN_HEADS].set(dn_a_log).at[:, 1, :DN_HEADS].set(dn_dt_bias)
    w_route = jnp.concatenate([w_rg, w_re, jnp.zeros((depth, D_MODEL, LANE - N_GROUPS - N_EXPERTS), F32)], axis=-1)
    b_route = jnp.concatenate([b_rg, b_re, jnp.zeros((depth, LANE - N_GROUPS - N_EXPERTS), F32)], axis=-1)
    p = dict(norm1_g=norm1_g, norm2_g=norm2_g, w_in=w_in, w_rest=w_in[:, :, COL_REST:],
             s5_disc=_s5_discretize(s5_log_dt, s5_a_re, s5_a_im, s5_b_re, s5_b_im),
             s5_cr=_s5_out_matrix(s5_c_re), s5_ci=_s5_out_matrix(s5_c_im),
             s5_d=s5_d, s5_w_glu=s5_w_glu, s5_b_glu=s5_b_glu,
             dn_conv_w=dn_conv_w, dn_prm=prm, dn_norm_g=dn_norm_g,
             lru_conv_w=lru_conv_w, lru_conv_b=lru_conv_b, lru_wa_bd=_block_diag(lru_w_a), lru_b_a=lru_b_a,
             lru_wx_bd=_block_diag(lru_w_x), lru_b_x=lru_b_x, lru_lambda=lru_lambda,
             w_branch=w_branch, w_out=w_out, w_route=w_route, b_route=b_route.reshape(depth, 1, LANE),
             w_e_gate=w_e_gate, w_e_up=w_e_up, w_e_down=w_e_down)

    xs = [x_prompt.reshape(gp.T, D_MODEL), x_sample.reshape(gs.T, D_MODEL)]
    zero = lambda *s: jnp.zeros(s, F32)
    per_layer = []
    for l in range(depth):
        mods = [mod_all[l, :bp].reshape(bp, 1, 6 * D_MODEL), mod_all[l, bp + pad:]]
        states = [(zero(bp, S5_N), zero(bp, S5_N), None, None, zero(bp, LRU_W), None),
                  (state_s5_re[l].reshape(bs, S5_N), state_s5_im[l].reshape(bs, S5_N), state_dn[l],
                   state_dn_conv[l], state_lru[l], state_lru_conv[l])]
        xs, new_states = _layer(l, groups, xs, mods, states, p)
        per_layer.append(new_states)

    y_prompt = _final_norm(gp, xs[0], final_g).reshape(bp, lp, D_MODEL)
    y_sample = _final_norm(gs, xs[1], final_g).reshape(bs, 1, D_MODEL)
    stack = lambda gi: tuple(jnp.stack([per_layer[l][gi][k] for l in range(depth)]) for k in range(6))
    return (y_prompt, y_sample) + stack(0) + stack(1)
```

```python
import functools
import math
from typing import NamedTuple

import jax
import jax.numpy as jnp
from jax import lax
from jax.experimental import pallas as pl
from jax.experimental.pallas import tpu as pltpu

F32 = jnp.float32
MXU_DTYPE = jnp.bfloat16

D_MODEL = 2048
BRANCH_W = D_MODEL // 4
S5_GROUP = 16
S5_GROUPS = BRANCH_W // S5_GROUP
S5_STATE = 64
S5_N = S5_GROUPS * S5_STATE
S5_BLOCKS = 4
DN_HEADS = 4
DN_HEAD_DIM = 128
DN_W = DN_HEADS * DN_HEAD_DIM
DN_CHUNK = 64
CONV_W = 4
LRU_W = BRANCH_W
LRU_BLOCKS = 4
LRU_C = 8.0
N_GROUPS = 4
EXPERTS_PER_GROUP = 8
N_EXPERTS = N_GROUPS * EXPERTS_PER_GROUP
TOP_K = 2
D_EXPERT = D_MODEL // 4
NORM_EPS = 1e-6
PAST_LEN = 16384

COL_AB = BRANCH_W + 3 * DN_W
COL_REST = COL_AB + 2 * DN_HEADS
REST_GATE = DN_W + 2 * LRU_W

LANE = 128
SUBLANE = 8
VMEM_LIMIT = 56 * 1024 * 1024
SEQ_CHUNK = 256
MOE_ROWS = 256
COMBINE_ROWS = 128


class _Group(NamedTuple):
    B: int
    L: int
    tm: int

    @property
    def T(self):
        return self.B * self.L

    @property
    def nt(self):
        return max(self.L // self.tm, 1)

    @property
    def n_tiles(self):
        return self.T // self.tm


def _params(*sem):
    return pltpu.CompilerParams(dimension_semantics=sem, vmem_limit_bytes=VMEM_LIMIT)


def _mm(a, b):
    return jnp.dot(a.astype(MXU_DTYPE), b.astype(MXU_DTYPE), preferred_element_type=F32)


def _mm_nt(a, b):
    return lax.dot_general(a.astype(MXU_DTYPE), b.astype(MXU_DTYPE), (((1,), (1,)), ((), ())),
                           preferred_element_type=F32)


def _mm_tn(a, b):
    return lax.dot_general(a.astype(MXU_DTYPE), b.astype(MXU_DTYPE), (((0,), (0,)), ((), ())),
                           preferred_element_type=F32)


def _silu(x):
    return x * jax.nn.sigmoid(x)


def _gelu_tanh(x):
    return 0.5 * x * (1.0 + jnp.tanh(math.sqrt(2.0 / math.pi) * (x + 0.044715 * (x * x * x))))


def _softplus(x):
    return jnp.maximum(x, 0.0) + jnp.log1p(jnp.exp(-jnp.abs(x)))


def _rms(x, gain):
    return x * lax.rsqrt(jnp.mean(x * x, axis=-1, keepdims=True) + NORM_EPS) * gain


def _row_iota(shape):
    return lax.broadcasted_iota(jnp.int32, shape, len(shape) - 2)


def _last_row(x):
    return jnp.broadcast_to(x[SUBLANE - 1:SUBLANE], x.shape)


def _nat_spec(g, tn, col0=0):
    return pl.BlockSpec((g.tm, tn), lambda i, j: (i, col0 + j))


def _mod_spec(g, k, tn):
    nb, nt = D_MODEL // tn, g.nt
    if g.L == 1:
        return pl.BlockSpec((g.tm, tn), lambda i, j: (i, k * nb + j))
    return pl.BlockSpec((None, 1, tn), lambda i, j: (i // nt, 0, k * nb + j))


def _ada_kernel(c_ref, w_ref, b_ref, o_ref):
    o_ref[...] = _mm(_silu(c_ref[...]), w_ref[...]) + b_ref[...]


def _ada_mod(c_all, w_ada, b_ada):
    depth, _, n6 = w_ada.shape
    rows = c_all.shape[0]
    tn = 1024
    return pl.pallas_call(
        _ada_kernel,
        grid=(depth, n6 // tn),
        in_specs=[pl.BlockSpec((rows, D_MODEL), lambda l, j: (0, 0)),
                  pl.BlockSpec((None, D_MODEL, tn), lambda l, j: (l, 0, j)),
                  pl.BlockSpec((None, 1, tn), lambda l, j: (l, 0, j))],
        out_specs=pl.BlockSpec((None, rows, tn), lambda l, j: (l, 0, j)),
        out_shape=jax.ShapeDtypeStruct((depth, rows, n6), F32),
        compiler_params=_params("arbitrary", "arbitrary"),
        name="ada_mod",
    )(c_all, w_ada, b_ada.reshape(depth, 1, n6))


def _norm_mod_kernel(x_ref, g_ref, sc_ref, sh_ref, o_ref):
    y = _rms(x_ref[...], g_ref[...])
    o_ref[...] = (y * (1.0 + sc_ref[...]) + sh_ref[...]).astype(o_ref.dtype)


def _norm_mod(g, x, gain, mod, k_shift, k_scale):
    gm = g._replace(tm=min(g.tm, 512))
    return pl.pallas_call(
        _norm_mod_kernel,
        grid=(gm.n_tiles, 1),
        in_specs=[_nat_spec(gm, D_MODEL),
                  pl.BlockSpec((1, D_MODEL), lambda i, j: (0, 0)),
                  _mod_spec(gm, k_scale, D_MODEL),
                  _mod_spec(gm, k_shift, D_MODEL)],
        out_specs=_nat_spec(gm, D_MODEL),
        out_shape=jax.ShapeDtypeStruct((g.T, D_MODEL), MXU_DTYPE),
        compiler_params=_params("arbitrary", "arbitrary"),
        name="norm_mod",
    )(x, gain.reshape(1, D_MODEL), mod, mod)


def _final_norm_kernel(x_ref, g_ref, o_ref):
    o_ref[...] = _rms(x_ref[...], g_ref[...])


def _final_norm(g, x, gain):
    gm = g._replace(tm=min(g.tm, 512))
    return pl.pallas_call(
        _final_norm_kernel,
        grid=(gm.n_tiles, 1),
        in_specs=[_nat_spec(gm, D_MODEL), pl.BlockSpec((1, D_MODEL), lambda i, j: (0, 0))],
        out_specs=_nat_spec(gm, D_MODEL),
        out_shape=jax.ShapeDtypeStruct((g.T, D_MODEL), F32),
        compiler_params=_params("arbitrary", "arbitrary"),
        name="final_norm",
    )(x, gain.reshape(1, D_MODEL))


def _proj_kernel(a_ref, w_ref, o_ref):
    o_ref[...] = _mm(a_ref[...], w_ref[...]).astype(o_ref.dtype)


def _proj(g, a, w, layer, col0, n, tn):
    cb = col0 // tn
    return pl.pallas_call(
        _proj_kernel,
        grid=(g.n_tiles, n // tn),
        in_specs=[pl.BlockSpec((g.tm, D_MODEL), lambda i, j: (i, 0)),
                  pl.BlockSpec((None, D_MODEL, tn), lambda i, j: (layer, 0, cb + j))],
        out_specs=_nat_spec(g, tn),
        out_shape=jax.ShapeDtypeStruct((g.T, n), F32),
        compiler_params=_params("arbitrary", "arbitrary"),
        name="in_proj",
    )(a, w)


def _out_proj_kernel(a_ref, w_ref, x_ref, g_ref, o_ref):
    o_ref[...] = x_ref[...] + g_ref[...] * _mm(a_ref[...], w_ref[...])


def _out_proj(g, merged, w_out, layer, x, mod, k_gate):
    tn = 512
    return pl.pallas_call(
        _out_proj_kernel,
        grid=(g.n_tiles, D_MODEL // tn),
        in_specs=[pl.BlockSpec((g.tm, D_MODEL), lambda i, j: (i, 0)),
                  pl.BlockSpec((None, D_MODEL, tn), lambda i, j: (layer, 0, j)),
                  _nat_spec(g, tn),
                  _mod_spec(g, k_gate, tn)],
        out_specs=_nat_spec(g, tn),
        out_shape=jax.ShapeDtypeStruct((g.T, D_MODEL), F32),
        compiler_params=_params("arbitrary", "arbitrary"),
        name="out_proj",
    )(merged, w_out, x, mod)


def _merge_kernel(h_ref, ya_ref, yb_ref, yc_ref, wg0, wg1, wg2, wb0, wb1, wb2, o_ref):
    h = h_ref[...]
    acc = None
    for y_ref, wg, wb in ((ya_ref, wg0, wb0), (yb_ref, wg1, wb1), (yc_ref, wg2, wb2)):
        t = jax.nn.sigmoid(_mm(h, wg[...])) * _mm(y_ref[...], wb[...])
        acc = t if acc is None else acc + t
    o_ref[...] = acc.astype(o_ref.dtype)


def _merge(g, h, ya, yb, yc, w_rest, w_branch, layer):
    tn = 256
    gate_specs = [pl.BlockSpec((None, D_MODEL, tn),
                               functools.partial(lambda i, j, c: (layer, 0, c + j), c=(REST_GATE + n * D_MODEL) // tn))
                  for n in range(3)]
    br_specs = [pl.BlockSpec((None, None, BRANCH_W, tn), functools.partial(lambda i, j, n: (layer, n, 0, j), n=n))
                for n in range(3)]
    y_spec = pl.BlockSpec((g.tm, BRANCH_W), lambda i, j: (i, 0))
    return pl.pallas_call(
        _merge_kernel,
        grid=(g.n_tiles, D_MODEL // tn),
        in_specs=[pl.BlockSpec((g.tm, D_MODEL), lambda i, j: (i, 0)), y_spec, y_spec, y_spec] + gate_specs + br_specs,
        out_specs=_nat_spec(g, tn),
        out_shape=jax.ShapeDtypeStruct((g.T, D_MODEL), MXU_DTYPE),
        compiler_params=_params("arbitrary", "arbitrary"),
        name="merge",
    )(h, ya, yb, yc, w_rest, w_rest, w_rest, w_branch, w_branch, w_branch)


def _s5_disc_kernel(ldt_ref, ar_ref, ai_ref, br_ref, bi_ref, lr_ref, li_ref, bbr_ref, bbi_ref):
    dt = jnp.exp(ldt_ref[...])
    ar, ai = ar_ref[...], ai_ref[...]
    mag = jnp.exp(ar * dt)
    lr = mag * jnp.cos(ai * dt)
    li = mag * jnp.sin(ai * dt)
    den = ar * ar + ai * ai
    fr = ((lr - 1.0) * ar + li * ai) / den
    fi = (li * ar - (lr - 1.0) * ai) / den
    lr_ref[...] = lr
    li_ref[...] = li
    br, bi = br_ref[...], bi_ref[...]
    bbr_ref[...] = fr * br - fi * bi
    bbi_ref[...] = fr * bi + fi * br


def _s5_discretize(log_dt, a_re, a_im, b_re, b_im):
    depth = log_dt.shape[0]
    n = depth * S5_GROUPS
    rows = n * S5_GROUP
    rep = lambda v, w: jnp.repeat(v.reshape(n, w), S5_GROUP, axis=0)
    brt = jnp.swapaxes(b_re, -1, -2).reshape(rows, S5_STATE)
    bit = jnp.swapaxes(b_im, -1, -2).reshape(rows, S5_STATE)
    sds = jax.ShapeDtypeStruct
    lr, li, bbr, bbi = pl.pallas_call(
        _s5_disc_kernel,
        out_shape=tuple(sds((rows, S5_STATE), F32) for _ in range(4)),
        name="s5_discretize",
    )(rep(log_dt, 1), rep(a_re, S5_STATE), rep(a_im, S5_STATE), brt, bit)
    gpb = S5_GROUPS // S5_BLOCKS
    eye = jnp.eye(gpb, dtype=F32)

    def blocks_in(m):
        m = m.reshape(depth, S5_BLOCKS, gpb, S5_GROUP, S5_STATE)
        m = jnp.einsum('ljgnp,gh->ljgnhp', m, eye)
        return m.reshape(depth, S5_BLOCKS, gpb * S5_GROUP, gpb * S5_STATE).astype(MXU_DTYPE)

    lam = lambda v: v.reshape(n, S5_GROUP, S5_STATE)[:, 0].reshape(depth, 1, S5_N)
    return lam(lr), lam(li), blocks_in(bbr), blocks_in(bbi)


def _s5_out_blocks(c):
    depth = c.shape[0]
    gpb = S5_GROUPS // S5_BLOCKS
    eye = jnp.eye(gpb, dtype=F32)
    m = jnp.einsum('ljgnp,gh->ljgphn', c.reshape(depth, S5_BLOCKS, gpb, S5_GROUP, S5_STATE), eye)
    return m.reshape(depth, S5_BLOCKS, gpb * S5_STATE, gpb * S5_GROUP).astype(MXU_DTYPE)


def _s5_in(u, bb_ref):
    cw, sw = BRANCH_W // S5_BLOCKS, S5_N // S5_BLOCKS
    return [_mm(u[:, j * cw:(j + 1) * cw], bb_ref[j]) for j in range(S5_BLOCKS)], sw


def _s5_out(hr, hi, u, cr_ref, ci_ref, d_ref, wglu_ref, bglu_ref):
    sw = S5_N // S5_BLOCKS
    y = jnp.concatenate([_mm(hr[:, j * sw:(j + 1) * sw], cr_ref[j]) - _mm(hi[:, j * sw:(j + 1) * sw], ci_ref[j])
                         for j in range(S5_BLOCKS)], axis=1)
    y = _gelu_tanh(y + d_ref[...] * u)
    return y * jax.nn.sigmoid(_mm(y, wglu_ref[...]) + bglu_ref[...])


def _scan_tiles(n_tiles, cw, width, tile_fn, load_state, store_state):
    for c in range(width // cw):
        cs = pl.ds(c * cw, cw)

        def body(k, carry, cs=cs):
            return tile_fn(pl.ds(pl.multiple_of(k * SUBLANE, SUBLANE), SUBLANE), cs, carry)

        store_state(cs, lax.fori_loop(0, n_tiles, body, load_state(cs)))


def _s5_seq_kernel(u_ref, bbr_ref, bbi_ref, lr_ref, li_ref, cr_ref, ci_ref, d_ref, wglu_ref, bglu_ref,
                   y_ref, hlr_ref, hli_ref, xr_s, xi_s, cr_s, ci_s, pw_s):
    c = pl.program_id(1)
    lc = u_ref.shape[0]

    @pl.when(c == 0)
    def _():
        cr_s[...] = jnp.zeros_like(cr_s)
        ci_s[...] = jnp.zeros_like(ci_s)
        lr, li = lr_ref[...], li_ref[...]
        row = _row_iota((SUBLANE, S5_N))
        pr, pi = lr, li
        acc_r = jnp.zeros((SUBLANE, S5_N), F32)
        acc_i = jnp.zeros((SUBLANE, S5_N), F32)
        for r in range(SUBLANE):
            acc_r = jnp.where(row == r, pr, acc_r)
            acc_i = jnp.where(row == r, pi, acc_i)
            if r + 1 in (1, 2, 4):
                k = (1, 2, 4).index(r + 1)
                pw_s[2 * k] = jnp.where(row >= r + 1, pr, 0.0)
                pw_s[2 * k + 1] = jnp.where(row >= r + 1, pi, 0.0)
            pr, pi = pr * lr - pi * li, pr * li + pi * lr
        pw_s[6] = acc_r
        pw_s[7] = acc_i

    u = u_ref[...]
    xr, sw = _s5_in(u, bbr_ref)
    xi, _ = _s5_in(u, bbi_ref)
    for j in range(S5_BLOCKS):
        xr_s[:, pl.ds(j * sw, sw)] = xr[j]
        xi_s[:, pl.ds(j * sw, sw)] = xi[j]

    def tile(rows, cs, carry):
        pr0, pi0 = carry
        hr, hi = xr_s[rows, cs], xi_s[rows, cs]
        for k, s in enumerate((1, 2, 4)):
            ar, ai = pw_s[2 * k, :, cs], pw_s[2 * k + 1, :, cs]
            sr, si = pltpu.roll(hr, s, 0), pltpu.roll(hi, s, 0)
            hr, hi = hr + ar * sr - ai * si, hi + ar * si + ai * sr
        ar, ai = pw_s[6, :, cs], pw_s[7, :, cs]
        hr, hi = hr + ar * pr0 - ai * pi0, hi + ar * pi0 + ai * pr0
        xr_s[rows, cs] = hr
        xi_s[rows, cs] = hi
        return _last_row(hr), _last_row(hi)

    def load(cs):
        return cr_s[:, cs], ci_s[:, cs]

    def store(cs, carry):
        cr_s[:, cs], ci_s[:, cs] = carry

    _scan_tiles(lc // SUBLANE, 512, S5_N, tile, load, store)

    y_ref[...] = _s5_out(xr_s[...], xi_s[...], u, cr_ref, ci_ref, d_ref, wglu_ref, bglu_ref).astype(y_ref.dtype)

    @pl.when(c == pl.num_programs(1) - 1)
    def _():
        hlr_ref[...] = cr_s[pl.ds(0, 1), :]
        hli_ref[...] = ci_s[pl.ds(0, 1), :]


def _s5_step_kernel(u_ref, h0r_ref, h0i_ref, bbr_ref, bbi_ref, lr_ref, li_ref, cr_ref, ci_ref, d_ref, wglu_ref,
                    bglu_ref, y_ref, hlr_ref, hli_ref):
    u = u_ref[...]
    xr, _ = _s5_in(u, bbr_ref)
    xi, _ = _s5_in(u, bbi_ref)
    lr, li, sr, si = lr_ref[...], li_ref[...], h0r_ref[...], h0i_ref[...]
    hr = lr * sr - li * si + jnp.concatenate(xr, axis=1)
    hi = lr * si + li * sr + jnp.concatenate(xi, axis=1)
    hlr_ref[...] = hr
    hli_ref[...] = hi
    y_ref[...] = _s5_out(hr, hi, u, cr_ref, ci_ref, d_ref, wglu_ref, bglu_ref).astype(y_ref.dtype)


def _s5(g, uq, h0r, h0i, disc, cr, ci, d, w_glu, b_glu, layer):
    lam_r, lam_i, bbr, bbi = disc
    cw, sw = BRANCH_W // S5_BLOCKS, S5_N // S5_BLOCKS
    sds = jax.ShapeDtypeStruct
    seq = g.L > 1
    lc = min(g.L, SEQ_CHUNK) if seq else g.B
    nc = g.L // lc if seq else 1
    grid = (g.B, nc) if seq else (1, 1)
    row = (lambda b, c: (b * nc + c, 0))
    lay3 = lambda b, c: (layer, 0, 0)
    lay4 = lambda b, c: (layer, 0, 0, 0)
    w_specs = [pl.BlockSpec((None, S5_BLOCKS, cw, sw), lay4), pl.BlockSpec((None, S5_BLOCKS, cw, sw), lay4),
               pl.BlockSpec((None, 1, S5_N), lay3), pl.BlockSpec((None, 1, S5_N), lay3),
               pl.BlockSpec((None, S5_BLOCKS, sw, cw), lay4), pl.BlockSpec((None, S5_BLOCKS, sw, cw), lay4),
               pl.BlockSpec((None, 1, BRANCH_W), lay3),
               pl.BlockSpec((None, BRANCH_W, BRANCH_W), lay3),
               pl.BlockSpec((None, 1, BRANCH_W), lay3)]
    w_args = (bbr, bbi, lam_r, lam_i, cr, ci, d.reshape(-1, 1, BRANCH_W), w_glu, b_glu.reshape(-1, 1, BRANCH_W))
    if seq:
        st_spec = pl.BlockSpec((None, 1, S5_N), lambda b, c: (b, 0, 0))
        y, hlr, hli = pl.pallas_call(
            _s5_seq_kernel,
            grid=grid,
            in_specs=[pl.BlockSpec((lc, BRANCH_W), row)] + w_specs,
            out_specs=(pl.BlockSpec((lc, BRANCH_W), row), st_spec, st_spec),
            out_shape=(sds((g.T, BRANCH_W), MXU_DTYPE), sds((g.B, 1, S5_N), F32), sds((g.B, 1, S5_N), F32)),
            scratch_shapes=[pltpu.VMEM((lc, S5_N), F32), pltpu.VMEM((lc, S5_N), F32),
                            pltpu.VMEM((SUBLANE, S5_N), F32), pltpu.VMEM((SUBLANE, S5_N), F32),
                            pltpu.VMEM((8, SUBLANE, S5_N), F32)],
            compiler_params=_params("arbitrary", "arbitrary"),
            name="s5_ssm",
        )(uq, *w_args)
        return y, hlr.reshape(g.B, S5_N), hli.reshape(g.B, S5_N)
    st_spec = pl.BlockSpec((g.B, S5_N), lambda b, c: (0, 0))
    return pl.pallas_call(
        _s5_step_kernel,
        grid=grid,
        in_specs=[pl.BlockSpec((lc, BRANCH_W), row), st_spec, st_spec] + w_specs,
        out_specs=(pl.BlockSpec((lc, BRANCH_W), row), st_spec, st_spec),
        out_shape=(sds((g.T, BRANCH_W), MXU_DTYPE), sds((g.B, S5_N), F32), sds((g.B, S5_N), F32)),
        compiler_params=_params("arbitrary", "arbitrary"),
        name="s5_ssm",
    )(uq, h0r, h0i, *w_args)


def _lru_gates(xc, wa_ref, ba_ref, wx_ref, bx_ref, lam_ref):
    r = jax.nn.sigmoid(_mm(xc, wa_ref[...]) + ba_ref[...])
    ig = jax.nn.sigmoid(_mm(xc, wx_ref[...]) + bx_ref[...])
    log_a = -LRU_C * r * _softplus(-lam_ref[...])
    return jnp.exp(log_a), jnp.sqrt(1.0 - jnp.exp(2.0 * log_a)), ig


def _lru_seq_kernel(x_ref, g_ref, cw_ref, cbias_ref, wa_ref, ba_ref, wx_ref, bx_ref, lam_ref,
                    y_ref, hl_ref, xbuf, a_s, b_s, h_s):
    c = pl.program_id(1)
    lc = x_ref.shape[0]
    pad = SUBLANE

    @pl.when(c == 0)
    def _():
        xbuf[pl.ds(0, pad), :] = jnp.zeros((pad, LRU_W), F32)
        h_s[...] = jnp.zeros_like(h_s)

    xbuf[pl.ds(pad, lc), :] = x_ref[...]
    xc = cbias_ref[...]
    for j in range(CONV_W):
        xc = xc + cw_ref[pl.ds(j, 1), :] * xbuf[pl.ds(pad - (CONV_W - 1) + j, lc), :]
    new_tail = xbuf[pl.ds(lc, pad), :]
    xbuf[pl.ds(0, pad), :] = new_tail

    a, mult, ig = _lru_gates(xc, wa_ref, ba_ref, wx_ref, bx_ref, lam_ref)
    first = jnp.logical_and(c == 0, _row_iota((lc, LRU_W)) == 0)
    a_s[...] = jnp.where(first, 0.0, a)
    b_s[...] = jnp.where(first, 1.0, mult) * (ig * xc)

    def tile(rows, cs, h0):
        a, b = a_s[rows, cs], b_s[rows, cs]
        row = _row_iota(a.shape)
        for s in (1, 2, 4):
            keep = row >= s
            b = a * jnp.where(keep, pltpu.roll(b, s, 0), 0.0) + b
            a = a * jnp.where(keep, pltpu.roll(a, s, 0), 1.0)
        h = a * h0 + b
        b_s[rows, cs] = h
        return _last_row(h)

    def store(cs, h):
        h_s[:, cs] = h

    _scan_tiles(lc // SUBLANE, LRU_W, LRU_W, tile, lambda cs: h_s[:, cs], store)
    y_ref[...] = (b_s[...] * _gelu_tanh(g_ref[...])).astype(y_ref.dtype)

    @pl.when(c == pl.num_programs(1) - 1)
    def _():
        hl_ref[...] = h_s[pl.ds(0, 1), :]


def _lru_step_kernel(x_ref, g_ref, cs_ref, h0_ref, cw_ref, cbias_ref, wa_ref, ba_ref, wx_ref, bx_ref, lam_ref,
                     y_ref, hl_ref):
    x = x_ref[...]
    xc = cbias_ref[...] + cw_ref[pl.ds(CONV_W - 1, 1), :] * x
    for j in range(CONV_W - 1):
        xc = xc + cw_ref[pl.ds(j, 1), :] * cs_ref[j]
    a, mult, ig = _lru_gates(xc, wa_ref, ba_ref, wx_ref, bx_ref, lam_ref)
    h = a * h0_ref[...] + mult * (ig * xc)
    hl_ref[...] = h
    y_ref[...] = (h * _gelu_tanh(g_ref[...])).astype(y_ref.dtype)


def _block_diag(w):
    depth, nb, k, _ = w.shape
    eye = jnp.eye(nb, dtype=w.dtype)
    return jnp.einsum('lnwv,nm->lnwmv', w, eye).reshape(depth, nb * k, nb * k).astype(MXU_DTYPE)


def _lru(g, rest, conv_state, h0, p, layer, pos0):
    sds = jax.ShapeDtypeStruct
    seq = g.L > 1
    lc = min(g.L, SEQ_CHUNK) if seq else g.B
    nc = g.L // lc if seq else 1
    lay3 = lambda b, c: (layer, 0, 0)
    vec = lambda v: v.reshape(-1, 1, LRU_W)
    w_specs = [pl.BlockSpec((None, CONV_W, LRU_W), lay3), pl.BlockSpec((None, 1, LRU_W), lay3),
               pl.BlockSpec((None, LRU_W, LRU_W), lay3), pl.BlockSpec((None, 1, LRU_W), lay3),
               pl.BlockSpec((None, LRU_W, LRU_W), lay3), pl.BlockSpec((None, 1, LRU_W), lay3),
               pl.BlockSpec((None, 1, LRU_W), lay3)]
    w_args = (p['lru_conv_w'], vec(p['lru_conv_b']), p['lru_wa_bd'], vec(p['lru_b_a']), p['lru_wx_bd'],
              vec(p['lru_b_x']), vec(p['lru_lambda']))
    x_spec = pl.BlockSpec((lc, LRU_W), lambda b, c: (b * nc + c, 1))
    g_spec = pl.BlockSpec((lc, LRU_W), lambda b, c: (b * nc + c, 2))
    y_spec = pl.BlockSpec((lc, LRU_W), lambda b, c: (b * nc + c, 0))
    if seq:
        assert pos0 == 0
        y, hl = pl.pallas_call(
            _lru_seq_kernel,
            grid=(g.B, nc),
            in_specs=[x_spec, g_spec] + w_specs,
            out_specs=(y_spec, pl.BlockSpec((None, 1, LRU_W), lambda b, c: (b, 0, 0))),
            out_shape=(sds((g.T, LRU_W), MXU_DTYPE), sds((g.B, 1, LRU_W), F32)),
            scratch_shapes=[pltpu.VMEM((lc + SUBLANE, LRU_W), F32), pltpu.VMEM((lc, LRU_W), F32),
                            pltpu.VMEM((lc, LRU_W), F32), pltpu.VMEM((SUBLANE, LRU_W), F32)],
            compiler_params=_params("arbitrary", "arbitrary"),
            name="rg_lru",
        )(rest, rest, *w_args)
        return y, hl.reshape(g.B, LRU_W)
    assert pos0 > 0
    st_spec = pl.BlockSpec((g.B, LRU_W), lambda b, c: (0, 0))
    return pl.pallas_call(
        _lru_step_kernel,
        grid=(1, 1),
        in_specs=[x_spec, g_spec, pl.BlockSpec((CONV_W - 1, g.B, LRU_W), lambda b, c: (0, 0, 0)), st_spec] + w_specs,
        out_specs=(y_spec, st_spec),
        out_shape=(sds((g.T, LRU_W), MXU_DTYPE), sds((g.B, LRU_W), F32)),
        compiler_params=_params("arbitrary", "arbitrary"),
        name="rg_lru",
    )(rest, rest, jnp.swapaxes(conv_state, 0, 1), h0, *w_args)


def _l2norm(x):
    return x * lax.rsqrt(jnp.sum(x * x, axis=-1, keepdims=True) + NORM_EPS)


def _dn_gates(ab, prm):
    g = -jnp.exp(prm[0:1, :]) * _softplus(ab + prm[1:2, :])
    return g, jax.nn.sigmoid(ab)


def _dn_seq_kernel(q_ref, k_ref, v_ref, z_ref, ab_ref, cw_ref, prm_ref, ng_ref, y_ref, sfin_ref, xbuf, s_s):
    c = pl.program_id(0)
    bsz, cs, _ = q_ref.shape
    hd = DN_HEAD_DIM
    rows = bsz * cs
    pad = SUBLANE

    @pl.when(c == 0)
    def _():
        xbuf[:, pl.ds(0, pad), :] = jnp.zeros((bsz, pad, 3 * DN_W), F32)
        s_s[...] = jnp.zeros_like(s_s)

    for s, ref in enumerate((q_ref, k_ref, v_ref)):
        xbuf[:, pl.ds(pad, cs), pl.ds(s * DN_W, DN_W)] = ref[...]
    conv = None
    for j in range(CONV_W):
        t = cw_ref[pl.ds(j, 1), :] * xbuf[:, pl.ds(pad - (CONV_W - 1) + j, cs), :]
        conv = t if conv is None else conv + t
    new_tail = xbuf[:, pl.ds(cs, pad), :]
    xbuf[:, pl.ds(0, pad), :] = new_tail
    act = _silu(conv).reshape(rows, 3 * DN_W)

    glog, beta_all = _dn_gates(ab_ref[...].reshape(rows, LANE), prm_ref[...])
    row = lax.broadcasted_iota(jnp.int32, (rows, rows), 0)
    col = lax.broadcasted_iota(jnp.int32, (rows, rows), 1)
    shift = cs.bit_length() - 1
    same = lax.shift_right_logical(row, shift) == lax.shift_right_logical(col, shift)
    causal = jnp.logical_and(same, row >= col)
    strict = jnp.logical_and(same, row > col)
    gc = jnp.dot(causal.astype(F32), glog, precision=lax.Precision.HIGHEST, preferred_element_type=F32)
    lane = lax.broadcasted_iota(jnp.int32, (rows, LANE), 1)
    eye = (row == col).astype(F32)
    ng = ng_ref[...]
    rblk = lax.shift_right_logical(lax.broadcasted_iota(jnp.int32, (rows, bsz * hd), 0), shift)
    cblk = lax.shift_right_logical(lax.broadcasted_iota(jnp.int32, (rows, bsz * hd), 1), hd.bit_length() - 1)
    own = rblk == cblk

    def diag_blocks(m):
        return jnp.concatenate([m[b * cs:(b + 1) * cs, b * hd:(b + 1) * hd] for b in range(bsz)], axis=0)

    for h in range(DN_HEADS):
        q = _l2norm(act[:, h * hd:(h + 1) * hd]) * (hd ** -0.5)
        k = _l2norm(act[:, DN_W + h * hd:DN_W + (h + 1) * hd])
        v = act[:, 2 * DN_W + h * hd:2 * DN_W + (h + 1) * hd]
        beta = beta_all[:, DN_HEADS + h:DN_HEADS + h + 1]
        gcol = gc[:, h:h + 1]
        sel = (lane == h).astype(F32)
        grow = lax.dot_general(sel, gc, (((1,), (1,)), ((), ())), precision=lax.Precision.HIGHEST,
                               preferred_element_type=F32)
        decay = jnp.exp(jnp.where(causal, gcol - grow, -jnp.inf))
        kb = k * beta
        nmat = jnp.where(strict, -(_mm_nt(kb, k) * decay), 0.0)
        rinv = eye + nmat
        xpow = nmat
        for _ in range(cs.bit_length() - 2):
            xpow = _mm(xpow, xpow)
            rinv = rinv + _mm(rinv, xpow)
        eg = jnp.exp(gcol)
        uw = _mm(rinv, jnp.concatenate([v * beta, kb * eg], axis=1))
        u, w = uw[:, :hd], uw[:, hd:]
        att = jnp.where(causal, _mm_nt(q, k) * decay, 0.0)
        s = s_s[h]
        ws_qs = _mm(jnp.concatenate([w, q * eg], axis=0), s)
        v_new = u - diag_blocks(ws_qs[:rows])
        o = diag_blocks(ws_qs[rows:]) + _mm(att, v_new)
        glast = [gc[(b + 1) * cs - 1:(b + 1) * cs, h:h + 1] for b in range(bsz)]
        glast_rows = jnp.concatenate([jnp.broadcast_to(t, (cs, 1)) for t in glast], axis=0)
        glast_lanes = jnp.concatenate([jnp.broadcast_to(t, (1, hd)) for t in glast], axis=1)
        v_bd = jnp.where(own, jnp.concatenate([v_new] * bsz, axis=1), 0.0)
        s_s[h] = s * jnp.exp(glast_lanes) + _mm_tn(k * jnp.exp(glast_rows - gcol), v_bd)
        o = _rms(o, ng) * _silu(z_ref[:, :, pl.ds(h * hd, hd)].reshape(rows, hd))
        y_ref[:, :, pl.ds(h * hd, hd)] = o.reshape(bsz, cs, hd).astype(y_ref.dtype)

    @pl.when(c == pl.num_programs(0) - 1)
    def _():
        for b in range(bsz):
            for h in range(DN_HEADS):
                sfin_ref[b, h] = s_s[h, :, pl.ds(b * hd, hd)]


def _dn_seq(g, uq, rest, ab, conv_w, prm, norm_g, layer):
    cs = DN_CHUNK
    assert cs & (cs - 1) == 0 and g.L % cs == 0
    nc = g.L // cs
    hd = DN_HEAD_DIM
    lay3 = lambda c: (layer, 0, 0)
    sds = jax.ShapeDtypeStruct
    sec = lambda s: pl.BlockSpec((g.B, cs, DN_W), lambda c: (0, c, s))
    y, sfin = pl.pallas_call(
        _dn_seq_kernel,
        grid=(nc,),
        in_specs=[sec(1), sec(2), sec(3), sec(0),
                  pl.BlockSpec((g.B, cs, LANE), lambda c: (0, c, 0)),
                  pl.BlockSpec((None, CONV_W, 3 * DN_W), lay3),
                  pl.BlockSpec((None, 2, LANE), lay3),
                  pl.BlockSpec((None, 1, hd), lay3)],
        out_specs=(pl.BlockSpec((g.B, cs, DN_W), lambda c: (0, c, 0)),
                   pl.BlockSpec((g.B, DN_HEADS, hd, hd), lambda c: (0, 0, 0, 0))),
        out_shape=(sds((g.B, g.L, DN_W), MXU_DTYPE), sds((g.B, DN_HEADS, hd, hd), F32)),
        scratch_shapes=[pltpu.VMEM((g.B, cs + SUBLANE, 3 * DN_W), F32),
                        pltpu.VMEM((DN_HEADS, hd, g.B * hd), F32)],
        compiler_params=_params("arbitrary"),
        name="deltanet_seq",
    )(*(3 * [uq.reshape(g.B, g.L, -1)]), rest.reshape(g.B, g.L, -1), ab.reshape(g.B, g.L, LANE), conv_w, prm,
      norm_g.reshape(-1, 1, hd))
    return y.reshape(g.T, DN_W), sfin


def _dn_step_kernel(q_ref, k_ref, v_ref, cq_ref, ck_ref, cv_ref, z_ref, ab_ref, cwq_ref, cwk_ref, cwv_ref,
                    prm_ref, ng_ref, s_ref, y_ref, so_ref, o_s, *, bsz):
    h = pl.program_id(0)
    hd = DN_HEAD_DIM

    def conv(x_ref, c_ref, w_ref):
        acc = w_ref[pl.ds(CONV_W - 1, 1), :] * x_ref[...]
        for j in range(CONV_W - 1):
            acc = acc + w_ref[pl.ds(j, 1), :] * c_ref[j]
        return _silu(acc)

    q = _l2norm(conv(q_ref, cq_ref, cwq_ref)) * (hd ** -0.5)
    k = _l2norm(conv(k_ref, ck_ref, cwk_ref))
    v = conv(v_ref, cv_ref, cwv_ref)
    glog, beta_all = _dn_gates(ab_ref[...], prm_ref[...])
    lane = lax.broadcasted_iota(jnp.int32, (bsz, LANE), 1)
    eg = jnp.exp(jnp.sum(jnp.where(lane == h, glog, 0.0), axis=-1, keepdims=True))
    beta = jnp.sum(jnp.where(lane == DN_HEADS + h, beta_all, 0.0), axis=-1, keepdims=True)
    att = jnp.sum(q * k, axis=-1, keepdims=True)
    kt = k.T
    qt = q.T
    for b in range(bsz):
        s = s_ref[b]
        kcol = kt[:, b:b + 1]
        ks = jnp.sum(s * kcol, axis=0, keepdims=True)
        qs = jnp.sum(s * qt[:, b:b + 1], axis=0, keepdims=True)
        eg_b, beta_b = eg[b:b + 1, :], beta[b:b + 1, :]
        v_new = beta_b * v[b:b + 1, :] - (beta_b * eg_b) * ks
        o_s[pl.ds(b, 1), :] = eg_b * qs + att[b:b + 1, :] * v_new
        so_ref[b] = s * eg_b + kcol * v_new
    y_ref[...] = (_rms(o_s[...], ng_ref[...]) * _silu(z_ref[...])).astype(y_ref.dtype)


def _dn_step(g, uq, conv_state, rest, ab, conv_w, prm, norm_g, s0_all, layer):
    bsz = g.B
    hd = DN_HEAD_DIM
    conv_t = jnp.swapaxes(conv_state, 0, 1)
    sec = lambda s: (lambda h: (0, (s + 1) * DN_HEADS + h))
    sec3 = lambda s: (lambda h: (0, 0, s * DN_HEADS + h))
    secw = lambda s: (lambda h: (layer, 0, s * DN_HEADS + h))
    lay3 = lambda h: (layer, 0, 0)
    sds = jax.ShapeDtypeStruct
    return pl.pallas_call(
        functools.partial(_dn_step_kernel, bsz=bsz),
        grid=(DN_HEADS,),
        in_specs=[pl.BlockSpec((bsz, hd), sec(s)) for s in range(3)]
                 + [pl.BlockSpec((CONV_W - 1, bsz, hd), sec3(s)) for s in range(3)]
                 + [pl.BlockSpec((bsz, hd), lambda h: (0, h)),
                    pl.BlockSpec((bsz, LANE), lambda h: (0, 0))]
                 + [pl.BlockSpec((None, CONV_W, hd), secw(s)) for s in range(3)]
                 + [pl.BlockSpec((None, 2, LANE), lay3),
                    pl.BlockSpec((None, 1, hd), lay3),
                    pl.BlockSpec((None, bsz, None, hd, hd), lambda h: (layer, 0, h, 0, 0))],
        out_specs=(pl.BlockSpec((bsz, hd), lambda h: (0, h)),
                   pl.BlockSpec((bsz, None, hd, hd), lambda h: (0, h, 0, 0))),
        out_shape=(sds((bsz, DN_W), MXU_DTYPE), sds((bsz, DN_HEADS, hd, hd), F32)),
        scratch_shapes=[pltpu.VMEM((bsz, hd), F32)],
        compiler_params=_params("arbitrary"),
        name="deltanet_step",
    )(uq, uq, uq, conv_t, conv_t, conv_t, rest, ab, conv_w, conv_w, conv_w, prm,
      norm_g.reshape(-1, 1, hd), s0_all)


def _router_kernel(x_ref, g_ref, sc_ref, sh_ref, wr_ref, br_ref, h_ref, r_ref):
    h = _rms(x_ref[...], g_ref[...]) * (1.0 + sc_ref[...]) + sh_ref[...]
    h_ref[...] = h
    logits = jnp.dot(h, wr_ref[...], precision=lax.Precision.HIGHEST, preferred_element_type=F32) + br_ref[...]
    lane = lax.broadcasted_iota(jnp.int32, logits.shape, 1).astype(F32)
    ninf = -jnp.inf
    big = float(LANE)
    lg = jnp.where(lane < N_GROUPS, logits, ninf)
    gmax = jnp.max(lg, axis=-1, keepdims=True)
    gsel = jnp.min(jnp.where(lg == gmax, lane, big), axis=-1, keepdims=True)
    pg = 1.0 / jnp.sum(jnp.where(lane < N_GROUPS, jnp.exp(logits - gmax), 0.0), axis=-1, keepdims=True)
    lo = N_GROUPS + EXPERTS_PER_GROUP * gsel
    le = jnp.where(jnp.logical_and(lane >= lo, lane < lo + EXPERTS_PER_GROUP), logits, ninf)
    v1 = jnp.max(le, axis=-1, keepdims=True)
    i1 = jnp.min(jnp.where(le == v1, lane, big), axis=-1, keepdims=True)
    le2 = jnp.where(lane == i1, ninf, le)
    v2 = jnp.max(le2, axis=-1, keepdims=True)
    i2 = jnp.min(jnp.where(le2 == v2, lane, big), axis=-1, keepdims=True)
    e2 = jnp.exp(v2 - v1)
    w1 = pg / (1.0 + e2)
    w2 = pg * e2 / (1.0 + e2)
    r_ref[...] = jnp.where(lane == 0, i1 - N_GROUPS,
                           jnp.where(lane == 1, i2 - N_GROUPS,
                                     jnp.where(lane == 2, w1, jnp.where(lane == 3, w2, 0.0))))


def _router(g, x, gain, mod, w_route, b_route, layer):
    gm = g._replace(tm=min(g.tm, 256))
    sds = jax.ShapeDtypeStruct
    return pl.pallas_call(
        _router_kernel,
        grid=(gm.n_tiles, 1),
        in_specs=[_nat_spec(gm, D_MODEL),
                  pl.BlockSpec((1, D_MODEL), lambda i, j: (0, 0)),
                  _mod_spec(gm, 4, D_MODEL),
                  _mod_spec(gm, 3, D_MODEL),
                  pl.BlockSpec((None, D_MODEL, LANE), lambda i, j: (layer, 0, 0)),
                  pl.BlockSpec((None, 1, LANE), lambda i, j: (layer, 0, 0))],
        out_specs=(_nat_spec(gm, D_MODEL), _nat_spec(gm, LANE)),
        out_shape=(sds((g.T, D_MODEL), F32), sds((g.T, LANE), F32)),
        compiler_params=_params("arbitrary", "arbitrary"),
        name="moe_router",
    )(x, gain.reshape(1, D_MODEL), mod, mod, w_route, b_route)


def _row_copy(src_hbm, row, dst, dst_row, sem):
    return pltpu.make_async_copy(src_hbm.at[pl.ds(row, 1)], dst.at[pl.ds(dst_row, 1)], sem)


def _experts_kernel(be_ref, tok_ref, nact_ref, h_hbm, wg_ref, wu_ref, wd_ref, y_ref, xbuf, sem, wgb, wub, wdb):
    b = pl.program_id(0)
    nact = nact_ref[0]
    rb = MOE_ROWS

    def gather(blk, slot):
        for r in range(rb):
            _row_copy(h_hbm, tok_ref[blk * rb + r], xbuf.at[slot], r, sem.at[slot]).start()

    @pl.when(b == 0)
    def _():
        gather(0, 0)

    @pl.when(b + 1 < nact)
    def _():
        gather(b + 1, (b + 1) % 2)

    @pl.when(jnp.logical_and(b < nact, jnp.logical_or(b == 0, be_ref[b] != be_ref[jnp.maximum(b - 1, 0)])))
    def _():
        wgb[...] = wg_ref[...].astype(MXU_DTYPE)
        wub[...] = wu_ref[...].astype(MXU_DTYPE)
        wdb[...] = wd_ref[...].astype(MXU_DTYPE)

    @pl.when(b < nact)
    def _():
        slot = b % 2
        for r in range(rb):
            _row_copy(h_hbm, 0, xbuf.at[slot], r, sem.at[slot]).wait()
        x = xbuf[slot].astype(MXU_DTYPE)
        hid = _silu(_mm(x, wgb[...])) * _mm(x, wub[...])
        y_ref[...] = _mm(hid, wdb[...])

    @pl.when(b >= nact)
    def _():
        y_ref[...] = jnp.zeros_like(y_ref)


def _experts(h_all, block_e, row_token, n_active, w_gate, w_up, w_down, layer):
    n_blocks = block_e.shape[0]
    rb = MOE_ROWS
    wspec = lambda shape: pl.BlockSpec((None, None) + shape, lambda b, be, tok, na: (layer, be[b], 0, 0))
    return pl.pallas_call(
        _experts_kernel,
        grid_spec=pltpu.PrefetchScalarGridSpec(
            num_scalar_prefetch=3,
            grid=(n_blocks,),
            in_specs=[pl.BlockSpec(memory_space=pl.ANY),
                      wspec((D_MODEL, D_EXPERT)), wspec((D_MODEL, D_EXPERT)), wspec((D_EXPERT, D_MODEL))],
            out_specs=pl.BlockSpec((rb, D_MODEL), lambda b, be, tok, na: (b, 0)),
            scratch_shapes=[pltpu.VMEM((2, rb, D_MODEL), F32), pltpu.SemaphoreType.DMA((2,)),
                            pltpu.VMEM((D_MODEL, D_EXPERT), MXU_DTYPE), pltpu.VMEM((D_MODEL, D_EXPERT), MXU_DTYPE),
                            pltpu.VMEM((D_EXPERT, D_MODEL), MXU_DTYPE)]),
        out_shape=jax.ShapeDtypeStruct((n_blocks * rb, D_MODEL), F32),
        compiler_params=_params("arbitrary"),
        name="moe_experts",
    )(block_e, row_token, n_active, h_all, w_gate, w_up, w_down)


def _combine_kernel(pos_ref, y_hbm, x_ref, r_ref, g_ref, o_ref, buf, sem, *, tok0, tm):
    i = pl.program_id(0)
    n = pl.num_programs(0)

    def gather(tile, slot):
        for r in range(tm):
            for k in range(TOP_K):
                p = pos_ref[(tok0 + tile * tm + r) * TOP_K + k]
                _row_copy(y_hbm, p, buf.at[slot], k * tm + r, sem.at[slot]).start()

    @pl.when(i == 0)
    def _():
        gather(0, 0)

    @pl.when(i + 1 < n)
    def _():
        gather(i + 1, (i + 1) % 2)

    slot = i % 2
    for r in range(TOP_K * tm):
        _row_copy(y_hbm, 0, buf.at[slot], r, sem.at[slot]).wait()
    route = r_ref[...]
    moe = buf[slot, pl.ds(0, tm), :] * route[:, 2:3] + buf[slot, pl.ds(tm, tm), :] * route[:, 3:4]
    o_ref[...] = x_ref[...] + g_ref[...] * moe


def _combine(g, tok0, pos, y_rows, x, route, mod):
    tm = min(COMBINE_ROWS, g.tm)
    gc = g._replace(tm=tm)
    nt = gc.nt
    if g.L == 1:
        gate_spec = pl.BlockSpec((tm, D_MODEL), lambda i, pos: (i, 5))
    else:
        gate_spec = pl.BlockSpec((None, 1, D_MODEL), lambda i, pos: (i // nt, 0, 5))
    return pl.pallas_call(
        functools.partial(_combine_kernel, tok0=tok0, tm=tm),
        grid_spec=pltpu.PrefetchScalarGridSpec(
            num_scalar_prefetch=1,
            grid=(gc.n_tiles,),
            in_specs=[pl.BlockSpec(memory_space=pl.ANY),
                      pl.BlockSpec((tm, D_MODEL), lambda i, pos: (i, 0)),
                      pl.BlockSpec((tm, LANE), lambda i, pos: (i, 0)),
                      gate_spec],
            out_specs=pl.BlockSpec((tm, D_MODEL), lambda i, pos: (i, 0)),
            scratch_shapes=[pltpu.VMEM((2, TOP_K * tm, D_MODEL), F32), pltpu.SemaphoreType.DMA((2,))]),
        out_shape=jax.ShapeDtypeStruct((g.T, D_MODEL), F32),
        compiler_params=_params("arbitrary"),
        name="moe_combine",
    )(pos, y_rows, x, route, mod)


def _dispatch_plan(route_all):
    rb = MOE_ROWS
    n_tok = route_all.shape[0]
    n_assign = n_tok * TOP_K
    n_blocks = -(-(n_assign + N_EXPERTS * (rb - 1)) // rb)
    flat_e = route_all[:, :TOP_K].astype(jnp.int32).reshape(-1)
    onehot = (flat_e[:, None] == jnp.arange(N_EXPERTS, dtype=jnp.int32)[None, :]).astype(jnp.int32)
    csum = jnp.cumsum(onehot, axis=0)
    counts = csum[-1]
    rank = jnp.sum(onehot * csum, axis=1) - 1
    padded = (counts + rb - 1) // rb * rb
    pad_end = jnp.cumsum(padded)
    pad_start = pad_end - padded
    start = jnp.cumsum(counts) - counts
    pos = pad_start[flat_e] + rank
    n_active = (pad_end[-1] // rb).astype(jnp.int32)
    blk = jnp.minimum(jnp.arange(n_blocks, dtype=jnp.int32), n_active - 1)
    block_e = jnp.minimum(jnp.searchsorted(pad_end, blk * rb, side='right'), N_EXPERTS - 1).astype(jnp.int32)
    order = jnp.argsort(flat_e).astype(jnp.int32)
    row = jnp.arange(n_blocks * rb, dtype=jnp.int32)
    row_e = jnp.repeat(block_e, rb)
    idx = row - pad_start[row_e]
    src = jnp.clip(start[row_e] + idx, 0, n_assign - 1)
    row_token = jnp.where(jnp.logical_and(idx < counts[row_e], row < pad_end[-1]), order[src] // TOP_K, 0)
    return block_e, row_token.astype(jnp.int32), n_active.reshape(1), pos.astype(jnp.int32)


def _layer(layer, groups, xs, mods, states, p, state_dn):
    hs, new_states = [], []
    for g, x, mod, st in zip(groups, xs, mods, states):
        s5_re, s5_im, dn_conv, lru_h, lru_conv = st
        single = g.L == 1
        h = _norm_mod(g, x, p['norm1_g'][layer], mod, 0, 1)
        uq = _proj(g, h, p['w_a'], layer, 0, BRANCH_W + 3 * DN_W, 512)
        ab = _proj(g, h, p['w_a'], layer, COL_AB, LANE, LANE)
        rest = _proj(g, h, p['w_rest'], layer, 0, REST_GATE, 512)

        ya, n_re, n_im = _s5(g, uq, s5_re, s5_im, p['s5_disc'], p['s5_cr'], p['s5_ci'], p['s5_d'],
                             p['s5_w_glu'], p['s5_b_glu'], layer)
        qkv = uq[:, BRANCH_W:]
        lru_x = rest[:, DN_W:DN_W + LRU_W]
        if single:
            yb, n_dn = _dn_step(g, uq, dn_conv, rest, ab, p['dn_conv_w'], p['dn_prm'], p['dn_norm_g'], state_dn, layer)
            n_dn_conv = jnp.concatenate([dn_conv[:, 1:], qkv[:, None, :]], axis=1)
            n_lru_conv = jnp.concatenate([lru_conv[:, 1:], lru_x[:, None, :]], axis=1)
            yc, n_lru = _lru(g, rest, lru_conv, lru_h, p, layer, pos0=p['past_len'])
        else:
            yb, n_dn = _dn_seq(g, uq, rest, ab, p['dn_conv_w'], p['dn_prm'], p['dn_norm_g'], layer)
            tail = lambda a: a.reshape(g.B, g.L, -1)[:, g.L - (CONV_W - 1):]
            n_dn_conv = tail(qkv)
            n_lru_conv = tail(lru_x)
            yc, n_lru = _lru(g, rest, None, None, p, layer, pos0=0)

        merged = _merge(g, h, ya, yb, yc, p['w_rest'], p['w_branch'], layer)
        x1 = _out_proj(g, merged, p['w_out'], layer, x, mod, 2)
        h2, route = _router(g, x1, p['norm2_g'][layer], mod, p['w_route'], p['b_route'], layer)
        hs.append((x1, h2, route))
        new_states.append((n_re.reshape(g.B, S5_GROUPS, S5_STATE), n_im.reshape(g.B, S5_GROUPS, S5_STATE),
                           n_dn, n_dn_conv, n_lru, n_lru_conv))

    h_all = jnp.concatenate([t[1] for t in hs], axis=0)
    route_all = jnp.concatenate([t[2] for t in hs], axis=0)
    block_e, row_token, n_active, pos = _dispatch_plan(route_all)
    y_rows = _experts(h_all, block_e, row_token, n_active, p['w_e_gate'], p['w_e_up'], p['w_e_down'], layer)
    outs, tok0 = [], 0
    for g, (x1, _, route), mod in zip(groups, hs, mods):
        outs.append(_combine(g, tok0, pos, y_rows, x1, route, mod))
        tok0 += g.T
    return outs, new_states


def kernel(x_prompt, x_sample, c_prompt, c_sample, state_s5_re, state_s5_im, state_dn, state_dn_conv, state_lru, state_lru_conv, final_g, norm1_g, norm2_g, w_ada, b_ada, w_in, s5_log_dt, s5_a_re, s5_a_im, s5_b_re, s5_b_im, s5_c_re, s5_c_im, s5_d, s5_w_glu, s5_b_glu, dn_conv_w, dn_a_log, dn_dt_bias, dn_norm_g, lru_conv_w, lru_conv_b, lru_w_a, lru_b_a, lru_w_x, lru_b_x, lru_lambda, w_branch, w_out, w_rg, b_rg, w_re, b_re, w_e_gate, w_e_up, w_e_down):
    depth = w_in.shape[0]
    bp, lp, _ = x_prompt.shape
    bs, ls, _ = x_sample.shape
    assert ls == 1 and bs % SUBLANE == 0 and lp % SEQ_CHUNK == 0
    gp = _Group(bp, lp, min(lp, 1024))
    gs = _Group(bs, 1, bs)
    groups = (gp, gs)

    pad = (-bp) % SUBLANE
    c_all = jnp.concatenate([c_prompt, jnp.zeros((pad, D_MODEL), F32), c_sample], axis=0)
    mod_all = _ada_mod(c_all, w_ada, b_ada)

    prm = jnp.zeros((depth, 2, LANE), F32)
    prm = prm.at[:, 0, :DN_HEADS].set(dn_a_log).at[:, 1, :DN_HEADS].set(dn_dt_bias)
    w_route = jnp.concatenate([w_rg, w_re, jnp.zeros((depth, D_MODEL, LANE - N_GROUPS - N_EXPERTS), F32)], axis=-1)
    b_route = jnp.concatenate([b_rg, b_re, jnp.zeros((depth, LANE - N_GROUPS - N_EXPERTS), F32)], axis=-1)
    p = dict(norm1_g=norm1_g, norm2_g=norm2_g,
             w_a=w_in[:, :, :COL_AB + LANE].astype(MXU_DTYPE), w_rest=w_in[:, :, COL_REST:].astype(MXU_DTYPE),
             s5_disc=_s5_discretize(s5_log_dt, s5_a_re, s5_a_im, s5_b_re, s5_b_im),
             s5_cr=_s5_out_blocks(s5_c_re), s5_ci=_s5_out_blocks(s5_c_im),
             s5_d=s5_d, s5_w_glu=s5_w_glu, s5_b_glu=s5_b_glu,
             dn_conv_w=dn_conv_w, dn_prm=prm, dn_norm_g=dn_norm_g,
             lru_conv_w=lru_conv_w, lru_conv_b=lru_conv_b, lru_wa_bd=_block_diag(lru_w_a), lru_b_a=lru_b_a,
             lru_wx_bd=_block_diag(lru_w_x), lru_b_x=lru_b_x, lru_lambda=lru_lambda,
             w_branch=w_branch, w_out=w_out, w_route=w_route, b_route=b_route.reshape(depth, 1, LANE),
             w_e_gate=w_e_gate, w_e_up=w_e_up, w_e_down=w_e_down, past_len=PAST_LEN)

    xs = [x_prompt.reshape(gp.T, D_MODEL), x_sample.reshape(gs.T, D_MODEL)]
    per_layer = []
    for l in range(depth):
        mods = [mod_all[l, :bp].reshape(bp, 1, 6 * D_MODEL), mod_all[l, bp + pad:]]
        states = [(None, None, None, None, None),
                  (state_s5_re[l].reshape(bs, S5_N), state_s5_im[l].reshape(bs, S5_N), state_dn_conv[l],
                   state_lru[l], state_lru_conv[l])]
        xs, new_states = _layer(l, groups, xs, mods, states, p, state_dn)
        per_layer.append(new_states)

    y_prompt = _final_norm(gp, xs[0], final_g).reshape(bp, lp, D_MODEL)
    y_sample = _final_norm(gs, xs[1], final_g).reshape(bs, 1, D_MODEL)
    stack = lambda gi: tuple(jnp.stack([per_layer[l][gi][k] for l in range(depth)]) for k in range(6))
    return (y_prompt, y_sample) + stack(0) + stack(1)
```

```python
import functools
import math
from typing import NamedTuple

import jax
import jax.numpy as jnp
from jax import lax
from jax.experimental import pallas as pl
from jax.experimental.pallas import tpu as pltpu

F32 = jnp.float32
MXU_DTYPE = jnp.bfloat16

D_MODEL = 2048
BRANCH_W = D_MODEL // 4
S5_GROUP = 16
S5_GROUPS = BRANCH_W // S5_GROUP
S5_STATE = 64
S5_N = S5_GROUPS * S5_STATE
S5_BLOCKS = 4
DN_HEADS = 4
DN_HEAD_DIM = 128
DN_W = DN_HEADS * DN_HEAD_DIM
DN_CHUNK = 64
CONV_W = 4
LRU_W = BRANCH_W
LRU_BLOCKS = 4
LRU_C = 8.0
N_GROUPS = 4
EXPERTS_PER_GROUP = 8
N_EXPERTS = N_GROUPS * EXPERTS_PER_GROUP
TOP_K = 2
D_EXPERT = D_MODEL // 4
NORM_EPS = 1e-6
PAST_LEN = 16384

COL_AB = BRANCH_W + 3 * DN_W
COL_REST = COL_AB + 2 * DN_HEADS
REST_GATE = DN_W + 2 * LRU_W

LANE = 128
SUBLANE = 8
VMEM_LIMIT = 56 * 1024 * 1024
SEQ_CHUNK = 256
MOE_ROWS = 256
COMBINE_ROWS = 128


class _Group(NamedTuple):
    B: int
    L: int
    tm: int

    @property
    def T(self):
        return self.B * self.L

    @property
    def nt(self):
        return max(self.L // self.tm, 1)

    @property
    def n_tiles(self):
        return self.T // self.tm


def _params(*sem):
    return pltpu.CompilerParams(dimension_semantics=sem, vmem_limit_bytes=VMEM_LIMIT)


def _mm(a, b):
    return jnp.dot(a.astype(MXU_DTYPE), b.astype(MXU_DTYPE), preferred_element_type=F32)


def _mm_nt(a, b):
    return lax.dot_general(a.astype(MXU_DTYPE), b.astype(MXU_DTYPE), (((1,), (1,)), ((), ())),
                           preferred_element_type=F32)


def _mm_tn(a, b):
    return lax.dot_general(a.astype(MXU_DTYPE), b.astype(MXU_DTYPE), (((0,), (0,)), ((), ())),
                           preferred_element_type=F32)


def _silu(x):
    return x * jax.nn.sigmoid(x)


def _gelu_tanh(x):
    return 0.5 * x * (1.0 + jnp.tanh(math.sqrt(2.0 / math.pi) * (x + 0.044715 * (x * x * x))))


def _softplus(x):
    return jnp.maximum(x, 0.0) + jnp.log1p(jnp.exp(-jnp.abs(x)))


def _rms(x, gain):
    return x * lax.rsqrt(jnp.mean(x * x, axis=-1, keepdims=True) + NORM_EPS) * gain


def _row_iota(shape):
    return lax.broadcasted_iota(jnp.int32, shape, len(shape) - 2)


def _last_row(x):
    return jnp.broadcast_to(x[SUBLANE - 1:SUBLANE], x.shape)


def _nat_spec(g, tn, col0=0):
    return pl.BlockSpec((g.tm, tn), lambda i, j: (i, col0 + j))


def _mod_spec(g, k, tn):
    nb, nt = D_MODEL // tn, g.nt
    if g.L == 1:
        return pl.BlockSpec((g.tm, tn), lambda i, j: (i, k * nb + j))
    return pl.BlockSpec((None, 1, tn), lambda i, j: (i // nt, 0, k * nb + j))


def _ada_kernel(c_ref, w_ref, b_ref, o_ref):
    o_ref[...] = _mm(_silu(c_ref[...]), w_ref[...]) + b_ref[...]


def _ada_mod(c_all, w_ada, b_ada):
    depth, _, n6 = w_ada.shape
    rows = c_all.shape[0]
    tn = 1024
    return pl.pallas_call(
        _ada_kernel,
        grid=(depth, n6 // tn),
        in_specs=[pl.BlockSpec((rows, D_MODEL), lambda l, j: (0, 0)),
                  pl.BlockSpec((None, D_MODEL, tn), lambda l, j: (l, 0, j)),
                  pl.BlockSpec((None, 1, tn), lambda l, j: (l, 0, j))],
        out_specs=pl.BlockSpec((None, rows, tn), lambda l, j: (l, 0, j)),
        out_shape=jax.ShapeDtypeStruct((depth, rows, n6), F32),
        compiler_params=_params("arbitrary", "arbitrary"),
        name="ada_mod",
    )(c_all, w_ada, b_ada.reshape(depth, 1, n6))


def _norm_mod_kernel(x_ref, g_ref, sc_ref, sh_ref, o_ref):
    y = _rms(x_ref[...], g_ref[...])
    o_ref[...] = (y * (1.0 + sc_ref[...]) + sh_ref[...]).astype(o_ref.dtype)


def _norm_mod(g, x, gain, mod, k_shift, k_scale):
    gm = g._replace(tm=min(g.tm, 512))
    return pl.pallas_call(
        _norm_mod_kernel,
        grid=(gm.n_tiles, 1),
        in_specs=[_nat_spec(gm, D_MODEL),
                  pl.BlockSpec((1, D_MODEL), lambda i, j: (0, 0)),
                  _mod_spec(gm, k_scale, D_MODEL),
                  _mod_spec(gm, k_shift, D_MODEL)],
        out_specs=_nat_spec(gm, D_MODEL),
        out_shape=jax.ShapeDtypeStruct((g.T, D_MODEL), MXU_DTYPE),
        compiler_params=_params("arbitrary", "arbitrary"),
        name="norm_mod",
    )(x, gain.reshape(1, D_MODEL), mod, mod)


def _final_norm_kernel(x_ref, g_ref, o_ref):
    o_ref[...] = _rms(x_ref[...], g_ref[...])


def _final_norm(g, x, gain):
    gm = g._replace(tm=min(g.tm, 512))
    return pl.pallas_call(
        _final_norm_kernel,
        grid=(gm.n_tiles, 1),
        in_specs=[_nat_spec(gm, D_MODEL), pl.BlockSpec((1, D_MODEL), lambda i, j: (0, 0))],
        out_specs=_nat_spec(gm, D_MODEL),
        out_shape=jax.ShapeDtypeStruct((g.T, D_MODEL), F32),
        compiler_params=_params("arbitrary", "arbitrary"),
        name="final_norm",
    )(x, gain.reshape(1, D_MODEL))


def _proj_kernel(a_ref, w_ref, o_ref):
    o_ref[...] = _mm(a_ref[...], w_ref[...]).astype(o_ref.dtype)


def _proj(g, a, w, layer, col0, n, tn):
    cb = col0 // tn
    return pl.pallas_call(
        _proj_kernel,
        grid=(g.n_tiles, n // tn),
        in_specs=[pl.BlockSpec((g.tm, D_MODEL), lambda i, j: (i, 0)),
                  pl.BlockSpec((None, D_MODEL, tn), lambda i, j: (layer, 0, cb + j))],
        out_specs=_nat_spec(g, tn),
        out_shape=jax.ShapeDtypeStruct((g.T, n), F32),
        compiler_params=_params("arbitrary", "arbitrary"),
        name="in_proj",
    )(a, w)


def _out_proj_kernel(a_ref, w_ref, x_ref, g_ref, o_ref):
    o_ref[...] = x_ref[...] + g_ref[...] * _mm(a_ref[...], w_ref[...])


def _out_proj(g, merged, w_out, layer, x, mod, k_gate):
    tn = 512
    return pl.pallas_call(
        _out_proj_kernel,
        grid=(g.n_tiles, D_MODEL // tn),
        in_specs=[pl.BlockSpec((g.tm, D_MODEL), lambda i, j: (i, 0)),
                  pl.BlockSpec((None, D_MODEL, tn), lambda i, j: (layer, 0, j)),
                  _nat_spec(g, tn),
                  _mod_spec(g, k_gate, tn)],
        out_specs=_nat_spec(g, tn),
        out_shape=jax.ShapeDtypeStruct((g.T, D_MODEL), F32),
        compiler_params=_params("arbitrary", "arbitrary"),
        name="out_proj",
    )(merged, w_out, x, mod)


def _merge_kernel(h_ref, ya_ref, yb_ref, yc_ref, wg0, wg1, wg2, wb0, wb1, wb2, o_ref):
    h = h_ref[...]
    acc = None
    for y_ref, wg, wb in ((ya_ref, wg0, wb0), (yb_ref, wg1, wb1), (yc_ref, wg2, wb2)):
        t = jax.nn.sigmoid(_mm(h, wg[...])) * _mm(y_ref[...], wb[...])
        acc = t if acc is None else acc + t
    o_ref[...] = acc.astype(o_ref.dtype)


def _merge(g, h, ya, yb, yc, w_rest, w_branch, layer):
    tn = 256
    gate_specs = [pl.BlockSpec((None, D_MODEL, tn),
                               functools.partial(lambda i, j, c: (layer, 0, c + j), c=(REST_GATE + n * D_MODEL) // tn))
                  for n in range(3)]
    br_specs = [pl.BlockSpec((None, None, BRANCH_W, tn), functools.partial(lambda i, j, n: (layer, n, 0, j), n=n))
                for n in range(3)]
    y_spec = pl.BlockSpec((g.tm, BRANCH_W), lambda i, j: (i, 0))
    return pl.pallas_call(
        _merge_kernel,
        grid=(g.n_tiles, D_MODEL // tn),
        in_specs=[pl.BlockSpec((g.tm, D_MODEL), lambda i, j: (i, 0)), y_spec, y_spec, y_spec] + gate_specs + br_specs,
        out_specs=_nat_spec(g, tn),
        out_shape=jax.ShapeDtypeStruct((g.T, D_MODEL), MXU_DTYPE),
        compiler_params=_params("arbitrary", "arbitrary"),
        name="merge",
    )(h, ya, yb, yc, w_rest, w_rest, w_rest, w_branch, w_branch, w_branch)


def _s5_disc_kernel(ldt_ref, ar_ref, ai_ref, br_ref, bi_ref, lr_ref, li_ref, bbr_ref, bbi_ref):
    dt = jnp.exp(ldt_ref[...])
    ar, ai = ar_ref[...], ai_ref[...]
    mag = jnp.exp(ar * dt)
    lr = mag * jnp.cos(ai * dt)
    li = mag * jnp.sin(ai * dt)
    den = ar * ar + ai * ai
    fr = ((lr - 1.0) * ar + li * ai) / den
    fi = (li * ar - (lr - 1.0) * ai) / den
    lr_ref[...] = lr
    li_ref[...] = li
    br, bi = br_ref[...], bi_ref[...]
    bbr_ref[...] = fr * br - fi * bi
    bbi_ref[...] = fr * bi + fi * br


def _s5_discretize(log_dt, a_re, a_im, b_re, b_im):
    depth = log_dt.shape[0]
    n = depth * S5_GROUPS
    rows = n * S5_GROUP
    rep = lambda v, w: jnp.repeat(v.reshape(n, w), S5_GROUP, axis=0)
    brt = jnp.swapaxes(b_re, -1, -2).reshape(rows, S5_STATE)
    bit = jnp.swapaxes(b_im, -1, -2).reshape(rows, S5_STATE)
    sds = jax.ShapeDtypeStruct
    lr, li, bbr, bbi = pl.pallas_call(
        _s5_disc_kernel,
        out_shape=tuple(sds((rows, S5_STATE), F32) for _ in range(4)),
        name="s5_discretize",
    )(rep(log_dt, 1), rep(a_re, S5_STATE), rep(a_im, S5_STATE), brt, bit)
    gpb = S5_GROUPS // S5_BLOCKS
    eye = jnp.eye(gpb, dtype=F32)

    def blocks_in(m):
        m = m.reshape(depth, S5_BLOCKS, gpb, S5_GROUP, S5_STATE)
        m = jnp.einsum('ljgnp,gh->ljgnhp', m, eye)
        return m.reshape(depth, S5_BLOCKS, gpb * S5_GROUP, gpb * S5_STATE).astype(MXU_DTYPE)

    lam = lambda v: v.reshape(n, S5_GROUP, S5_STATE)[:, 0].reshape(depth, 1, S5_N)
    return lam(lr), lam(li), blocks_in(bbr), blocks_in(bbi)


def _s5_out_blocks(c):
    depth = c.shape[0]
    gpb = S5_GROUPS // S5_BLOCKS
    eye = jnp.eye(gpb, dtype=F32)
    m = jnp.einsum('ljgnp,gh->ljgphn', c.reshape(depth, S5_BLOCKS, gpb, S5_GROUP, S5_STATE), eye)
    return m.reshape(depth, S5_BLOCKS, gpb * S5_STATE, gpb * S5_GROUP).astype(MXU_DTYPE)


def _s5_in(u, bb_ref):
    cw, sw = BRANCH_W // S5_BLOCKS, S5_N // S5_BLOCKS
    return [_mm(u[:, j * cw:(j + 1) * cw], bb_ref[j]) for j in range(S5_BLOCKS)], sw


def _s5_out(hr, hi, u, cr_ref, ci_ref, d_ref, wglu_ref, bglu_ref):
    sw = S5_N // S5_BLOCKS
    y = jnp.concatenate([_mm(hr[:, j * sw:(j + 1) * sw], cr_ref[j]) - _mm(hi[:, j * sw:(j + 1) * sw], ci_ref[j])
                         for j in range(S5_BLOCKS)], axis=1)
    y = _gelu_tanh(y + d_ref[...] * u)
    return y * jax.nn.sigmoid(_mm(y, wglu_ref[...]) + bglu_ref[...])


def _scan_tiles(n_tiles, cw, width, tile_fn, load_state, store_state):
    for c in range(width // cw):
        cs = pl.ds(c * cw, cw)

        def body(k, carry, cs=cs):
            return tile_fn(pl.ds(pl.multiple_of(k * SUBLANE, SUBLANE), SUBLANE), cs, carry)

        store_state(cs, lax.fori_loop(0, n_tiles, body, load_state(cs)))


def _s5_seq_kernel(u_ref, bbr_ref, bbi_ref, lr_ref, li_ref, cr_ref, ci_ref, d_ref, wglu_ref, bglu_ref,
                   y_ref, hlr_ref, hli_ref, xr_s, xi_s, cr_s, ci_s, pw_s):
    c = pl.program_id(1)
    lc = u_ref.shape[0]

    @pl.when(c == 0)
    def _():
        cr_s[...] = jnp.zeros_like(cr_s)
        ci_s[...] = jnp.zeros_like(ci_s)
        lr, li = lr_ref[...], li_ref[...]
        row = _row_iota((SUBLANE, S5_N))
        pr, pi = lr, li
        acc_r = jnp.zeros((SUBLANE, S5_N), F32)
        acc_i = jnp.zeros((SUBLANE, S5_N), F32)
        for r in range(SUBLANE):
            acc_r = jnp.where(row == r, pr, acc_r)
            acc_i = jnp.where(row == r, pi, acc_i)
            if r + 1 in (1, 2, 4):
                k = (1, 2, 4).index(r + 1)
                pw_s[2 * k] = jnp.where(row >= r + 1, pr, 0.0)
                pw_s[2 * k + 1] = jnp.where(row >= r + 1, pi, 0.0)
            pr, pi = pr * lr - pi * li, pr * li + pi * lr
        pw_s[6] = acc_r
        pw_s[7] = acc_i

    u = u_ref[...]
    xr, sw = _s5_in(u, bbr_ref)
    xi, _ = _s5_in(u, bbi_ref)
    for j in range(S5_BLOCKS):
        xr_s[:, pl.ds(j * sw, sw)] = xr[j]
        xi_s[:, pl.ds(j * sw, sw)] = xi[j]

    def tile(rows, cs, carry):
        pr0, pi0 = carry
        hr, hi = xr_s[rows, cs], xi_s[rows, cs]
        for k, s in enumerate((1, 2, 4)):
            ar, ai = pw_s[2 * k, :, cs], pw_s[2 * k + 1, :, cs]
            sr, si = pltpu.roll(hr, s, 0), pltpu.roll(hi, s, 0)
            hr, hi = hr + ar * sr - ai * si, hi + ar * si + ai * sr
        ar, ai = pw_s[6, :, cs], pw_s[7, :, cs]
        hr, hi = hr + ar * pr0 - ai * pi0, hi + ar * pi0 + ai * pr0
        xr_s[rows, cs] = hr
        xi_s[rows, cs] = hi
        return _last_row(hr), _last_row(hi)

    def load(cs):
        return cr_s[:, cs], ci_s[:, cs]

    def store(cs, carry):
        cr_s[:, cs], ci_s[:, cs] = carry

    _scan_tiles(lc // SUBLANE, 512, S5_N, tile, load, store)

    y_ref[...] = _s5_out(xr_s[...], xi_s[...], u, cr_ref, ci_ref, d_ref, wglu_ref, bglu_ref).astype(y_ref.dtype)

    @pl.when(c == pl.num_programs(1) - 1)
    def _():
        hlr_ref[...] = cr_s[pl.ds(0, 1), :]
        hli_ref[...] = ci_s[pl.ds(0, 1), :]


def _s5_step_kernel(u_ref, h0r_ref, h0i_ref, bbr_ref, bbi_ref, lr_ref, li_ref, cr_ref, ci_ref, d_ref, wglu_ref,
                    bglu_ref, y_ref, hlr_ref, hli_ref):
    u = u_ref[...]
    xr, _ = _s5_in(u, bbr_ref)
    xi, _ = _s5_in(u, bbi_ref)
    lr, li, sr, si = lr_ref[...], li_ref[...], h0r_ref[...], h0i_ref[...]
    hr = lr * sr - li * si + jnp.concatenate(xr, axis=1)
    hi = lr * si + li * sr + jnp.concatenate(xi, axis=1)
    hlr_ref[...] = hr
    hli_ref[...] = hi
    y_ref[...] = _s5_out(hr, hi, u, cr_ref, ci_ref, d_ref, wglu_ref, bglu_ref).astype(y_ref.dtype)


def _s5(g, uq, h0r, h0i, disc, cr, ci, d, w_glu, b_glu, layer):
    lam_r, lam_i, bbr, bbi = disc
    cw, sw = BRANCH_W // S5_BLOCKS, S5_N // S5_BLOCKS
    sds = jax.ShapeDtypeStruct
    seq = g.L > 1
    lc = min(g.L, SEQ_CHUNK) if seq else g.B
    nc = g.L // lc if seq else 1
    grid = (g.B, nc) if seq else (1, 1)
    row = (lambda b, c: (b * nc + c, 0))
    lay3 = lambda b, c: (layer, 0, 0)
    lay4 = lambda b, c: (layer, 0, 0, 0)
    w_specs = [pl.BlockSpec((None, S5_BLOCKS, cw, sw), lay4), pl.BlockSpec((None, S5_BLOCKS, cw, sw), lay4),
               pl.BlockSpec((None, 1, S5_N), lay3), pl.BlockSpec((None, 1, S5_N), lay3),
               pl.BlockSpec((None, S5_BLOCKS, sw, cw), lay4), pl.BlockSpec((None, S5_BLOCKS, sw, cw), lay4),
               pl.BlockSpec((None, 1, BRANCH_W), lay3),
               pl.BlockSpec((None, BRANCH_W, BRANCH_W), lay3),
               pl.BlockSpec((None, 1, BRANCH_W), lay3)]
    w_args = (bbr, bbi, lam_r, lam_i, cr, ci, d.reshape(-1, 1, BRANCH_W), w_glu, b_glu.reshape(-1, 1, BRANCH_W))
    if seq:
        st_spec = pl.BlockSpec((None, 1, S5_N), lambda b, c: (b, 0, 0))
        y, hlr, hli = pl.pallas_call(
            _s5_seq_kernel,
            grid=grid,
            in_specs=[pl.BlockSpec((lc, BRANCH_W), row)] + w_specs,
            out_specs=(pl.BlockSpec((lc, BRANCH_W), row), st_spec, st_spec),
            out_shape=(sds((g.T, BRANCH_W), MXU_DTYPE), sds((g.B, 1, S5_N), F32), sds((g.B, 1, S5_N), F32)),
            scratch_shapes=[pltpu.VMEM((lc, S5_N), F32), pltpu.VMEM((lc, S5_N), F32),
                            pltpu.VMEM((SUBLANE, S5_N), F32), pltpu.VMEM((SUBLANE, S5_N), F32),
                            pltpu.VMEM((8, SUBLANE, S5_N), F32)],
            compiler_params=_params("arbitrary", "arbitrary"),
            name="s5_ssm",
        )(uq, *w_args)
        return y, hlr.reshape(g.B, S5_N), hli.reshape(g.B, S5_N)
    st_spec = pl.BlockSpec((g.B, S5_N), lambda b, c: (0, 0))
    return pl.pallas_call(
        _s5_step_kernel,
        grid=grid,
        in_specs=[pl.BlockSpec((lc, BRANCH_W), row), st_spec, st_spec] + w_specs,
        out_specs=(pl.BlockSpec((lc, BRANCH_W), row), st_spec, st_spec),
        out_shape=(sds((g.T, BRANCH_W), MXU_DTYPE), sds((g.B, S5_N), F32), sds((g.B, S5_N), F32)),
        compiler_params=_params("arbitrary", "arbitrary"),
        name="s5_ssm",
    )(uq, h0r, h0i, *w_args)


def _lru_gates(xc, wa_ref, ba_ref, wx_ref, bx_ref, lam_ref):
    r = jax.nn.sigmoid(_mm(xc, wa_ref[...]) + ba_ref[...])
    ig = jax.nn.sigmoid(_mm(xc, wx_ref[...]) + bx_ref[...])
    log_a = -LRU_C * r * _softplus(-lam_ref[...])
    return jnp.exp(log_a), jnp.sqrt(1.0 - jnp.exp(2.0 * log_a)), ig


def _lru_seq_kernel(x_ref, g_ref, cw_ref, cbias_ref, wa_ref, ba_ref, wx_ref, bx_ref, lam_ref,
                    y_ref, hl_ref, xbuf, a_s, b_s, h_s):
    c = pl.program_id(1)
    lc = x_ref.shape[0]
    pad = SUBLANE

    @pl.when(c == 0)
    def _():
        xbuf[pl.ds(0, pad), :] = jnp.zeros((pad, LRU_W), F32)
        h_s[...] = jnp.zeros_like(h_s)

    xbuf[pl.ds(pad, lc), :] = x_ref[...]
    xc = cbias_ref[...]
    for j in range(CONV_W):
        xc = xc + cw_ref[pl.ds(j, 1), :] * xbuf[pl.ds(pad - (CONV_W - 1) + j, lc), :]
    new_tail = xbuf[pl.ds(lc, pad), :]
    xbuf[pl.ds(0, pad), :] = new_tail

    a, mult, ig = _lru_gates(xc, wa_ref, ba_ref, wx_ref, bx_ref, lam_ref)
    first = jnp.logical_and(c == 0, _row_iota((lc, LRU_W)) == 0)
    a_s[...] = jnp.where(first, 0.0, a)
    b_s[...] = jnp.where(first, 1.0, mult) * (ig * xc)

    def tile(rows, cs, h0):
        a, b = a_s[rows, cs], b_s[rows, cs]
        row = _row_iota(a.shape)
        for s in (1, 2, 4):
            keep = row >= s
            b = a * jnp.where(keep, pltpu.roll(b, s, 0), 0.0) + b
            a = a * jnp.where(keep, pltpu.roll(a, s, 0), 1.0)
        h = a * h0 + b
        b_s[rows, cs] = h
        return _last_row(h)

    def store(cs, h):
        h_s[:, cs] = h

    _scan_tiles(lc // SUBLANE, LRU_W, LRU_W, tile, lambda cs: h_s[:, cs], store)
    y_ref[...] = (b_s[...] * _gelu_tanh(g_ref[...])).astype(y_ref.dtype)

    @pl.when(c == pl.num_programs(1) - 1)
    def _():
        hl_ref[...] = h_s[pl.ds(0, 1), :]


def _lru_step_kernel(x_ref, g_ref, cs_ref, h0_ref, cw_ref, cbias_ref, wa_ref, ba_ref, wx_ref, bx_ref, lam_ref,
                     y_ref, hl_ref):
    x = x_ref[...]
    xc = cbias_ref[...] + cw_ref[pl.ds(CONV_W - 1, 1), :] * x
    for j in range(CONV_W - 1):
        xc = xc + cw_ref[pl.ds(j, 1), :] * cs_ref[j]
    a, mult, ig = _lru_gates(xc, wa_ref, ba_ref, wx_ref, bx_ref, lam_ref)
    h = a * h0_ref[...] + mult * (ig * xc)
    hl_ref[...] = h
    y_ref[...] = (h * _gelu_tanh(g_ref[...])).astype(y_ref.dtype)


def _block_diag(w):
    depth, nb, k, _ = w.shape
    eye = jnp.eye(nb, dtype=w.dtype)
    return jnp.einsum('lnwv,nm->lnwmv', w, eye).reshape(depth, nb * k, nb * k).astype(MXU_DTYPE)


def _lru(g, rest, conv_state, h0, p, layer, pos0):
    sds = jax.ShapeDtypeStruct
    seq = g.L > 1
    lc = min(g.L, SEQ_CHUNK) if seq else g.B
    nc = g.L // lc if seq else 1
    lay3 = lambda b, c: (layer, 0, 0)
    vec = lambda v: v.reshape(-1, 1, LRU_W)
    w_specs = [pl.BlockSpec((None, CONV_W, LRU_W), lay3), pl.BlockSpec((None, 1, LRU_W), lay3),
               pl.BlockSpec((None, LRU_W, LRU_W), lay3), pl.BlockSpec((None, 1, LRU_W), lay3),
               pl.BlockSpec((None, LRU_W, LRU_W), lay3), pl.BlockSpec((None, 1, LRU_W), lay3),
               pl.BlockSpec((None, 1, LRU_W), lay3)]
    w_args = (p['lru_conv_w'], vec(p['lru_conv_b']), p['lru_wa_bd'], vec(p['lru_b_a']), p['lru_wx_bd'],
              vec(p['lru_b_x']), vec(p['lru_lambda']))
    x_spec = pl.BlockSpec((lc, LRU_W), lambda b, c: (b * nc + c, 1))
    g_spec = pl.BlockSpec((lc, LRU_W), lambda b, c: (b * nc + c, 2))
    y_spec = pl.BlockSpec((lc, LRU_W), lambda b, c: (b * nc + c, 0))
    if seq:
        assert pos0 == 0
        y, hl = pl.pallas_call(
            _lru_seq_kernel,
            grid=(g.B, nc),
            in_specs=[x_spec, g_spec] + w_specs,
            out_specs=(y_spec, pl.BlockSpec((None, 1, LRU_W), lambda b, c: (b, 0, 0))),
            out_shape=(sds((g.T, LRU_W), MXU_DTYPE), sds((g.B, 1, LRU_W), F32)),
            scratch_shapes=[pltpu.VMEM((lc + SUBLANE, LRU_W), F32), pltpu.VMEM((lc, LRU_W), F32),
                            pltpu.VMEM((lc, LRU_W), F32), pltpu.VMEM((SUBLANE, LRU_W), F32)],
            compiler_params=_params("arbitrary", "arbitrary"),
            name="rg_lru",
        )(rest, rest, *w_args)
        return y, hl.reshape(g.B, LRU_W)
    assert pos0 > 0
    st_spec = pl.BlockSpec((g.B, LRU_W), lambda b, c: (0, 0))
    return pl.pallas_call(
        _lru_step_kernel,
        grid=(1, 1),
        in_specs=[x_spec, g_spec, pl.BlockSpec((CONV_W - 1, g.B, LRU_W), lambda b, c: (0, 0, 0)), st_spec] + w_specs,
        out_specs=(y_spec, st_spec),
        out_shape=(sds((g.T, LRU_W), MXU_DTYPE), sds((g.B, LRU_W), F32)),
        compiler_params=_params("arbitrary", "arbitrary"),
        name="rg_lru",
    )(rest, rest, jnp.swapaxes(conv_state, 0, 1), h0, *w_args)


def _l2norm(x):
    return x * lax.rsqrt(jnp.sum(x * x, axis=-1, keepdims=True) + NORM_EPS)


def _dn_gates(ab, prm):
    g = -jnp.exp(prm[0:1, :]) * _softplus(ab + prm[1:2, :])
    return g, jax.nn.sigmoid(ab)


def _dn_seq_kernel(q_ref, k_ref, v_ref, z_ref, ab_ref, cw_ref, prm_ref, ng_ref, y_ref, sfin_ref, xbuf, s_s):
    c = pl.program_id(0)
    bsz, cs, _ = q_ref.shape
    hd = DN_HEAD_DIM
    rows = bsz * cs
    pad = SUBLANE

    @pl.when(c == 0)
    def _():
        xbuf[:, pl.ds(0, pad), :] = jnp.zeros((bsz, pad, 3 * DN_W), F32)
        s_s[...] = jnp.zeros_like(s_s)

    for s, ref in enumerate((q_ref, k_ref, v_ref)):
        xbuf[:, pl.ds(pad, cs), pl.ds(s * DN_W, DN_W)] = ref[...]
    conv = None
    for j in range(CONV_W):
        t = cw_ref[pl.ds(j, 1), :] * xbuf[:, pl.ds(pad - (CONV_W - 1) + j, cs), :]
        conv = t if conv is None else conv + t
    new_tail = xbuf[:, pl.ds(cs, pad), :]
    xbuf[:, pl.ds(0, pad), :] = new_tail
    act = _silu(conv).reshape(rows, 3 * DN_W)

    glog, beta_all = _dn_gates(ab_ref[...].reshape(rows, LANE), prm_ref[...])
    row = lax.broadcasted_iota(jnp.int32, (rows, rows), 0)
    col = lax.broadcasted_iota(jnp.int32, (rows, rows), 1)
    shift = cs.bit_length() - 1
    same = lax.shift_right_logical(row, shift) == lax.shift_right_logical(col, shift)
    causal = jnp.logical_and(same, row >= col)
    strict = jnp.logical_and(same, row > col)
    gc = jnp.dot(causal.astype(F32), glog, precision=lax.Precision.HIGHEST, preferred_element_type=F32)
    gct = gc.T
    eye = (row == col).astype(F32)
    ng = ng_ref[...]
    rblk = lax.shift_right_logical(lax.broadcasted_iota(jnp.int32, (rows, bsz * hd), 0), shift)
    cblk = lax.shift_right_logical(lax.broadcasted_iota(jnp.int32, (rows, bsz * hd), 1), hd.bit_length() - 1)
    own = rblk == cblk

    def diag_blocks(m):
        return jnp.concatenate([m[b * cs:(b + 1) * cs, b * hd:(b + 1) * hd] for b in range(bsz)], axis=0)

    for h in range(DN_HEADS):
        q = _l2norm(act[:, h * hd:(h + 1) * hd]) * (hd ** -0.5)
        k = _l2norm(act[:, DN_W + h * hd:DN_W + (h + 1) * hd])
        v = act[:, 2 * DN_W + h * hd:2 * DN_W + (h + 1) * hd]
        beta = beta_all[:, DN_HEADS + h:DN_HEADS + h + 1]
        gcol = gc[:, h:h + 1]
        decay = jnp.exp(jnp.where(causal, gcol - gct[h:h + 1, :], -jnp.inf))
        kb = k * beta
        nmat = jnp.where(strict, -(_mm_nt(kb, k) * decay), 0.0)
        rinv = eye + nmat
        xpow = nmat
        for _ in range(cs.bit_length() - 2):
            xpow = _mm(xpow, xpow)
            rinv = rinv + _mm(rinv, xpow)
        eg = jnp.exp(gcol)
        uw = _mm(rinv, jnp.concatenate([v * beta, kb * eg], axis=1))
        u, w = uw[:, :hd], uw[:, hd:]
        att = jnp.where(causal, _mm_nt(q, k) * decay, 0.0)
        s = s_s[h]
        ws_qs = _mm(jnp.concatenate([w, q * eg], axis=0), s)
        v_new = u - diag_blocks(ws_qs[:rows])
        o = diag_blocks(ws_qs[rows:]) + _mm(att, v_new)
        glast = [gc[(b + 1) * cs - 1:(b + 1) * cs, h:h + 1] for b in range(bsz)]
        glast_rows = jnp.concatenate([jnp.broadcast_to(t, (cs, 1)) for t in glast], axis=0)
        glast_lanes = jnp.concatenate([jnp.broadcast_to(t, (1, hd)) for t in glast], axis=1)
        v_bd = jnp.where(own, jnp.concatenate([v_new] * bsz, axis=1), 0.0)
        s_s[h] = s * jnp.exp(glast_lanes) + _mm_tn(k * jnp.exp(glast_rows - gcol), v_bd)
        o = _rms(o, ng) * _silu(z_ref[:, :, pl.ds(h * hd, hd)].reshape(rows, hd))
        y_ref[:, :, pl.ds(h * hd, hd)] = o.reshape(bsz, cs, hd).astype(y_ref.dtype)

    @pl.when(c == pl.num_programs(0) - 1)
    def _():
        for b in range(bsz):
            for h in range(DN_HEADS):
                sfin_ref[b, h] = s_s[h, :, pl.ds(b * hd, hd)]


def _dn_seq(g, uq, rest, ab, conv_w, prm, norm_g, layer):
    cs = DN_CHUNK
    assert cs & (cs - 1) == 0 and g.L % cs == 0
    nc = g.L // cs
    hd = DN_HEAD_DIM
    lay3 = lambda c: (layer, 0, 0)
    sds = jax.ShapeDtypeStruct
    sec = lambda s: pl.BlockSpec((g.B, cs, DN_W), lambda c: (0, c, s))
    y, sfin = pl.pallas_call(
        _dn_seq_kernel,
        grid=(nc,),
        in_specs=[sec(1), sec(2), sec(3), sec(0),
                  pl.BlockSpec((g.B, cs, LANE), lambda c: (0, c, 0)),
                  pl.BlockSpec((None, CONV_W, 3 * DN_W), lay3),
                  pl.BlockSpec((None, 2, LANE), lay3),
                  pl.BlockSpec((None, 1, hd), lay3)],
        out_specs=(pl.BlockSpec((g.B, cs, DN_W), lambda c: (0, c, 0)),
                   pl.BlockSpec((g.B, DN_HEADS, hd, hd), lambda c: (0, 0, 0, 0))),
        out_shape=(sds((g.B, g.L, DN_W), MXU_DTYPE), sds((g.B, DN_HEADS, hd, hd), F32)),
        scratch_shapes=[pltpu.VMEM((g.B, cs + SUBLANE, 3 * DN_W), F32),
                        pltpu.VMEM((DN_HEADS, hd, g.B * hd), F32)],
        compiler_params=_params("arbitrary"),
        name="deltanet_seq",
    )(*(3 * [uq.reshape(g.B, g.L, -1)]), rest.reshape(g.B, g.L, -1), ab.reshape(g.B, g.L, LANE), conv_w, prm,
      norm_g.reshape(-1, 1, hd))
    return y.reshape(g.T, DN_W), sfin


def _dn_step_kernel(q_ref, k_ref, v_ref, cq_ref, ck_ref, cv_ref, z_ref, ab_ref, cwq_ref, cwk_ref, cwv_ref,
                    prm_ref, ng_ref, s_ref, y_ref, so_ref, o_s, *, bsz):
    h = pl.program_id(0)
    hd = DN_HEAD_DIM

    def conv(x_ref, c_ref, w_ref):
        acc = w_ref[pl.ds(CONV_W - 1, 1), :] * x_ref[...]
        for j in range(CONV_W - 1):
            acc = acc + w_ref[pl.ds(j, 1), :] * c_ref[j]
        return _silu(acc)

    q = _l2norm(conv(q_ref, cq_ref, cwq_ref)) * (hd ** -0.5)
    k = _l2norm(conv(k_ref, ck_ref, cwk_ref))
    v = conv(v_ref, cv_ref, cwv_ref)
    glog, beta_all = _dn_gates(ab_ref[...], prm_ref[...])
    lane = lax.broadcasted_iota(jnp.int32, (bsz, LANE), 1)
    eg = jnp.exp(jnp.sum(jnp.where(lane == h, glog, 0.0), axis=-1, keepdims=True))
    beta = jnp.sum(jnp.where(lane == DN_HEADS + h, beta_all, 0.0), axis=-1, keepdims=True)
    att = jnp.sum(q * k, axis=-1, keepdims=True)
    kt = k.T
    qt = q.T
    for b in range(bsz):
        s = s_ref[b]
        kcol = kt[:, b:b + 1]
        ks = jnp.sum(s * kcol, axis=0, keepdims=True)
        qs = jnp.sum(s * qt[:, b:b + 1], axis=0, keepdims=True)
        eg_b, beta_b = eg[b:b + 1, :], beta[b:b + 1, :]
        v_new = beta_b * v[b:b + 1, :] - (beta_b * eg_b) * ks
        o_s[pl.ds(b, 1), :] = eg_b * qs + att[b:b + 1, :] * v_new
        so_ref[b] = s * eg_b + kcol * v_new
    y_ref[...] = (_rms(o_s[...], ng_ref[...]) * _silu(z_ref[...])).astype(y_ref.dtype)


def _dn_step(g, uq, conv_state, rest, ab, conv_w, prm, norm_g, s0_all, layer):
    bsz = g.B
    hd = DN_HEAD_DIM
    conv_t = jnp.swapaxes(conv_state, 0, 1)
    sec = lambda s: (lambda h: (0, (s + 1) * DN_HEADS + h))
    sec3 = lambda s: (lambda h: (0, 0, s * DN_HEADS + h))
    secw = lambda s: (lambda h: (layer, 0, s * DN_HEADS + h))
    lay3 = lambda h: (layer, 0, 0)
    sds = jax.ShapeDtypeStruct
    return pl.pallas_call(
        functools.partial(_dn_step_kernel, bsz=bsz),
        grid=(DN_HEADS,),
        in_specs=[pl.BlockSpec((bsz, hd), sec(s)) for s in range(3)]
                 + [pl.BlockSpec((CONV_W - 1, bsz, hd), sec3(s)) for s in range(3)]
                 + [pl.BlockSpec((bsz, hd), lambda h: (0, h)),
                    pl.BlockSpec((bsz, LANE), lambda h: (0, 0))]
                 + [pl.BlockSpec((None, CONV_W, hd), secw(s)) for s in range(3)]
                 + [pl.BlockSpec((None, 2, LANE), lay3),
                    pl.BlockSpec((None, 1, hd), lay3),
                    pl.BlockSpec((None, bsz, None, hd, hd), lambda h: (layer, 0, h, 0, 0))],
        out_specs=(pl.BlockSpec((bsz, hd), lambda h: (0, h)),
                   pl.BlockSpec((bsz, None, hd, hd), lambda h: (0, h, 0, 0))),
        out_shape=(sds((bsz, DN_W), MXU_DTYPE), sds((bsz, DN_HEADS, hd, hd), F32)),
        scratch_shapes=[pltpu.VMEM((bsz, hd), F32)],
        compiler_params=_params("arbitrary"),
        name="deltanet_step",
    )(uq, uq, uq, conv_t, conv_t, conv_t, rest, ab, conv_w, conv_w, conv_w, prm,
      norm_g.reshape(-1, 1, hd), s0_all)


def _router_kernel(x_ref, g_ref, sc_ref, sh_ref, wr_ref, br_ref, h_ref, r_ref):
    h = _rms(x_ref[...], g_ref[...]) * (1.0 + sc_ref[...]) + sh_ref[...]
    h_ref[...] = h
    logits = jnp.dot(h, wr_ref[...], precision=lax.Precision.HIGHEST, preferred_element_type=F32) + br_ref[...]
    lane = lax.broadcasted_iota(jnp.int32, logits.shape, 1).astype(F32)
    ninf = -jnp.inf
    big = float(LANE)
    lg = jnp.where(lane < N_GROUPS, logits, ninf)
    gmax = jnp.max(lg, axis=-1, keepdims=True)
    gsel = jnp.min(jnp.where(lg == gmax, lane, big), axis=-1, keepdims=True)
    pg = 1.0 / jnp.sum(jnp.where(lane < N_GROUPS, jnp.exp(logits - gmax), 0.0), axis=-1, keepdims=True)
    lo = N_GROUPS + EXPERTS_PER_GROUP * gsel
    le = jnp.where(jnp.logical_and(lane >= lo, lane < lo + EXPERTS_PER_GROUP), logits, ninf)
    v1 = jnp.max(le, axis=-1, keepdims=True)
    i1 = jnp.min(jnp.where(le == v1, lane, big), axis=-1, keepdims=True)
    le2 = jnp.where(lane == i1, ninf, le)
    v2 = jnp.max(le2, axis=-1, keepdims=True)
    i2 = jnp.min(jnp.where(le2 == v2, lane, big), axis=-1, keepdims=True)
    e2 = jnp.exp(v2 - v1)
    w1 = pg / (1.0 + e2)
    w2 = pg * e2 / (1.0 + e2)
    r_ref[...] = jnp.where(lane == 0, i1 - N_GROUPS,
                           jnp.where(lane == 1, i2 - N_GROUPS,
                                     jnp.where(lane == 2, w1, jnp.where(lane == 3, w2, 0.0))))


def _router(g, x, gain, mod, w_route, b_route, layer):
    gm = g._replace(tm=min(g.tm, 256))
    sds = jax.ShapeDtypeStruct
    return pl.pallas_call(
        _router_kernel,
        grid=(gm.n_tiles, 1),
        in_specs=[_nat_spec(gm, D_MODEL),
                  pl.BlockSpec((1, D_MODEL), lambda i, j: (0, 0)),
                  _mod_spec(gm, 4, D_MODEL),
                  _mod_spec(gm, 3, D_MODEL),
                  pl.BlockSpec((None, D_MODEL, LANE), lambda i, j: (layer, 0, 0)),
                  pl.BlockSpec((None, 1, LANE), lambda i, j: (layer, 0, 0))],
        out_specs=(_nat_spec(gm, D_MODEL), _nat_spec(gm, LANE)),
        out_shape=(sds((g.T, D_MODEL), F32), sds((g.T, LANE), F32)),
        compiler_params=_params("arbitrary", "arbitrary"),
        name="moe_router",
    )(x, gain.reshape(1, D_MODEL), mod, mod, w_route, b_route)


def _row_copy(src_hbm, row, dst, dst_row, sem):
    return pltpu.make_async_copy(src_hbm.at[pl.ds(row, 1)], dst.at[pl.ds(dst_row, 1)], sem)


def _experts_kernel(be_ref, tok_ref, nact_ref, first_ref, nxt_ref, par_ref, h_hbm, wg_hbm, wu_hbm, wd_hbm, y_ref,
                    xbuf, gsem, wst_g, wst_u, wst_d, wsem, wgb, wub, wdb, *, layer):
    b = pl.program_id(0)
    nact = nact_ref[0]
    rb = MOE_ROWS

    def gather(blk, slot):
        for r in range(rb):
            _row_copy(h_hbm, tok_ref[blk * rb + r], xbuf.at[slot], r, gsem.at[slot]).start(priority=1)

    def weight_copies(e, slot):
        return (pltpu.make_async_copy(wg_hbm.at[layer, e], wst_g.at[slot], wsem.at[slot]),
                pltpu.make_async_copy(wu_hbm.at[layer, e], wst_u.at[slot], wsem.at[slot]),
                pltpu.make_async_copy(wd_hbm.at[layer, e], wst_d.at[slot], wsem.at[slot]))

    @pl.when(b == 0)
    def _():
        for cp in weight_copies(be_ref[0], par_ref[0]):
            cp.start()
        gather(0, 0)

    @pl.when(b + 1 < nact)
    def _():
        gather(b + 1, (b + 1) % 2)

    @pl.when(jnp.logical_and(b < nact, first_ref[b] == 1))
    def _():
        slot = par_ref[b]
        for cp in weight_copies(be_ref[b], slot):
            cp.wait()

        @pl.when(nxt_ref[b] >= 0)
        def _():
            for cp in weight_copies(nxt_ref[b], 1 - slot):
                cp.start()

        wgb[...] = wst_g[slot].astype(MXU_DTYPE)
        wub[...] = wst_u[slot].astype(MXU_DTYPE)
        wdb[...] = wst_d[slot].astype(MXU_DTYPE)

    @pl.when(b < nact)
    def _():
        slot = b % 2
        for r in range(rb):
            _row_copy(h_hbm, 0, xbuf.at[slot], r, gsem.at[slot]).wait()
        x = xbuf[slot].astype(MXU_DTYPE)
        hid = _silu(_mm(x, wgb[...])) * _mm(x, wub[...])
        y_ref[...] = _mm(hid, wdb[...])

    @pl.when(b >= nact)
    def _():
        y_ref[...] = jnp.zeros_like(y_ref)


def _experts(h_all, plan, w_gate, w_up, w_down, layer):
    block_e, row_token, n_active, first, nxt, par = plan
    n_blocks = block_e.shape[0]
    rb = MOE_ROWS
    any_spec = pl.BlockSpec(memory_space=pl.ANY)
    return pl.pallas_call(
        functools.partial(_experts_kernel, layer=layer),
        grid_spec=pltpu.PrefetchScalarGridSpec(
            num_scalar_prefetch=6,
            grid=(n_blocks,),
            in_specs=[any_spec, any_spec, any_spec, any_spec],
            out_specs=pl.BlockSpec((rb, D_MODEL), lambda b, *_: (b, 0)),
            scratch_shapes=[pltpu.VMEM((2, rb, D_MODEL), F32), pltpu.SemaphoreType.DMA((2,)),
                            pltpu.VMEM((2, D_MODEL, D_EXPERT), F32), pltpu.VMEM((2, D_MODEL, D_EXPERT), F32),
                            pltpu.VMEM((2, D_EXPERT, D_MODEL), F32), pltpu.SemaphoreType.DMA((2,)),
                            pltpu.VMEM((D_MODEL, D_EXPERT), MXU_DTYPE), pltpu.VMEM((D_MODEL, D_EXPERT), MXU_DTYPE),
                            pltpu.VMEM((D_EXPERT, D_MODEL), MXU_DTYPE)]),
        out_shape=jax.ShapeDtypeStruct((n_blocks * rb, D_MODEL), F32),
        compiler_params=_params("arbitrary"),
        name="moe_experts",
    )(block_e, row_token, n_active, first, nxt, par, h_all, w_gate, w_up, w_down)


def _combine_kernel(pos_ref, y_hbm, x_ref, r_ref, g_ref, o_ref, buf, sem, *, tok0, tm):
    i = pl.program_id(0)
    n = pl.num_programs(0)

    def gather(tile, slot):
        for r in range(tm):
            for k in range(TOP_K):
                p = pos_ref[(tok0 + tile * tm + r) * TOP_K + k]
                _row_copy(y_hbm, p, buf.at[slot], k * tm + r, sem.at[slot]).start()

    @pl.when(i == 0)
    def _():
        gather(0, 0)

    @pl.when(i + 1 < n)
    def _():
        gather(i + 1, (i + 1) % 2)

    slot = i % 2
    for r in range(TOP_K * tm):
        _row_copy(y_hbm, 0, buf.at[slot], r, sem.at[slot]).wait()
    route = r_ref[...]
    moe = buf[slot, pl.ds(0, tm), :] * route[:, 2:3] + buf[slot, pl.ds(tm, tm), :] * route[:, 3:4]
    o_ref[...] = x_ref[...] + g_ref[...] * moe


def _combine(g, tok0, pos, y_rows, x, route, mod):
    tm = min(COMBINE_ROWS, g.tm)
    gc = g._replace(tm=tm)
    nt = gc.nt
    if g.L == 1:
        gate_spec = pl.BlockSpec((tm, D_MODEL), lambda i, pos: (i, 5))
    else:
        gate_spec = pl.BlockSpec((None, 1, D_MODEL), lambda i, pos: (i // nt, 0, 5))
    return pl.pallas_call(
        functools.partial(_combine_kernel, tok0=tok0, tm=tm),
        grid_spec=pltpu.PrefetchScalarGridSpec(
            num_scalar_prefetch=1,
            grid=(gc.n_tiles,),
            in_specs=[pl.BlockSpec(memory_space=pl.ANY),
                      pl.BlockSpec((tm, D_MODEL), lambda i, pos: (i, 0)),
                      pl.BlockSpec((tm, LANE), lambda i, pos: (i, 0)),
                      gate_spec],
            out_specs=pl.BlockSpec((tm, D_MODEL), lambda i, pos: (i, 0)),
            scratch_shapes=[pltpu.VMEM((2, TOP_K * tm, D_MODEL), F32), pltpu.SemaphoreType.DMA((2,))]),
        out_shape=jax.ShapeDtypeStruct((g.T, D_MODEL), F32),
        compiler_params=_params("arbitrary"),
        name="moe_combine",
    )(pos, y_rows, x, route, mod)


def _dispatch_plan(route_all):
    rb = MOE_ROWS
    n_tok = route_all.shape[0]
    n_assign = n_tok * TOP_K
    n_blocks = -(-(n_assign + N_EXPERTS * (rb - 1)) // rb)
    flat_e = route_all[:, :TOP_K].astype(jnp.int32).reshape(-1)
    onehot = (flat_e[:, None] == jnp.arange(N_EXPERTS, dtype=jnp.int32)[None, :]).astype(jnp.int32)
    csum = jnp.cumsum(onehot, axis=0)
    counts = csum[-1]
    rank = jnp.sum(onehot * csum, axis=1) - 1
    padded = (counts + rb - 1) // rb * rb
    pad_end = jnp.cumsum(padded)
    pad_start = pad_end - padded
    start = jnp.cumsum(counts) - counts
    pos = pad_start[flat_e] + rank
    n_active = (pad_end[-1] // rb).astype(jnp.int32)
    blk = jnp.minimum(jnp.arange(n_blocks, dtype=jnp.int32), n_active - 1)
    block_e = jnp.minimum(jnp.searchsorted(pad_end, blk * rb, side='right'), N_EXPERTS - 1).astype(jnp.int32)
    order = jnp.argsort(flat_e).astype(jnp.int32)
    row = jnp.arange(n_blocks * rb, dtype=jnp.int32)
    row_e = jnp.repeat(block_e, rb)
    idx = row - pad_start[row_e]
    src = jnp.clip(start[row_e] + idx, 0, n_assign - 1)
    row_token = jnp.where(jnp.logical_and(idx < counts[row_e], row < pad_end[-1]), order[src] // TOP_K, 0)
    bidx = jnp.arange(n_blocks, dtype=jnp.int32)
    first = jnp.logical_and(bidx < n_active, jnp.logical_or(bidx == 0, block_e != jnp.roll(block_e, 1)))
    par = (jnp.cumsum(first.astype(jnp.int32)) - 1) % 2
    nxt_first = lax.cummin(jnp.where(first, bidx, n_blocks)[::-1])[::-1]
    nxt_idx = jnp.concatenate([nxt_first[1:], jnp.full((1,), n_blocks, jnp.int32)])
    nxt = jnp.where(nxt_idx < n_blocks, block_e[jnp.minimum(nxt_idx, n_blocks - 1)], -1)
    plan = (block_e, row_token.astype(jnp.int32), n_active.reshape(1), first.astype(jnp.int32),
            nxt.astype(jnp.int32), par.astype(jnp.int32))
    return plan, pos.astype(jnp.int32)


def _layer(layer, groups, xs, mods, states, p, state_dn):
    hs, new_states = [], []
    for g, x, mod, st in zip(groups, xs, mods, states):
        s5_re, s5_im, dn_conv, lru_h, lru_conv = st
        single = g.L == 1
        h = _norm_mod(g, x, p['norm1_g'][layer], mod, 0, 1)
        uq = _proj(g, h, p['w_a'], layer, 0, BRANCH_W + 3 * DN_W, 512)
        ab = _proj(g, h, p['w_a'], layer, COL_AB, LANE, LANE)
        rest = _proj(g, h, p['w_rest'], layer, 0, REST_GATE, 512)

        ya, n_re, n_im = _s5(g, uq, s5_re, s5_im, p['s5_disc'], p['s5_cr'], p['s5_ci'], p['s5_d'],
                             p['s5_w_glu'], p['s5_b_glu'], layer)
        if single:
            qkv = uq[:, BRANCH_W:]
            lru_x = rest[:, DN_W:DN_W + LRU_W]
            yb, n_dn = _dn_step(g, uq, dn_conv, rest, ab, p['dn_conv_w'], p['dn_prm'], p['dn_norm_g'], state_dn, layer)
            n_dn_conv = jnp.concatenate([dn_conv[:, 1:], qkv[:, None, :]], axis=1)
            n_lru_conv = jnp.concatenate([lru_conv[:, 1:], lru_x[:, None, :]], axis=1)
            yc, n_lru = _lru(g, rest, lru_conv, lru_h, p, layer, pos0=p['past_len'])
        else:
            yb, n_dn = _dn_seq(g, uq, rest, ab, p['dn_conv_w'], p['dn_prm'], p['dn_norm_g'], layer)
            tail = lambda a: a.reshape(g.B, g.L, -1)[:, g.L - (CONV_W - 1):]
            n_dn_conv = tail(uq)[:, :, BRANCH_W:]
            n_lru_conv = tail(rest)[:, :, DN_W:DN_W + LRU_W]
            yc, n_lru = _lru(g, rest, None, None, p, layer, pos0=0)

        merged = _merge(g, h, ya, yb, yc, p['w_rest'], p['w_branch'], layer)
        x1 = _out_proj(g, merged, p['w_out'], layer, x, mod, 2)
        h2, route = _router(g, x1, p['norm2_g'][layer], mod, p['w_route'], p['b_route'], layer)
        hs.append((x1, h2, route))
        new_states.append((n_re.reshape(g.B, S5_GROUPS, S5_STATE), n_im.reshape(g.B, S5_GROUPS, S5_STATE),
                           n_dn, n_dn_conv, n_lru, n_lru_conv))

    h_all = jnp.concatenate([t[1] for t in hs], axis=0)
    route_all = jnp.concatenate([t[2] for t in hs], axis=0)
    plan, pos = _dispatch_plan(route_all)
    y_rows = _experts(h_all, plan, p['w_e_gate'], p['w_e_up'], p['w_e_down'], layer)
    outs, tok0 = [], 0
    for g, (x1, _, route), mod in zip(groups, hs, mods):
        outs.append(_combine(g, tok0, pos, y_rows, x1, route, mod))
        tok0 += g.T
    return outs, new_states


def kernel(x_prompt, x_sample, c_prompt, c_sample, state_s5_re, state_s5_im, state_dn, state_dn_conv, state_lru, state_lru_conv, final_g, norm1_g, norm2_g, w_ada, b_ada, w_in, s5_log_dt, s5_a_re, s5_a_im, s5_b_re, s5_b_im, s5_c_re, s5_c_im, s5_d, s5_w_glu, s5_b_glu, dn_conv_w, dn_a_log, dn_dt_bias, dn_norm_g, lru_conv_w, lru_conv_b, lru_w_a, lru_b_a, lru_w_x, lru_b_x, lru_lambda, w_branch, w_out, w_rg, b_rg, w_re, b_re, w_e_gate, w_e_up, w_e_down):
    depth = w_in.shape[0]
    bp, lp, _ = x_prompt.shape
    bs, ls, _ = x_sample.shape
    assert ls == 1 and bs % SUBLANE == 0 and lp % SEQ_CHUNK == 0
    gp = _Group(bp, lp, min(lp, 1024))
    gs = _Group(bs, 1, bs)
    groups = (gp, gs)

    pad = (-bp) % SUBLANE
    c_all = jnp.concatenate([c_prompt, jnp.zeros((pad, D_MODEL), F32), c_sample], axis=0)
    mod_all = _ada_mod(c_all, w_ada, b_ada)

    prm = jnp.zeros((depth, 2, LANE), F32)
    prm = prm.at[:, 0, :DN_HEADS].set(dn_a_log).at[:, 1, :DN_HEADS].set(dn_dt_bias)
    w_route = jnp.concatenate([w_rg, w_re, jnp.zeros((depth, D_MODEL, LANE - N_GROUPS - N_EXPERTS), F32)], axis=-1)
    b_route = jnp.concatenate([b_rg, b_re, jnp.zeros((depth, LANE - N_GROUPS - N_EXPERTS), F32)], axis=-1)
    p = dict(norm1_g=norm1_g, norm2_g=norm2_g,
             w_a=w_in[:, :, :COL_AB + LANE].astype(MXU_DTYPE), w_rest=w_in[:, :, COL_REST:].astype(MXU_DTYPE),
             s5_disc=_s5_discretize(s5_log_dt, s5_a_re, s5_a_im, s5_b_re, s5_b_im),
             s5_cr=_s5_out_blocks(s5_c_re), s5_ci=_s5_out_blocks(s5_c_im),
             s5_d=s5_d, s5_w_glu=s5_w_glu, s5_b_glu=s5_b_glu,
             dn_conv_w=dn_conv_w, dn_prm=prm, dn_norm_g=dn_norm_g,
             lru_conv_w=lru_conv_w, lru_conv_b=lru_conv_b, lru_wa_bd=_block_diag(lru_w_a), lru_b_a=lru_b_a,
             lru_wx_bd=_block_diag(lru_w_x), lru_b_x=lru_b_x, lru_lambda=lru_lambda,
             w_branch=w_branch, w_out=w_out, w_route=w_route, b_route=b_route.reshape(depth, 1, LANE),
             w_e_gate=w_e_gate, w_e_up=w_e_up, w_e_down=w_e_down, past_len=PAST_LEN)

    xs = [x_prompt.reshape(gp.T, D_MODEL), x_sample.reshape(gs.T, D_MODEL)]
    per_layer = []
    for l in range(depth):
        mods = [mod_all[l, :bp].reshape(bp, 1, 6 * D_MODEL), mod_all[l, bp + pad:]]
        states = [(None, None, None, None, None),
                  (state_s5_re[l].reshape(bs, S5_N), state_s5_im[l].reshape(bs, S5_N), state_dn_conv[l],
                   state_lru[l], state_lru_conv[l])]
        xs, new_states = _layer(l, groups, xs, mods, states, p, state_dn)
        per_layer.append(new_states)

    y_prompt = _final_norm(gp, xs[0], final_g).reshape(bp, lp, D_MODEL)
    y_sample = _final_norm(gs, xs[1], final_g).reshape(bs, 1, D_MODEL)
    stack = lambda gi: tuple(jnp.stack([per_layer[l][gi][k] for l in range(depth)]) for k in range(6))
    return (y_prompt, y_sample) + stack(0) + stack(1)
```

```python
import functools
import math
from typing import NamedTuple

import jax
import jax.numpy as jnp
from jax import lax
from jax.experimental import pallas as pl
from jax.experimental.pallas import tpu as pltpu

F32 = jnp.float32
MXU_DTYPE = jnp.bfloat16

D_MODEL = 2048
BRANCH_W = D_MODEL // 4
S5_GROUP = 16
S5_GROUPS = BRANCH_W // S5_GROUP
S5_STATE = 64
S5_N = S5_GROUPS * S5_STATE
S5_BLOCKS = 4
DN_HEADS = 4
DN_HEAD_DIM = 128
DN_W = DN_HEADS * DN_HEAD_DIM
DN_CHUNK = 64
CONV_W = 4
LRU_W = BRANCH_W
LRU_BLOCKS = 4
LRU_C = 8.0
N_GROUPS = 4
EXPERTS_PER_GROUP = 8
N_EXPERTS = N_GROUPS * EXPERTS_PER_GROUP
TOP_K = 2
D_EXPERT = D_MODEL // 4
NORM_EPS = 1e-6
PAST_LEN = 16384

COL_AB = BRANCH_W + 3 * DN_W
COL_REST = COL_AB + 2 * DN_HEADS
REST_GATE = DN_W + 2 * LRU_W

LANE = 128
SUBLANE = 8
VMEM_LIMIT = 56 * 1024 * 1024
SEQ_CHUNK = 256
MOE_ROWS = 256
COMBINE_ROWS = 128
XPOSE_CHUNK = 256


class _Group(NamedTuple):
    B: int
    L: int
    tm: int

    @property
    def T(self):
        return self.B * self.L

    @property
    def nt(self):
        return max(self.L // self.tm, 1)

    @property
    def n_tiles(self):
        return self.T // self.tm


def _params(*sem):
    return pltpu.CompilerParams(dimension_semantics=sem, vmem_limit_bytes=VMEM_LIMIT)


def _mm(a, b):
    return jnp.dot(a.astype(MXU_DTYPE), b.astype(MXU_DTYPE), preferred_element_type=F32)


def _mm_nt(a, b):
    return lax.dot_general(a.astype(MXU_DTYPE), b.astype(MXU_DTYPE), (((1,), (1,)), ((), ())),
                           preferred_element_type=F32)


def _mm_tn(a, b):
    return lax.dot_general(a.astype(MXU_DTYPE), b.astype(MXU_DTYPE), (((0,), (0,)), ((), ())),
                           preferred_element_type=F32)


def _silu(x):
    return x * jax.nn.sigmoid(x)


def _gelu_tanh(x):
    return 0.5 * x * (1.0 + jnp.tanh(math.sqrt(2.0 / math.pi) * (x + 0.044715 * (x * x * x))))


def _softplus(x):
    return jnp.maximum(x, 0.0) + jnp.log1p(jnp.exp(-jnp.abs(x)))


def _rms(x, gain):
    return x * lax.rsqrt(jnp.mean(x * x, axis=-1, keepdims=True) + NORM_EPS) * gain


def _row_iota(shape):
    return lax.broadcasted_iota(jnp.int32, shape, len(shape) - 2)


def _last_row(x):
    return jnp.broadcast_to(x[SUBLANE - 1:SUBLANE], x.shape)


def _nat_spec(g, tn, col0=0):
    return pl.BlockSpec((g.tm, tn), lambda i, j: (i, col0 + j))


def _mod_spec(g, k, tn):
    nb, nt = D_MODEL // tn, g.nt
    if g.L == 1:
        return pl.BlockSpec((g.tm, tn), lambda i, j: (i, k * nb + j))
    return pl.BlockSpec((None, 1, tn), lambda i, j: (i // nt, 0, k * nb + j))


def _ada_kernel(c_ref, w_ref, b_ref, o_ref):
    o_ref[...] = _mm(_silu(c_ref[...]), w_ref[...]) + b_ref[...]


def _ada_mod(c_all, w_ada, b_ada):
    depth, _, n6 = w_ada.shape
    rows = c_all.shape[0]
    tn = 1024
    return pl.pallas_call(
        _ada_kernel,
        grid=(depth, n6 // tn),
        in_specs=[pl.BlockSpec((rows, D_MODEL), lambda l, j: (0, 0)),
                  pl.BlockSpec((None, D_MODEL, tn), lambda l, j: (l, 0, j)),
                  pl.BlockSpec((None, 1, tn), lambda l, j: (l, 0, j))],
        out_specs=pl.BlockSpec((None, rows, tn), lambda l, j: (l, 0, j)),
        out_shape=jax.ShapeDtypeStruct((depth, rows, n6), F32),
        compiler_params=_params("arbitrary", "arbitrary"),
        name="ada_mod",
    )(c_all, w_ada, b_ada.reshape(depth, 1, n6))


def _norm_mod_kernel(x_ref, g_ref, sc_ref, sh_ref, o_ref):
    y = _rms(x_ref[...], g_ref[...])
    o_ref[...] = (y * (1.0 + sc_ref[...]) + sh_ref[...]).astype(o_ref.dtype)


def _norm_mod(g, x, gain, mod, k_shift, k_scale):
    gm = g._replace(tm=min(g.tm, 512))
    return pl.pallas_call(
        _norm_mod_kernel,
        grid=(gm.n_tiles, 1),
        in_specs=[_nat_spec(gm, D_MODEL),
                  pl.BlockSpec((1, D_MODEL), lambda i, j: (0, 0)),
                  _mod_spec(gm, k_scale, D_MODEL),
                  _mod_spec(gm, k_shift, D_MODEL)],
        out_specs=_nat_spec(gm, D_MODEL),
        out_shape=jax.ShapeDtypeStruct((g.T, D_MODEL), MXU_DTYPE),
        compiler_params=_params("arbitrary", "arbitrary"),
        name="norm_mod",
    )(x, gain.reshape(1, D_MODEL), mod, mod)


def _final_norm_kernel(x_ref, g_ref, o_ref):
    o_ref[...] = _rms(x_ref[...], g_ref[...])


def _final_norm(g, x, gain):
    gm = g._replace(tm=min(g.tm, 512))
    return pl.pallas_call(
        _final_norm_kernel,
        grid=(gm.n_tiles, 1),
        in_specs=[_nat_spec(gm, D_MODEL), pl.BlockSpec((1, D_MODEL), lambda i, j: (0, 0))],
        out_specs=_nat_spec(gm, D_MODEL),
        out_shape=jax.ShapeDtypeStruct((g.T, D_MODEL), F32),
        compiler_params=_params("arbitrary", "arbitrary"),
        name="final_norm",
    )(x, gain.reshape(1, D_MODEL))


def _proj_kernel(a_ref, w_ref, o_ref):
    o_ref[...] = _mm(a_ref[...], w_ref[...]).astype(o_ref.dtype)


def _proj(g, a, w, layer, col0, n, tn):
    cb = col0 // tn
    return pl.pallas_call(
        _proj_kernel,
        grid=(g.n_tiles, n // tn),
        in_specs=[pl.BlockSpec((g.tm, D_MODEL), lambda i, j: (i, 0)),
                  pl.BlockSpec((None, D_MODEL, tn), lambda i, j: (layer, 0, cb + j))],
        out_specs=_nat_spec(g, tn),
        out_shape=jax.ShapeDtypeStruct((g.T, n), F32),
        compiler_params=_params("arbitrary", "arbitrary"),
        name="in_proj",
    )(a, w)


def _out_proj_kernel(a_ref, w_ref, x_ref, g_ref, o_ref):
    o_ref[...] = x_ref[...] + g_ref[...] * _mm(a_ref[...], w_ref[...])


def _out_proj(g, merged, w_out, layer, x, mod, k_gate):
    tn = 512
    return pl.pallas_call(
        _out_proj_kernel,
        grid=(g.n_tiles, D_MODEL // tn),
        in_specs=[pl.BlockSpec((g.tm, D_MODEL), lambda i, j: (i, 0)),
                  pl.BlockSpec((None, D_MODEL, tn), lambda i, j: (layer, 0, j)),
                  _nat_spec(g, tn),
                  _mod_spec(g, k_gate, tn)],
        out_specs=_nat_spec(g, tn),
        out_shape=jax.ShapeDtypeStruct((g.T, D_MODEL), F32),
        compiler_params=_params("arbitrary", "arbitrary"),
        name="out_proj",
    )(merged, w_out, x, mod)


def _merge_kernel(h_ref, ya_ref, yb_ref, yc_ref, wg0, wg1, wg2, wb0, wb1, wb2, o_ref):
    h = h_ref[...]
    acc = None
    for y_ref, wg, wb in ((ya_ref, wg0, wb0), (yb_ref, wg1, wb1), (yc_ref, wg2, wb2)):
        t = jax.nn.sigmoid(_mm(h, wg[...])) * _mm(y_ref[...], wb[...])
        acc = t if acc is None else acc + t
    o_ref[...] = acc.astype(o_ref.dtype)


def _merge(g, h, ya, yb, yc, w_rest, w_branch, layer):
    tn = 256
    gate_specs = [pl.BlockSpec((None, D_MODEL, tn),
                               functools.partial(lambda i, j, c: (layer, 0, c + j), c=(REST_GATE + n * D_MODEL) // tn))
                  for n in range(3)]
    br_specs = [pl.BlockSpec((None, None, BRANCH_W, tn), functools.partial(lambda i, j, n: (layer, n, 0, j), n=n))
                for n in range(3)]
    y_spec = pl.BlockSpec((g.tm, BRANCH_W), lambda i, j: (i, 0))
    return pl.pallas_call(
        _merge_kernel,
        grid=(g.n_tiles, D_MODEL // tn),
        in_specs=[pl.BlockSpec((g.tm, D_MODEL), lambda i, j: (i, 0)), y_spec, y_spec, y_spec] + gate_specs + br_specs,
        out_specs=_nat_spec(g, tn),
        out_shape=jax.ShapeDtypeStruct((g.T, D_MODEL), MXU_DTYPE),
        compiler_params=_params("arbitrary", "arbitrary"),
        name="merge",
    )(h, ya, yb, yc, w_rest, w_rest, w_rest, w_branch, w_branch, w_branch)


def _s5_disc_kernel(ldt_ref, ar_ref, ai_ref, br_ref, bi_ref, lr_ref, li_ref, bbr_ref, bbi_ref):
    dt = jnp.exp(ldt_ref[...])
    ar, ai = ar_ref[...], ai_ref[...]
    mag = jnp.exp(ar * dt)
    lr = mag * jnp.cos(ai * dt)
    li = mag * jnp.sin(ai * dt)
    den = ar * ar + ai * ai
    fr = ((lr - 1.0) * ar + li * ai) / den
    fi = (li * ar - (lr - 1.0) * ai) / den
    lr_ref[...] = lr
    li_ref[...] = li
    br, bi = br_ref[...], bi_ref[...]
    bbr_ref[...] = fr * br - fi * bi
    bbi_ref[...] = fr * bi + fi * br


def _s5_discretize(log_dt, a_re, a_im, b_re, b_im):
    depth = log_dt.shape[0]
    n = depth * S5_GROUPS
    rows = n * S5_GROUP
    rep = lambda v, w: jnp.repeat(v.reshape(n, w), S5_GROUP, axis=0)
    brt = jnp.swapaxes(b_re, -1, -2).reshape(rows, S5_STATE)
    bit = jnp.swapaxes(b_im, -1, -2).reshape(rows, S5_STATE)
    sds = jax.ShapeDtypeStruct
    lr, li, bbr, bbi = pl.pallas_call(
        _s5_disc_kernel,
        out_shape=tuple(sds((rows, S5_STATE), F32) for _ in range(4)),
        name="s5_discretize",
    )(rep(log_dt, 1), rep(a_re, S5_STATE), rep(a_im, S5_STATE), brt, bit)
    gpb = S5_GROUPS // S5_BLOCKS
    eye = jnp.eye(gpb, dtype=F32)

    def blocks_in(m):
        m = m.reshape(depth, S5_BLOCKS, gpb, S5_GROUP, S5_STATE)
        m = jnp.einsum('ljgnp,gh->ljgnhp', m, eye)
        return m.reshape(depth, S5_BLOCKS, gpb * S5_GROUP, gpb * S5_STATE).astype(MXU_DTYPE)

    lam = lambda v: v.reshape(n, S5_GROUP, S5_STATE)[:, 0].reshape(depth, 1, S5_N)
    return lam(lr), lam(li), blocks_in(bbr), blocks_in(bbi)


def _s5_out_blocks(c):
    depth = c.shape[0]
    gpb = S5_GROUPS // S5_BLOCKS
    eye = jnp.eye(gpb, dtype=F32)
    m = jnp.einsum('ljgnp,gh->ljgphn', c.reshape(depth, S5_BLOCKS, gpb, S5_GROUP, S5_STATE), eye)
    return m.reshape(depth, S5_BLOCKS, gpb * S5_STATE, gpb * S5_GROUP).astype(MXU_DTYPE)


def _s5_in(u, bb_ref):
    cw, sw = BRANCH_W // S5_BLOCKS, S5_N // S5_BLOCKS
    return [_mm(u[:, j * cw:(j + 1) * cw], bb_ref[j]) for j in range(S5_BLOCKS)], sw


def _s5_out(hr, hi, u, cr_ref, ci_ref, d_ref, wglu_ref, bglu_ref):
    sw = S5_N // S5_BLOCKS
    y = jnp.concatenate([_mm(hr[:, j * sw:(j + 1) * sw], cr_ref[j]) - _mm(hi[:, j * sw:(j + 1) * sw], ci_ref[j])
                         for j in range(S5_BLOCKS)], axis=1)
    y = _gelu_tanh(y + d_ref[...] * u)
    return y * jax.nn.sigmoid(_mm(y, wglu_ref[...]) + bglu_ref[...])


def _scan_tiles(n_tiles, cw, width, tile_fn, load_state, store_state):
    for c in range(width // cw):
        cs = pl.ds(c * cw, cw)

        def body(k, carry, cs=cs):
            return tile_fn(pl.ds(pl.multiple_of(k * SUBLANE, SUBLANE), SUBLANE), cs, carry)

        store_state(cs, lax.fori_loop(0, n_tiles, body, load_state(cs)))


def _s5_seq_kernel(u_ref, bbr_ref, bbi_ref, lr_ref, li_ref, cr_ref, ci_ref, d_ref, wglu_ref, bglu_ref,
                   y_ref, hlr_ref, hli_ref, xr_s, xi_s, cr_s, ci_s, pw_s):
    c = pl.program_id(1)
    lc = u_ref.shape[0]

    @pl.when(c == 0)
    def _():
        cr_s[...] = jnp.zeros_like(cr_s)
        ci_s[...] = jnp.zeros_like(ci_s)
        lr, li = lr_ref[...], li_ref[...]
        row = _row_iota((SUBLANE, S5_N))
        pr, pi = lr, li
        acc_r = jnp.zeros((SUBLANE, S5_N), F32)
        acc_i = jnp.zeros((SUBLANE, S5_N), F32)
        for r in range(SUBLANE):
            acc_r = jnp.where(row == r, pr, acc_r)
            acc_i = jnp.where(row == r, pi, acc_i)
            if r + 1 in (1, 2, 4):
                k = (1, 2, 4).index(r + 1)
                pw_s[2 * k] = jnp.where(row >= r + 1, pr, 0.0)
                pw_s[2 * k + 1] = jnp.where(row >= r + 1, pi, 0.0)
            pr, pi = pr * lr - pi * li, pr * li + pi * lr
        pw_s[6] = acc_r
        pw_s[7] = acc_i

    u = u_ref[...]
    xr, sw = _s5_in(u, bbr_ref)
    xi, _ = _s5_in(u, bbi_ref)
    for j in range(S5_BLOCKS):
        xr_s[:, pl.ds(j * sw, sw)] = xr[j]
        xi_s[:, pl.ds(j * sw, sw)] = xi[j]

    def tile(rows, cs, carry):
        pr0, pi0 = carry
        hr, hi = xr_s[rows, cs], xi_s[rows, cs]
        for k, s in enumerate((1, 2, 4)):
            ar, ai = pw_s[2 * k, :, cs], pw_s[2 * k + 1, :, cs]
            sr, si = pltpu.roll(hr, s, 0), pltpu.roll(hi, s, 0)
            hr, hi = hr + ar * sr - ai * si, hi + ar * si + ai * sr
        ar, ai = pw_s[6, :, cs], pw_s[7, :, cs]
        hr, hi = hr + ar * pr0 - ai * pi0, hi + ar * pi0 + ai * pr0
        xr_s[rows, cs] = hr
        xi_s[rows, cs] = hi
        return _last_row(hr), _last_row(hi)

    def load(cs):
        return cr_s[:, cs], ci_s[:, cs]

    def store(cs, carry):
        cr_s[:, cs], ci_s[:, cs] = carry

    _scan_tiles(lc // SUBLANE, 512, S5_N, tile, load, store)

    y_ref[...] = _s5_out(xr_s[...], xi_s[...], u, cr_ref, ci_ref, d_ref, wglu_ref, bglu_ref).astype(y_ref.dtype)

    @pl.when(c == pl.num_programs(1) - 1)
    def _():
        hlr_ref[...] = cr_s[pl.ds(0, 1), :]
        hli_ref[...] = ci_s[pl.ds(0, 1), :]


def _s5_step_kernel(u_ref, h0r_ref, h0i_ref, bbr_ref, bbi_ref, lr_ref, li_ref, cr_ref, ci_ref, d_ref, wglu_ref,
                    bglu_ref, y_ref, hlr_ref, hli_ref):
    u = u_ref[...]
    xr, _ = _s5_in(u, bbr_ref)
    xi, _ = _s5_in(u, bbi_ref)
    lr, li, sr, si = lr_ref[...], li_ref[...], h0r_ref[...], h0i_ref[...]
    hr = lr * sr - li * si + jnp.concatenate(xr, axis=1)
    hi = lr * si + li * sr + jnp.concatenate(xi, axis=1)
    hlr_ref[...] = hr
    hli_ref[...] = hi
    y_ref[...] = _s5_out(hr, hi, u, cr_ref, ci_ref, d_ref, wglu_ref, bglu_ref).astype(y_ref.dtype)


def _s5(g, uq, h0r, h0i, disc, cr, ci, d, w_glu, b_glu, layer):
    lam_r, lam_i, bbr, bbi = disc
    cw, sw = BRANCH_W // S5_BLOCKS, S5_N // S5_BLOCKS
    sds = jax.ShapeDtypeStruct
    seq = g.L > 1
    lc = min(g.L, SEQ_CHUNK) if seq else g.B
    nc = g.L // lc if seq else 1
    grid = (g.B, nc) if seq else (1, 1)
    row = (lambda b, c: (b * nc + c, 0))
    lay3 = lambda b, c: (layer, 0, 0)
    lay4 = lambda b, c: (layer, 0, 0, 0)
    w_specs = [pl.BlockSpec((None, S5_BLOCKS, cw, sw), lay4), pl.BlockSpec((None, S5_BLOCKS, cw, sw), lay4),
               pl.BlockSpec((None, 1, S5_N), lay3), pl.BlockSpec((None, 1, S5_N), lay3),
               pl.BlockSpec((None, S5_BLOCKS, sw, cw), lay4), pl.BlockSpec((None, S5_BLOCKS, sw, cw), lay4),
               pl.BlockSpec((None, 1, BRANCH_W), lay3),
               pl.BlockSpec((None, BRANCH_W, BRANCH_W), lay3),
               pl.BlockSpec((None, 1, BRANCH_W), lay3)]
    w_args = (bbr, bbi, lam_r, lam_i, cr, ci, d.reshape(-1, 1, BRANCH_W), w_glu, b_glu.reshape(-1, 1, BRANCH_W))
    if seq:
        st_spec = pl.BlockSpec((None, 1, S5_N), lambda b, c: (b, 0, 0))
        y, hlr, hli = pl.pallas_call(
            _s5_seq_kernel,
            grid=grid,
            in_specs=[pl.BlockSpec((lc, BRANCH_W), row)] + w_specs,
            out_specs=(pl.BlockSpec((lc, BRANCH_W), row), st_spec, st_spec),
            out_shape=(sds((g.T, BRANCH_W), MXU_DTYPE), sds((g.B, 1, S5_N), F32), sds((g.B, 1, S5_N), F32)),
            scratch_shapes=[pltpu.VMEM((lc, S5_N), F32), pltpu.VMEM((lc, S5_N), F32),
                            pltpu.VMEM((SUBLANE, S5_N), F32), pltpu.VMEM((SUBLANE, S5_N), F32),
                            pltpu.VMEM((8, SUBLANE, S5_N), F32)],
            compiler_params=_params("arbitrary", "arbitrary"),
            name="s5_ssm",
        )(uq, *w_args)
        return y, hlr.reshape(g.B, S5_N), hli.reshape(g.B, S5_N)
    st_spec = pl.BlockSpec((g.B, S5_N), lambda b, c: (0, 0))
    return pl.pallas_call(
        _s5_step_kernel,
        grid=grid,
        in_specs=[pl.BlockSpec((lc, BRANCH_W), row), st_spec, st_spec] + w_specs,
        out_specs=(pl.BlockSpec((lc, BRANCH_W), row), st_spec, st_spec),
        out_shape=(sds((g.T, BRANCH_W), MXU_DTYPE), sds((g.B, S5_N), F32), sds((g.B, S5_N), F32)),
        compiler_params=_params("arbitrary", "arbitrary"),
        name="s5_ssm",
    )(uq, h0r, h0i, *w_args)


def _lru_gates(xc, wa_ref, ba_ref, wx_ref, bx_ref, lam_ref):
    r = jax.nn.sigmoid(_mm(xc, wa_ref[...]) + ba_ref[...])
    ig = jax.nn.sigmoid(_mm(xc, wx_ref[...]) + bx_ref[...])
    log_a = -LRU_C * r * _softplus(-lam_ref[...])
    return jnp.exp(log_a), jnp.sqrt(1.0 - jnp.exp(2.0 * log_a)), ig


def _lru_seq_kernel(x_ref, g_ref, cw_ref, cbias_ref, wa_ref, ba_ref, wx_ref, bx_ref, lam_ref,
                    y_ref, hl_ref, xbuf, a_s, b_s, h_s):
    c = pl.program_id(1)
    lc = x_ref.shape[0]
    pad = SUBLANE

    @pl.when(c == 0)
    def _():
        xbuf[pl.ds(0, pad), :] = jnp.zeros((pad, LRU_W), F32)
        h_s[...] = jnp.zeros_like(h_s)

    xbuf[pl.ds(pad, lc), :] = x_ref[...]
    xc = cbias_ref[...]
    for j in range(CONV_W):
        xc = xc + cw_ref[pl.ds(j, 1), :] * xbuf[pl.ds(pad - (CONV_W - 1) + j, lc), :]
    new_tail = xbuf[pl.ds(lc, pad), :]
    xbuf[pl.ds(0, pad), :] = new_tail

    a, mult, ig = _lru_gates(xc, wa_ref, ba_ref, wx_ref, bx_ref, lam_ref)
    first = jnp.logical_and(c == 0, _row_iota((lc, LRU_W)) == 0)
    a_s[...] = jnp.where(first, 0.0, a)
    b_s[...] = jnp.where(first, 1.0, mult) * (ig * xc)

    def tile(rows, cs, h0):
        a, b = a_s[rows, cs], b_s[rows, cs]
        row = _row_iota(a.shape)
        for s in (1, 2, 4):
            keep = row >= s
            b = a * jnp.where(keep, pltpu.roll(b, s, 0), 0.0) + b
            a = a * jnp.where(keep, pltpu.roll(a, s, 0), 1.0)
        h = a * h0 + b
        b_s[rows, cs] = h
        return _last_row(h)

    def store(cs, h):
        h_s[:, cs] = h

    _scan_tiles(lc // SUBLANE, LRU_W, LRU_W, tile, lambda cs: h_s[:, cs], store)
    y_ref[...] = (b_s[...] * _gelu_tanh(g_ref[...])).astype(y_ref.dtype)

    @pl.when(c == pl.num_programs(1) - 1)
    def _():
        hl_ref[...] = h_s[pl.ds(0, 1), :]


def _lru_step_kernel(x_ref, g_ref, cs_ref, h0_ref, cw_ref, cbias_ref, wa_ref, ba_ref, wx_ref, bx_ref, lam_ref,
                     y_ref, hl_ref):
    x = x_ref[...]
    xc = cbias_ref[...] + cw_ref[pl.ds(CONV_W - 1, 1), :] * x
    for j in range(CONV_W - 1):
        xc = xc + cw_ref[pl.ds(j, 1), :] * cs_ref[j]
    a, mult, ig = _lru_gates(xc, wa_ref, ba_ref, wx_ref, bx_ref, lam_ref)
    h = a * h0_ref[...] + mult * (ig * xc)
    hl_ref[...] = h
    y_ref[...] = (h * _gelu_tanh(g_ref[...])).astype(y_ref.dtype)


def _block_diag(w):
    depth, nb, k, _ = w.shape
    eye = jnp.eye(nb, dtype=w.dtype)
    return jnp.einsum('lnwv,nm->lnwmv', w, eye).reshape(depth, nb * k, nb * k).astype(MXU_DTYPE)


def _lru(g, rest, conv_state, h0, p, layer, pos0):
    sds = jax.ShapeDtypeStruct
    seq = g.L > 1
    lc = min(g.L, SEQ_CHUNK) if seq else g.B
    nc = g.L // lc if seq else 1
    lay3 = lambda b, c: (layer, 0, 0)
    vec = lambda v: v.reshape(-1, 1, LRU_W)
    w_specs = [pl.BlockSpec((None, CONV_W, LRU_W), lay3), pl.BlockSpec((None, 1, LRU_W), lay3),
               pl.BlockSpec((None, LRU_W, LRU_W), lay3), pl.BlockSpec((None, 1, LRU_W), lay3),
               pl.BlockSpec((None, LRU_W, LRU_W), lay3), pl.BlockSpec((None, 1, LRU_W), lay3),
               pl.BlockSpec((None, 1, LRU_W), lay3)]
    w_args = (p['lru_conv_w'], vec(p['lru_conv_b']), p['lru_wa_bd'], vec(p['lru_b_a']), p['lru_wx_bd'],
              vec(p['lru_b_x']), vec(p['lru_lambda']))
    x_spec = pl.BlockSpec((lc, LRU_W), lambda b, c: (b * nc + c, 1))
    g_spec = pl.BlockSpec((lc, LRU_W), lambda b, c: (b * nc + c, 2))
    y_spec = pl.BlockSpec((lc, LRU_W), lambda b, c: (b * nc + c, 0))
    if seq:
        assert pos0 == 0
        y, hl = pl.pallas_call(
            _lru_seq_kernel,
            grid=(g.B, nc),
            in_specs=[x_spec, g_spec] + w_specs,
            out_specs=(y_spec, pl.BlockSpec((None, 1, LRU_W), lambda b, c: (b, 0, 0))),
            out_shape=(sds((g.T, LRU_W), MXU_DTYPE), sds((g.B, 1, LRU_W), F32)),
            scratch_shapes=[pltpu.VMEM((lc + SUBLANE, LRU_W), F32), pltpu.VMEM((lc, LRU_W), F32),
                            pltpu.VMEM((lc, LRU_W), F32), pltpu.VMEM((SUBLANE, LRU_W), F32)],
            compiler_params=_params("arbitrary", "arbitrary"),
            name="rg_lru",
        )(rest, rest, *w_args)
        return y, hl.reshape(g.B, LRU_W)
    assert pos0 > 0
    st_spec = pl.BlockSpec((g.B, LRU_W), lambda b, c: (0, 0))
    return pl.pallas_call(
        _lru_step_kernel,
        grid=(1, 1),
        in_specs=[x_spec, g_spec, pl.BlockSpec((CONV_W - 1, g.B, LRU_W), lambda b, c: (0, 0, 0)), st_spec] + w_specs,
        out_specs=(y_spec, st_spec),
        out_shape=(sds((g.T, LRU_W), MXU_DTYPE), sds((g.B, LRU_W), F32)),
        compiler_params=_params("arbitrary", "arbitrary"),
        name="rg_lru",
    )(rest, rest, jnp.swapaxes(conv_state, 0, 1), h0, *w_args)


def _l2norm(x):
    return x * lax.rsqrt(jnp.sum(x * x, axis=-1, keepdims=True) + NORM_EPS)


def _dn_gates(ab, prm):
    g = -jnp.exp(prm[0:1, :]) * _softplus(ab + prm[1:2, :])
    return g, jax.nn.sigmoid(ab)


def _dn_seq_kernel(q_ref, k_ref, v_ref, z_ref, ab_ref, cw_ref, prm_ref, ng_ref, y_ref, sfin_ref, xbuf, s_s):
    c = pl.program_id(0)
    bsz, cs, _ = q_ref.shape
    hd = DN_HEAD_DIM
    rows = bsz * cs
    pad = SUBLANE

    @pl.when(c == 0)
    def _():
        xbuf[:, pl.ds(0, pad), :] = jnp.zeros((bsz, pad, 3 * DN_W), F32)
        s_s[...] = jnp.zeros_like(s_s)

    for s, ref in enumerate((q_ref, k_ref, v_ref)):
        xbuf[:, pl.ds(pad, cs), pl.ds(s * DN_W, DN_W)] = ref[...]
    conv = None
    for j in range(CONV_W):
        t = cw_ref[pl.ds(j, 1), :] * xbuf[:, pl.ds(pad - (CONV_W - 1) + j, cs), :]
        conv = t if conv is None else conv + t
    new_tail = xbuf[:, pl.ds(cs, pad), :]
    xbuf[:, pl.ds(0, pad), :] = new_tail
    act = _silu(conv).reshape(rows, 3 * DN_W)

    glog, beta_all = _dn_gates(ab_ref[...].reshape(rows, LANE), prm_ref[...])
    row = lax.broadcasted_iota(jnp.int32, (rows, rows), 0)
    col = lax.broadcasted_iota(jnp.int32, (rows, rows), 1)
    shift = cs.bit_length() - 1
    same = lax.shift_right_logical(row, shift) == lax.shift_right_logical(col, shift)
    causal = jnp.logical_and(same, row >= col)
    strict = jnp.logical_and(same, row > col)
    gc = jnp.dot(causal.astype(F32), glog, precision=lax.Precision.HIGHEST, preferred_element_type=F32)
    gct = gc.T
    eye = (row == col).astype(F32)
    ng = ng_ref[...]
    rblk = lax.shift_right_logical(lax.broadcasted_iota(jnp.int32, (rows, bsz * hd), 0), shift)
    cblk = lax.shift_right_logical(lax.broadcasted_iota(jnp.int32, (rows, bsz * hd), 1), hd.bit_length() - 1)
    own = rblk == cblk

    def diag_blocks(m):
        return jnp.concatenate([m[b * cs:(b + 1) * cs, b * hd:(b + 1) * hd] for b in range(bsz)], axis=0)

    for h in range(DN_HEADS):
        q = _l2norm(act[:, h * hd:(h + 1) * hd]) * (hd ** -0.5)
        k = _l2norm(act[:, DN_W + h * hd:DN_W + (h + 1) * hd])
        v = act[:, 2 * DN_W + h * hd:2 * DN_W + (h + 1) * hd]
        beta = beta_all[:, DN_HEADS + h:DN_HEADS + h + 1]
        gcol = gc[:, h:h + 1]
        decay = jnp.exp(jnp.where(causal, gcol - gct[h:h + 1, :], -jnp.inf))
        kb = k * beta
        kq = _mm_nt(jnp.concatenate([kb, q], axis=0), k)
        nmat = jnp.where(strict, -(kq[:rows] * decay), 0.0)
        att = jnp.where(causal, kq[rows:] * decay, 0.0)
        rinv = eye + nmat
        xpow = _mm(nmat, nmat)
        for _ in range(cs.bit_length() - 3):
            st = _mm(jnp.concatenate([xpow, rinv], axis=0), xpow)
            xpow, rinv = st[:rows], rinv + st[rows:]
        rinv = rinv + _mm(rinv, xpow)
        eg = jnp.exp(gcol)
        uw = _mm(rinv, jnp.concatenate([v * beta, kb * eg], axis=1))
        u, w = uw[:, :hd], uw[:, hd:]
        s = s_s[h]
        ws_qs = _mm(jnp.concatenate([w, q * eg], axis=0), s)
        v_new = u - diag_blocks(ws_qs[:rows])
        o = diag_blocks(ws_qs[rows:]) + _mm(att, v_new)
        glast = [gc[(b + 1) * cs - 1:(b + 1) * cs, h:h + 1] for b in range(bsz)]
        glast_rows = jnp.concatenate([jnp.broadcast_to(t, (cs, 1)) for t in glast], axis=0)
        glast_lanes = jnp.concatenate([jnp.broadcast_to(t, (1, hd)) for t in glast], axis=1)
        v_bd = jnp.where(own, jnp.concatenate([v_new] * bsz, axis=1), 0.0)
        s_s[h] = s * jnp.exp(glast_lanes) + _mm_tn(k * jnp.exp(glast_rows - gcol), v_bd)
        o = _rms(o, ng) * _silu(z_ref[:, :, pl.ds(h * hd, hd)].reshape(rows, hd))
        y_ref[:, :, pl.ds(h * hd, hd)] = o.reshape(bsz, cs, hd).astype(y_ref.dtype)

    @pl.when(c == pl.num_programs(0) - 1)
    def _():
        for b in range(bsz):
            for h in range(DN_HEADS):
                sfin_ref[b, h] = s_s[h, :, pl.ds(b * hd, hd)]


def _dn_seq(g, uq, rest, ab, conv_w, prm, norm_g, layer):
    cs = DN_CHUNK
    assert cs & (cs - 1) == 0 and g.L % cs == 0
    nc = g.L // cs
    hd = DN_HEAD_DIM
    lay3 = lambda c: (layer, 0, 0)
    sds = jax.ShapeDtypeStruct
    sec = lambda s: pl.BlockSpec((g.B, cs, DN_W), lambda c: (0, c, s))
    y, sfin = pl.pallas_call(
        _dn_seq_kernel,
        grid=(nc,),
        in_specs=[sec(1), sec(2), sec(3), sec(0),
                  pl.BlockSpec((g.B, cs, LANE), lambda c: (0, c, 0)),
                  pl.BlockSpec((None, CONV_W, 3 * DN_W), lay3),
                  pl.BlockSpec((None, 2, LANE), lay3),
                  pl.BlockSpec((None, 1, hd), lay3)],
        out_specs=(pl.BlockSpec((g.B, cs, DN_W), lambda c: (0, c, 0)),
                   pl.BlockSpec((g.B, DN_HEADS, hd, hd), lambda c: (0, 0, 0, 0))),
        out_shape=(sds((g.B, g.L, DN_W), MXU_DTYPE), sds((g.B, DN_HEADS, hd, hd), F32)),
        scratch_shapes=[pltpu.VMEM((g.B, cs + SUBLANE, 3 * DN_W), F32),
                        pltpu.VMEM((DN_HEADS, hd, g.B * hd), F32)],
        compiler_params=_params("arbitrary"),
        name="deltanet_seq",
    )(*(3 * [uq.reshape(g.B, g.L, -1)]), rest.reshape(g.B, g.L, -1), ab.reshape(g.B, g.L, LANE), conv_w, prm,
      norm_g.reshape(-1, 1, hd))
    return y.reshape(g.T, DN_W), sfin


def _dn_step_kernel(q_ref, k_ref, v_ref, cq_ref, ck_ref, cv_ref, z_ref, ab_ref, cwq_ref, cwk_ref, cwv_ref,
                    prm_ref, ng_ref, s_ref, y_ref, so_ref, o_s, *, bsz):
    h = pl.program_id(0)
    hd = DN_HEAD_DIM

    def conv(x_ref, c_ref, w_ref):
        acc = w_ref[pl.ds(CONV_W - 1, 1), :] * x_ref[...]
        for j in range(CONV_W - 1):
            acc = acc + w_ref[pl.ds(j, 1), :] * c_ref[j]
        return _silu(acc)

    q = _l2norm(conv(q_ref, cq_ref, cwq_ref)) * (hd ** -0.5)
    k = _l2norm(conv(k_ref, ck_ref, cwk_ref))
    v = conv(v_ref, cv_ref, cwv_ref)
    glog, beta_all = _dn_gates(ab_ref[...], prm_ref[...])
    lane = lax.broadcasted_iota(jnp.int32, (bsz, LANE), 1)
    eg = jnp.exp(jnp.sum(jnp.where(lane == h, glog, 0.0), axis=-1, keepdims=True))
    beta = jnp.sum(jnp.where(lane == DN_HEADS + h, beta_all, 0.0), axis=-1, keepdims=True)
    att = jnp.sum(q * k, axis=-1, keepdims=True)
    kt = k.T
    qt = q.T
    for b in range(bsz):
        s = s_ref[b]
        kcol = kt[:, b:b + 1]
        ks = jnp.sum(s * kcol, axis=0, keepdims=True)
        qs = jnp.sum(s * qt[:, b:b + 1], axis=0, keepdims=True)
        eg_b, beta_b = eg[b:b + 1, :], beta[b:b + 1, :]
        v_new = beta_b * v[b:b + 1, :] - (beta_b * eg_b) * ks
        o_s[pl.ds(b, 1), :] = eg_b * qs + att[b:b + 1, :] * v_new
        so_ref[b] = s * eg_b + kcol * v_new
    y_ref[...] = (_rms(o_s[...], ng_ref[...]) * _silu(z_ref[...])).astype(y_ref.dtype)


def _dn_step(g, uq, conv_state, rest, ab, conv_w, prm, norm_g, s0_all, layer):
    bsz = g.B
    hd = DN_HEAD_DIM
    conv_t = jnp.swapaxes(conv_state, 0, 1)
    sec = lambda s: (lambda h: (0, (s + 1) * DN_HEADS + h))
    sec3 = lambda s: (lambda h: (0, 0, s * DN_HEADS + h))
    secw = lambda s: (lambda h: (layer, 0, s * DN_HEADS + h))
    lay3 = lambda h: (layer, 0, 0)
    sds = jax.ShapeDtypeStruct
    return pl.pallas_call(
        functools.partial(_dn_step_kernel, bsz=bsz),
        grid=(DN_HEADS,),
        in_specs=[pl.BlockSpec((bsz, hd), sec(s)) for s in range(3)]
                 + [pl.BlockSpec((CONV_W - 1, bsz, hd), sec3(s)) for s in range(3)]
                 + [pl.BlockSpec((bsz, hd), lambda h: (0, h)),
                    pl.BlockSpec((bsz, LANE), lambda h: (0, 0))]
                 + [pl.BlockSpec((None, CONV_W, hd), secw(s)) for s in range(3)]
                 + [pl.BlockSpec((None, 2, LANE), lay3),
                    pl.BlockSpec((None, 1, hd), lay3),
                    pl.BlockSpec((None, bsz, None, hd, hd), lambda h: (layer, 0, h, 0, 0))],
        out_specs=(pl.BlockSpec((bsz, hd), lambda h: (0, h)),
                   pl.BlockSpec((bsz, None, hd, hd), lambda h: (0, h, 0, 0))),
        out_shape=(sds((bsz, DN_W), MXU_DTYPE), sds((bsz, DN_HEADS, hd, hd), F32)),
        scratch_shapes=[pltpu.VMEM((bsz, hd), F32)],
        compiler_params=_params("arbitrary"),
        name="deltanet_step",
    )(uq, uq, uq, conv_t, conv_t, conv_t, rest, ab, conv_w, conv_w, conv_w, prm,
      norm_g.reshape(-1, 1, hd), s0_all)


def _router_kernel(x_ref, g_ref, sc_ref, sh_ref, wr_ref, br_ref, h_ref, r_ref):
    h = _rms(x_ref[...], g_ref[...]) * (1.0 + sc_ref[...]) + sh_ref[...]
    h_ref[...] = h
    logits = _mm(h, wr_ref[...]) + br_ref[...]
    lane = lax.broadcasted_iota(jnp.int32, logits.shape, 1).astype(F32)
    ninf = -jnp.inf
    big = float(LANE)
    lg = jnp.where(lane < N_GROUPS, logits, ninf)
    gmax = jnp.max(lg, axis=-1, keepdims=True)
    gsel = jnp.min(jnp.where(lg == gmax, lane, big), axis=-1, keepdims=True)
    pg = 1.0 / jnp.sum(jnp.where(lane < N_GROUPS, jnp.exp(logits - gmax), 0.0), axis=-1, keepdims=True)
    lo = N_GROUPS + EXPERTS_PER_GROUP * gsel
    le = jnp.where(jnp.logical_and(lane >= lo, lane < lo + EXPERTS_PER_GROUP), logits, ninf)
    v1 = jnp.max(le, axis=-1, keepdims=True)
    i1 = jnp.min(jnp.where(le == v1, lane, big), axis=-1, keepdims=True)
    le2 = jnp.where(lane == i1, ninf, le)
    v2 = jnp.max(le2, axis=-1, keepdims=True)
    i2 = jnp.min(jnp.where(le2 == v2, lane, big), axis=-1, keepdims=True)
    e2 = jnp.exp(v2 - v1)
    w1 = pg / (1.0 + e2)
    w2 = pg * e2 / (1.0 + e2)
    r_ref[...] = jnp.where(lane == 0, i1 - N_GROUPS,
                           jnp.where(lane == 1, i2 - N_GROUPS,
                                     jnp.where(lane == 2, w1, jnp.where(lane == 3, w2, 0.0))))


def _router(g, x, gain, mod, w_route, b_route, layer):
    gm = g._replace(tm=min(g.tm, 256))
    sds = jax.ShapeDtypeStruct
    return pl.pallas_call(
        _router_kernel,
        grid=(gm.n_tiles, 1),
        in_specs=[_nat_spec(gm, D_MODEL),
                  pl.BlockSpec((1, D_MODEL), lambda i, j: (0, 0)),
                  _mod_spec(gm, 4, D_MODEL),
                  _mod_spec(gm, 3, D_MODEL),
                  pl.BlockSpec((None, D_MODEL, LANE), lambda i, j: (layer, 0, 0)),
                  pl.BlockSpec((None, 1, LANE), lambda i, j: (layer, 0, 0))],
        out_specs=(_nat_spec(gm, D_MODEL), _nat_spec(gm, LANE)),
        out_shape=(sds((g.T, D_MODEL), F32), sds((g.T, LANE), F32)),
        compiler_params=_params("arbitrary", "arbitrary"),
        name="moe_router",
    )(x, gain.reshape(1, D_MODEL), mod, mod, w_route, b_route)


def _row_copy(src_hbm, row, dst, dst_row, sem):
    return pltpu.make_async_copy(src_hbm.at[pl.ds(row, 1)], dst.at[pl.ds(dst_row, 1)], sem)


def _experts_kernel(be_ref, tok_ref, nact_ref, first_ref, nxt_ref, par_ref, h_hbm, wg_hbm, wu_hbm, wd_hbm, y_ref,
                    xbuf, gsem, wst_g, wst_u, wst_d, wsem, wgt, wut, wdt, *, layer):
    b = pl.program_id(0)
    nact = nact_ref[0]
    rb = MOE_ROWS

    def gather(blk, slot):
        for r in range(rb):
            _row_copy(h_hbm, tok_ref[blk * rb + r], xbuf.at[slot], r, gsem.at[slot]).start(priority=1)

    def weight_copies(e, slot):
        return (pltpu.make_async_copy(wg_hbm.at[layer, e], wst_g.at[slot], wsem.at[slot]),
                pltpu.make_async_copy(wu_hbm.at[layer, e], wst_u.at[slot], wsem.at[slot]),
                pltpu.make_async_copy(wd_hbm.at[layer, e], wst_d.at[slot], wsem.at[slot]))

    @pl.when(b == 0)
    def _():
        for cp in weight_copies(be_ref[0], par_ref[0]):
            cp.start()
        gather(0, 0)

    @pl.when(b + 1 < nact)
    def _():
        gather(b + 1, (b + 1) % 2)

    @pl.when(jnp.logical_and(b < nact, first_ref[b] == 1))
    def _():
        slot = par_ref[b]
        for cp in weight_copies(be_ref[b], slot):
            cp.wait()

        @pl.when(nxt_ref[b] >= 0)
        def _():
            for cp in weight_copies(nxt_ref[b], 1 - slot):
                cp.start()

        ck = XPOSE_CHUNK
        for j in range(D_MODEL // ck):
            wgt[:, pl.ds(j * ck, ck)] = wst_g[slot, pl.ds(j * ck, ck), :].T.astype(MXU_DTYPE)
            wut[:, pl.ds(j * ck, ck)] = wst_u[slot, pl.ds(j * ck, ck), :].T.astype(MXU_DTYPE)
            wdt[pl.ds(j * ck, ck), :] = wst_d[slot, :, pl.ds(j * ck, ck)].T.astype(MXU_DTYPE)

    @pl.when(b < nact)
    def _():
        slot = b % 2
        for r in range(rb):
            _row_copy(h_hbm, 0, xbuf.at[slot], r, gsem.at[slot]).wait()
        x = xbuf[slot].astype(MXU_DTYPE)
        hid_t = _silu(_mm_nt(wgt[...], x)) * _mm_nt(wut[...], x)
        y_ref[...] = _mm(wdt[...], hid_t).T

    @pl.when(b >= nact)
    def _():
        y_ref[...] = jnp.zeros_like(y_ref)


def _experts(h_all, plan, w_gate, w_up, w_down, layer):
    block_e, row_token, n_active, first, nxt, par = plan
    n_blocks = block_e.shape[0]
    rb = MOE_ROWS
    any_spec = pl.BlockSpec(memory_space=pl.ANY)
    return pl.pallas_call(
        functools.partial(_experts_kernel, layer=layer),
        grid_spec=pltpu.PrefetchScalarGridSpec(
            num_scalar_prefetch=6,
            grid=(n_blocks,),
            in_specs=[any_spec, any_spec, any_spec, any_spec],
            out_specs=pl.BlockSpec((rb, D_MODEL), lambda b, *_: (b, 0)),
            scratch_shapes=[pltpu.VMEM((2, rb, D_MODEL), F32), pltpu.SemaphoreType.DMA((2,)),
                            pltpu.VMEM((2, D_MODEL, D_EXPERT), F32), pltpu.VMEM((2, D_MODEL, D_EXPERT), F32),
                            pltpu.VMEM((2, D_EXPERT, D_MODEL), F32), pltpu.SemaphoreType.DMA((2,)),
                            pltpu.VMEM((D_EXPERT, D_MODEL), MXU_DTYPE), pltpu.VMEM((D_EXPERT, D_MODEL), MXU_DTYPE),
                            pltpu.VMEM((D_MODEL, D_EXPERT), MXU_DTYPE)]),
        out_shape=jax.ShapeDtypeStruct((n_blocks * rb, D_MODEL), F32),
        compiler_params=_params("arbitrary"),
        name="moe_experts",
    )(block_e, row_token, n_active, first, nxt, par, h_all, w_gate, w_up, w_down)


def _combine_kernel(pos_ref, y_hbm, x_ref, r_ref, g_ref, o_ref, buf, sem, *, tok0, tm):
    i = pl.program_id(0)
    n = pl.num_programs(0)

    def gather(tile, slot):
        for r in range(tm):
            for k in range(TOP_K):
                p = pos_ref[(tok0 + tile * tm + r) * TOP_K + k]
                _row_copy(y_hbm, p, buf.at[slot], k * tm + r, sem.at[slot]).start()

    @pl.when(i == 0)
    def _():
        gather(0, 0)

    @pl.when(i + 1 < n)
    def _():
        gather(i + 1, (i + 1) % 2)

    slot = i % 2
    for r in range(TOP_K * tm):
        _row_copy(y_hbm, 0, buf.at[slot], r, sem.at[slot]).wait()
    route = r_ref[...]
    moe = buf[slot, pl.ds(0, tm), :] * route[:, 2:3] + buf[slot, pl.ds(tm, tm), :] * route[:, 3:4]
    o_ref[...] = x_ref[...] + g_ref[...] * moe


def _combine(g, tok0, pos, y_rows, x, route, mod):
    tm = min(COMBINE_ROWS, g.tm)
    gc = g._replace(tm=tm)
    nt = gc.nt
    if g.L == 1:
        gate_spec = pl.BlockSpec((tm, D_MODEL), lambda i, pos: (i, 5))
    else:
        gate_spec = pl.BlockSpec((None, 1, D_MODEL), lambda i, pos: (i // nt, 0, 5))
    return pl.pallas_call(
        functools.partial(_combine_kernel, tok0=tok0, tm=tm),
        grid_spec=pltpu.PrefetchScalarGridSpec(
            num_scalar_prefetch=1,
            grid=(gc.n_tiles,),
            in_specs=[pl.BlockSpec(memory_space=pl.ANY),
                      pl.BlockSpec((tm, D_MODEL), lambda i, pos: (i, 0)),
                      pl.BlockSpec((tm, LANE), lambda i, pos: (i, 0)),
                      gate_spec],
            out_specs=pl.BlockSpec((tm, D_MODEL), lambda i, pos: (i, 0)),
            scratch_shapes=[pltpu.VMEM((2, TOP_K * tm, D_MODEL), F32), pltpu.SemaphoreType.DMA((2,))]),
        out_shape=jax.ShapeDtypeStruct((g.T, D_MODEL), F32),
        compiler_params=_params("arbitrary"),
        name="moe_combine",
    )(pos, y_rows, x, route, mod)


def _dispatch_plan(route_all):
    rb = MOE_ROWS
    n_tok = route_all.shape[0]
    n_assign = n_tok * TOP_K
    n_blocks = -(-(n_assign + N_EXPERTS * (rb - 1)) // rb)
    flat_e = route_all[:, :TOP_K].astype(jnp.int32).reshape(-1)
    onehot = (flat_e[:, None] == jnp.arange(N_EXPERTS, dtype=jnp.int32)[None, :]).astype(jnp.int32)
    csum = jnp.cumsum(onehot, axis=0)
    counts = csum[-1]
    rank = jnp.sum(onehot * csum, axis=1) - 1
    padded = (counts + rb - 1) // rb * rb
    pad_end = jnp.cumsum(padded)
    pad_start = pad_end - padded
    start = jnp.cumsum(counts) - counts
    pos = pad_start[flat_e] + rank
    n_active = (pad_end[-1] // rb).astype(jnp.int32)
    blk = jnp.minimum(jnp.arange(n_blocks, dtype=jnp.int32), n_active - 1)
    block_e = jnp.minimum(jnp.searchsorted(pad_end, blk * rb, side='right'), N_EXPERTS - 1).astype(jnp.int32)
    order = jnp.argsort(flat_e).astype(jnp.int32)
    row = jnp.arange(n_blocks * rb, dtype=jnp.int32)
    row_e = jnp.repeat(block_e, rb)
    idx = row - pad_start[row_e]
    src = jnp.clip(start[row_e] + idx, 0, n_assign - 1)
    row_token = jnp.where(jnp.logical_and(idx < counts[row_e], row < pad_end[-1]), order[src] // TOP_K, 0)
    bidx = jnp.arange(n_blocks, dtype=jnp.int32)
    first = jnp.logical_and(bidx < n_active, jnp.logical_or(bidx == 0, block_e != jnp.roll(block_e, 1)))
    par = (jnp.cumsum(first.astype(jnp.int32)) - 1) % 2
    nxt_first = lax.cummin(jnp.where(first, bidx, n_blocks)[::-1])[::-1]
    nxt_idx = jnp.concatenate([nxt_first[1:], jnp.full((1,), n_blocks, jnp.int32)])
    nxt = jnp.where(nxt_idx < n_blocks, block_e[jnp.minimum(nxt_idx, n_blocks - 1)], -1)
    plan = (block_e, row_token.astype(jnp.int32), n_active.reshape(1), first.astype(jnp.int32),
            nxt.astype(jnp.int32), par.astype(jnp.int32))
    return plan, pos.astype(jnp.int32)


def _layer(layer, groups, xs, mods, states, p, state_dn):
    hs, new_states = [], []
    for g, x, mod, st in zip(groups, xs, mods, states):
        s5_re, s5_im, dn_conv, lru_h, lru_conv = st
        single = g.L == 1
        h = _norm_mod(g, x, p['norm1_g'][layer], mod, 0, 1)
        uq = _proj(g, h, p['w_a'], layer, 0, BRANCH_W + 3 * DN_W, 512)
        ab = _proj(g, h, p['w_a'], layer, COL_AB, LANE, LANE)
        rest = _proj(g, h, p['w_rest'], layer, 0, REST_GATE, 512)

        ya, n_re, n_im = _s5(g, uq, s5_re, s5_im, p['s5_disc'], p['s5_cr'], p['s5_ci'], p['s5_d'],
                             p['s5_w_glu'], p['s5_b_glu'], layer)
        if single:
            qkv = uq[:, BRANCH_W:]
            lru_x = rest[:, DN_W:DN_W + LRU_W]
            yb, n_dn = _dn_step(g, uq, dn_conv, rest, ab, p['dn_conv_w'], p['dn_prm'], p['dn_norm_g'], state_dn, layer)
            n_dn_conv = jnp.concatenate([dn_conv[:, 1:], qkv[:, None, :]], axis=1)
            n_lru_conv = jnp.concatenate([lru_conv[:, 1:], lru_x[:, None, :]], axis=1)
            yc, n_lru = _lru(g, rest, lru_conv, lru_h, p, layer, pos0=p['past_len'])
        else:
            yb, n_dn = _dn_seq(g, uq, rest, ab, p['dn_conv_w'], p['dn_prm'], p['dn_norm_g'], layer)
            tail = lambda a: a.reshape(g.B, g.L, -1)[:, g.L - (CONV_W - 1):]
            n_dn_conv = tail(uq)[:, :, BRANCH_W:]
            n_lru_conv = tail(rest)[:, :, DN_W:DN_W + LRU_W]
            yc, n_lru = _lru(g, rest, None, None, p, layer, pos0=0)

        merged = _merge(g, h, ya, yb, yc, p['w_rest'], p['w_branch'], layer)
        x1 = _out_proj(g, merged, p['w_out'], layer, x, mod, 2)
        h2, route = _router(g, x1, p['norm2_g'][layer], mod, p['w_route'], p['b_route'], layer)
        hs.append((x1, h2, route))
        new_states.append((n_re.reshape(g.B, S5_GROUPS, S5_STATE), n_im.reshape(g.B, S5_GROUPS, S5_STATE),
                           n_dn, n_dn_conv, n_lru, n_lru_conv))

    h_all = jnp.concatenate([t[1] for t in hs], axis=0)
    route_all = jnp.concatenate([t[2] for t in hs], axis=0)
    plan, pos = _dispatch_plan(route_all)
    y_rows = _experts(h_all, plan, p['w_e_gate'], p['w_e_up'], p['w_e_down'], layer)
    outs, tok0 = [], 0
    for g, (x1, _, route), mod in zip(groups, hs, mods):
        outs.append(_combine(g, tok0, pos, y_rows, x1, route, mod))
        tok0 += g.T
    return outs, new_states


def kernel(x_prompt, x_sample, c_prompt, c_sample, state_s5_re, state_s5_im, state_dn, state_dn_conv, state_lru, state_lru_conv, final_g, norm1_g, norm2_g, w_ada, b_ada, w_in, s5_log_dt, s5_a_re, s5_a_im, s5_b_re, s5_b_im, s5_c_re, s5_c_im, s5_d, s5_w_glu, s5_b_glu, dn_conv_w, dn_a_log, dn_dt_bias, dn_norm_g, lru_conv_w, lru_conv_b, lru_w_a, lru_b_a, lru_w_x, lru_b_x, lru_lambda, w_branch, w_out, w_rg, b_rg, w_re, b_re, w_e_gate, w_e_up, w_e_down):
    depth = w_in.shape[0]
    bp, lp, _ = x_prompt.shape
    bs, ls, _ = x_sample.shape
    assert ls == 1 and bs % SUBLANE == 0 and lp % SEQ_CHUNK == 0
    gp = _Group(bp, lp, min(lp, 1024))
    gs = _Group(bs, 1, bs)
    groups = (gp, gs)

    pad = (-bp) % SUBLANE
    c_all = jnp.concatenate([c_prompt, jnp.zeros((pad, D_MODEL), F32), c_sample], axis=0)
    mod_all = _ada_mod(c_all, w_ada, b_ada)

    prm = jnp.zeros((depth, 2, LANE), F32)
    prm = prm.at[:, 0, :DN_HEADS].set(dn_a_log).at[:, 1, :DN_HEADS].set(dn_dt_bias)
    w_route = jnp.concatenate([w_rg, w_re, jnp.zeros((depth, D_MODEL, LANE - N_GROUPS - N_EXPERTS), F32)], axis=-1)
    b_route = jnp.concatenate([b_rg, b_re, jnp.zeros((depth, LANE - N_GROUPS - N_EXPERTS), F32)], axis=-1)
    p = dict(norm1_g=norm1_g, norm2_g=norm2_g,
             w_a=w_in[:, :, :COL_AB + LANE].astype(MXU_DTYPE), w_rest=w_in[:, :, COL_REST:].astype(MXU_DTYPE),
             s5_disc=_s5_discretize(s5_log_dt, s5_a_re, s5_a_im, s5_b_re, s5_b_im),
             s5_cr=_s5_out_blocks(s5_c_re), s5_ci=_s5_out_blocks(s5_c_im),
             s5_d=s5_d, s5_w_glu=s5_w_glu, s5_b_glu=s5_b_glu,
             dn_conv_w=dn_conv_w, dn_prm=prm, dn_norm_g=dn_norm_g,
             lru_conv_w=lru_conv_w, lru_conv_b=lru_conv_b, lru_wa_bd=_block_diag(lru_w_a), lru_b_a=lru_b_a,
             lru_wx_bd=_block_diag(lru_w_x), lru_b_x=lru_b_x, lru_lambda=lru_lambda,
             w_branch=w_branch, w_out=w_out, w_route=w_route, b_route=b_route.reshape(depth, 1, LANE),
             w_e_gate=w_e_gate, w_e_up=w_e_up, w_e_down=w_e_down, past_len=PAST_LEN)

    xs = [x_prompt.reshape(gp.T, D_MODEL), x_sample.reshape(gs.T, D_MODEL)]
    per_layer = []
    for l in range(depth):
        mods = [mod_all[l, :bp].reshape(bp, 1, 6 * D_MODEL), mod_all[l, bp + pad:]]
        states = [(None, None, None, None, None),
                  (state_s5_re[l].reshape(bs, S5_N), state_s5_im[l].reshape(bs, S5_N), state_dn_conv[l],
                   state_lru[l], state_lru_conv[l])]
        xs, new_states = _layer(l, groups, xs, mods, states, p, state_dn)
        per_layer.append(new_states)

    y_prompt = _final_norm(gp, xs[0], final_g).reshape(bp, lp, D_MODEL)
    y_sample = _final_norm(gs, xs[1], final_g).reshape(bs, 1, D_MODEL)
    stack = lambda gi: tuple(jnp.stack([per_layer[l][gi][k] for l in range(depth)]) for k in range(6))
    return (y_prompt, y_sample) + stack(0) + stack(1)
```

```python
import functools
import math
from typing import NamedTuple

import jax
import jax.numpy as jnp
from jax import lax
from jax.experimental import pallas as pl
from jax.experimental.pallas import tpu as pltpu

F32 = jnp.float32
MXU_DTYPE = jnp.bfloat16

D_MODEL = 2048
BRANCH_W = D_MODEL // 4
S5_GROUP = 16
S5_GROUPS = BRANCH_W // S5_GROUP
S5_STATE = 64
S5_N = S5_GROUPS * S5_STATE
S5_BLOCKS = 4
DN_HEADS = 4
DN_HEAD_DIM = 128
DN_W = DN_HEADS * DN_HEAD_DIM
DN_CHUNK = 64
CONV_W = 4
LRU_W = BRANCH_W
LRU_BLOCKS = 4
LRU_C = 8.0
N_GROUPS = 4
EXPERTS_PER_GROUP = 8
N_EXPERTS = N_GROUPS * EXPERTS_PER_GROUP
TOP_K = 2
D_EXPERT = D_MODEL // 4
NORM_EPS = 1e-6
PAST_LEN = 16384

COL_AB = BRANCH_W + 3 * DN_W
COL_REST = COL_AB + 2 * DN_HEADS
REST_GATE = DN_W + 2 * LRU_W
PACK_REST = 2560

LANE = 128
SUBLANE = 8
VMEM_LIMIT = 56 * 1024 * 1024
SEQ_CHUNK = 256
MOE_ROWS = 256
COMBINE_ROWS = 128
XPOSE_CHUNK = 256


class _Group(NamedTuple):
    B: int
    L: int
    tm: int

    @property
    def T(self):
        return self.B * self.L

    @property
    def nt(self):
        return max(self.L // self.tm, 1)

    @property
    def n_tiles(self):
        return self.T // self.tm


def _params(*sem):
    return pltpu.CompilerParams(dimension_semantics=sem, vmem_limit_bytes=VMEM_LIMIT)


def _mm(a, b):
    return jnp.dot(a.astype(MXU_DTYPE), b.astype(MXU_DTYPE), preferred_element_type=F32)


def _mm_nt(a, b):
    return lax.dot_general(a.astype(MXU_DTYPE), b.astype(MXU_DTYPE), (((1,), (1,)), ((), ())),
                           preferred_element_type=F32)


def _mm_tn(a, b):
    return lax.dot_general(a.astype(MXU_DTYPE), b.astype(MXU_DTYPE), (((0,), (0,)), ((), ())),
                           preferred_element_type=F32)


def _silu(x):
    return x * jax.nn.sigmoid(x)


def _gelu_tanh(x):
    return 0.5 * x * (1.0 + jnp.tanh(math.sqrt(2.0 / math.pi) * (x + 0.044715 * (x * x * x))))


def _softplus(x):
    return jnp.maximum(x, 0.0) + jnp.log1p(jnp.exp(-jnp.abs(x)))


def _rms(x, gain):
    return x * lax.rsqrt(jnp.mean(x * x, axis=-1, keepdims=True) + NORM_EPS) * gain


def _row_iota(shape):
    return lax.broadcasted_iota(jnp.int32, shape, len(shape) - 2)


def _last_row(x):
    return jnp.broadcast_to(x[SUBLANE - 1:SUBLANE], x.shape)


def _nat_spec(g, tn, col0=0):
    return pl.BlockSpec((g.tm, tn), lambda i, j: (i, col0 + j))


def _mod_spec(g, k, tn):
    nb, nt = D_MODEL // tn, g.nt
    if g.L == 1:
        return pl.BlockSpec((g.tm, tn), lambda i, j: (i, k * nb + j))
    return pl.BlockSpec((None, 1, tn), lambda i, j: (i // nt, 0, k * nb + j))


def _ada_kernel(c_ref, w_ref, b_ref, o_ref):
    o_ref[...] = _mm(_silu(c_ref[...]), w_ref[...]) + b_ref[...]


def _ada_mod(c_all, w_ada, b_ada):
    depth, _, n6 = w_ada.shape
    rows = c_all.shape[0]
    tn = 1024
    return pl.pallas_call(
        _ada_kernel,
        grid=(depth, n6 // tn),
        in_specs=[pl.BlockSpec((rows, D_MODEL), lambda l, j: (0, 0)),
                  pl.BlockSpec((None, D_MODEL, tn), lambda l, j: (l, 0, j)),
                  pl.BlockSpec((None, 1, tn), lambda l, j: (l, 0, j))],
        out_specs=pl.BlockSpec((None, rows, tn), lambda l, j: (l, 0, j)),
        out_shape=jax.ShapeDtypeStruct((depth, rows, n6), F32),
        compiler_params=_params("arbitrary", "arbitrary"),
        name="ada_mod",
    )(c_all, w_ada, b_ada.reshape(depth, 1, n6))


def _norm_mod_kernel(x_ref, g_ref, sc_ref, sh_ref, o_ref):
    y = _rms(x_ref[...], g_ref[...])
    o_ref[...] = (y * (1.0 + sc_ref[...]) + sh_ref[...]).astype(o_ref.dtype)


def _norm_mod(g, x, gain, mod, k_shift, k_scale):
    gm = g._replace(tm=min(g.tm, 512))
    return pl.pallas_call(
        _norm_mod_kernel,
        grid=(gm.n_tiles, 1),
        in_specs=[_nat_spec(gm, D_MODEL),
                  pl.BlockSpec((1, D_MODEL), lambda i, j: (0, 0)),
                  _mod_spec(gm, k_scale, D_MODEL),
                  _mod_spec(gm, k_shift, D_MODEL)],
        out_specs=_nat_spec(gm, D_MODEL),
        out_shape=jax.ShapeDtypeStruct((g.T, D_MODEL), MXU_DTYPE),
        compiler_params=_params("arbitrary", "arbitrary"),
        name="norm_mod",
    )(x, gain.reshape(1, D_MODEL), mod, mod)


def _final_norm_kernel(x_ref, g_ref, o_ref):
    o_ref[...] = _rms(x_ref[...], g_ref[...])


def _final_norm(g, x, gain):
    gm = g._replace(tm=min(g.tm, 512))
    return pl.pallas_call(
        _final_norm_kernel,
        grid=(gm.n_tiles, 1),
        in_specs=[_nat_spec(gm, D_MODEL), pl.BlockSpec((1, D_MODEL), lambda i, j: (0, 0))],
        out_specs=_nat_spec(gm, D_MODEL),
        out_shape=jax.ShapeDtypeStruct((g.T, D_MODEL), F32),
        compiler_params=_params("arbitrary", "arbitrary"),
        name="final_norm",
    )(x, gain.reshape(1, D_MODEL))


def _proj_kernel(a_ref, w_ref, o_ref):
    o_ref[...] = _mm(a_ref[...], w_ref[...]).astype(o_ref.dtype)


def _proj(g, a, w, layer, col0, n, tn):
    cb = col0 // tn
    return pl.pallas_call(
        _proj_kernel,
        grid=(g.n_tiles, n // tn),
        in_specs=[pl.BlockSpec((g.tm, D_MODEL), lambda i, j: (i, 0)),
                  pl.BlockSpec((None, D_MODEL, tn), lambda i, j: (layer, 0, cb + j))],
        out_specs=_nat_spec(g, tn),
        out_shape=jax.ShapeDtypeStruct((g.T, n), F32),
        compiler_params=_params("arbitrary", "arbitrary"),
        name="in_proj",
    )(a, w)


def _out_proj_kernel(a_ref, w_ref, x_ref, g_ref, o_ref):
    o_ref[...] = x_ref[...] + g_ref[...] * _mm(a_ref[...], w_ref[...])


def _out_proj(g, merged, w_out, layer, x, mod, k_gate):
    tn = 512
    return pl.pallas_call(
        _out_proj_kernel,
        grid=(g.n_tiles, D_MODEL // tn),
        in_specs=[pl.BlockSpec((g.tm, D_MODEL), lambda i, j: (i, 0)),
                  pl.BlockSpec((None, D_MODEL, tn), lambda i, j: (layer, 0, j)),
                  _nat_spec(g, tn),
                  _mod_spec(g, k_gate, tn)],
        out_specs=_nat_spec(g, tn),
        out_shape=jax.ShapeDtypeStruct((g.T, D_MODEL), F32),
        compiler_params=_params("arbitrary", "arbitrary"),
        name="out_proj",
    )(merged, w_out, x, mod)


def _merge_kernel(h_ref, ya_ref, yb_ref, yc_ref, wg0, wg1, wg2, wb0, wb1, wb2, o_ref):
    h = h_ref[...]
    acc = None
    for y_ref, wg, wb in ((ya_ref, wg0, wb0), (yb_ref, wg1, wb1), (yc_ref, wg2, wb2)):
        t = jax.nn.sigmoid(_mm(h, wg[...])) * _mm(y_ref[...], wb[...])
        acc = t if acc is None else acc + t
    o_ref[...] = acc.astype(o_ref.dtype)


def _merge(g, h, ya, yb, yc, w_pack, w_branch, layer):
    tn = 256
    gate_specs = [pl.BlockSpec((None, D_MODEL, tn),
                               functools.partial(lambda i, j, c: (layer, 0, c + j), c=(PACK_REST + REST_GATE + n * D_MODEL) // tn))
                  for n in range(3)]
    br_specs = [pl.BlockSpec((None, None, BRANCH_W, tn), functools.partial(lambda i, j, n: (layer, n, 0, j), n=n))
                for n in range(3)]
    y_spec = pl.BlockSpec((g.tm, BRANCH_W), lambda i, j: (i, 0))
    return pl.pallas_call(
        _merge_kernel,
        grid=(g.n_tiles, D_MODEL // tn),
        in_specs=[pl.BlockSpec((g.tm, D_MODEL), lambda i, j: (i, 0)), y_spec, y_spec, y_spec] + gate_specs + br_specs,
        out_specs=_nat_spec(g, tn),
        out_shape=jax.ShapeDtypeStruct((g.T, D_MODEL), MXU_DTYPE),
        compiler_params=_params("arbitrary", "arbitrary"),
        name="merge",
    )(h, ya, yb, yc, w_pack, w_pack, w_pack, w_branch, w_branch, w_branch)


def _s5_disc_kernel(ldt_ref, ar_ref, ai_ref, br_ref, bi_ref, lr_ref, li_ref, bbr_ref, bbi_ref):
    dt = jnp.exp(ldt_ref[...])
    ar, ai = ar_ref[...], ai_ref[...]
    mag = jnp.exp(ar * dt)
    lr = mag * jnp.cos(ai * dt)
    li = mag * jnp.sin(ai * dt)
    den = ar * ar + ai * ai
    fr = ((lr - 1.0) * ar + li * ai) / den
    fi = (li * ar - (lr - 1.0) * ai) / den
    lr_ref[...] = lr
    li_ref[...] = li
    br, bi = br_ref[...], bi_ref[...]
    bbr_ref[...] = fr * br - fi * bi
    bbi_ref[...] = fr * bi + fi * br


def _s5_discretize(log_dt, a_re, a_im, b_re, b_im):
    depth = log_dt.shape[0]
    n = depth * S5_GROUPS
    rows = n * S5_GROUP
    rep = lambda v, w: jnp.repeat(v.reshape(n, w), S5_GROUP, axis=0)
    brt = jnp.swapaxes(b_re, -1, -2).reshape(rows, S5_STATE)
    bit = jnp.swapaxes(b_im, -1, -2).reshape(rows, S5_STATE)
    sds = jax.ShapeDtypeStruct
    lr, li, bbr, bbi = pl.pallas_call(
        _s5_disc_kernel,
        out_shape=tuple(sds((rows, S5_STATE), F32) for _ in range(4)),
        name="s5_discretize",
    )(rep(log_dt, 1), rep(a_re, S5_STATE), rep(a_im, S5_STATE), brt, bit)
    gpb = S5_GROUPS // S5_BLOCKS
    eye = jnp.eye(gpb, dtype=F32)

    def blocks_in(m):
        m = m.reshape(depth, S5_BLOCKS, gpb, S5_GROUP, S5_STATE)
        m = jnp.einsum('ljgnp,gh->ljgnhp', m, eye)
        return m.reshape(depth, S5_BLOCKS, gpb * S5_GROUP, gpb * S5_STATE).astype(MXU_DTYPE)

    lam = lambda v: v.reshape(n, S5_GROUP, S5_STATE)[:, 0].reshape(depth, 1, S5_N)
    return lam(lr), lam(li), blocks_in(bbr), blocks_in(bbi)


def _s5_out_blocks(c):
    depth = c.shape[0]
    gpb = S5_GROUPS // S5_BLOCKS
    eye = jnp.eye(gpb, dtype=F32)
    m = jnp.einsum('ljgnp,gh->ljgphn', c.reshape(depth, S5_BLOCKS, gpb, S5_GROUP, S5_STATE), eye)
    return m.reshape(depth, S5_BLOCKS, gpb * S5_STATE, gpb * S5_GROUP).astype(MXU_DTYPE)


def _s5_in(u, bb_ref):
    cw, sw = BRANCH_W // S5_BLOCKS, S5_N // S5_BLOCKS
    return [_mm(u[:, j * cw:(j + 1) * cw], bb_ref[j]) for j in range(S5_BLOCKS)], sw


def _s5_out(hr, hi, u, cr_ref, ci_ref, d_ref, wglu_ref, bglu_ref):
    sw = S5_N // S5_BLOCKS
    y = jnp.concatenate([_mm(hr[:, j * sw:(j + 1) * sw], cr_ref[j]) - _mm(hi[:, j * sw:(j + 1) * sw], ci_ref[j])
                         for j in range(S5_BLOCKS)], axis=1)
    y = _gelu_tanh(y + d_ref[...] * u)
    return y * jax.nn.sigmoid(_mm(y, wglu_ref[...]) + bglu_ref[...])


def _scan_tiles(n_tiles, cw, width, tile_fn, load_state, store_state):
    for c in range(width // cw):
        cs = pl.ds(c * cw, cw)

        def body(k, carry, cs=cs):
            return tile_fn(pl.ds(pl.multiple_of(k * SUBLANE, SUBLANE), SUBLANE), cs, carry)

        store_state(cs, lax.fori_loop(0, n_tiles, body, load_state(cs)))


def _split3(x):
    hi = x.astype(MXU_DTYPE)
    r1 = x - hi.astype(F32)
    mid = r1.astype(MXU_DTYPE)
    return hi, mid, (r1 - mid.astype(F32)).astype(MXU_DTYPE)


def _s5_seq_kernel(u_ref, bbr_ref, bbi_ref, lr_ref, li_ref, cr_ref, ci_ref, d_ref, wglu_ref, bglu_ref,
                   y_ref, hlr_ref, hli_ref, xr_s, xi_s, cr_s, ci_s, pw_s):
    c = pl.program_id(1)
    lc = u_ref.shape[0]
    seg = lc // SUBLANE
    assert seg & (seg - 1) == 0

    @pl.when(c == 0)
    def _():
        cr_s[...] = jnp.zeros_like(cr_s)
        ci_s[...] = jnp.zeros_like(ci_s)
        lr, li = lr_ref[...], li_ref[...]
        for _ in range(seg.bit_length() - 1):
            lr, li = lr * lr - li * li, 2.0 * lr * li
        row = _row_iota((SUBLANE, S5_N))
        pr, pi = lr, li
        acc_r = jnp.zeros((SUBLANE, S5_N), F32)
        acc_i = jnp.zeros((SUBLANE, S5_N), F32)
        for r in range(SUBLANE):
            acc_r = jnp.where(row == r, pr, acc_r)
            acc_i = jnp.where(row == r, pi, acc_i)
            if r + 1 in (1, 2, 4):
                k = (1, 2, 4).index(r + 1)
                pw_s[2 * k] = jnp.where(row >= r + 1, pr, 0.0)
                pw_s[2 * k + 1] = jnp.where(row >= r + 1, pi, 0.0)
            pr, pi = pr * lr - pi * li, pr * li + pi * lr
        pw_s[6] = acc_r
        pw_s[7] = acc_i

    n_i = lax.broadcasted_iota(jnp.int32, (lc, lc), 0)
    t_i = lax.broadcasted_iota(jnp.int32, (lc, lc), 1)
    sh = seg.bit_length() - 1
    perm = (t_i == (n_i & (SUBLANE - 1)) * seg + lax.shift_right_logical(n_i, 3)).astype(MXU_DTYPE)
    unperm = (t_i == (n_i & (seg - 1)) * SUBLANE + lax.shift_right_logical(n_i, sh)).astype(MXU_DTYPE)

    u = sum(_mm(perm, t) for t in _split3(u_ref[...]))
    xr, sw = _s5_in(u, bbr_ref)
    xi, _ = _s5_in(u, bbi_ref)
    for j in range(S5_BLOCKS):
        xr_s[:, pl.ds(j * sw, sw)] = xr[j]
        xi_s[:, pl.ds(j * sw, sw)] = xi[j]

    cw = 512
    for cc in range(S5_N // cw):
        cs = pl.ds(cc * cw, cw)
        lr, li = lr_ref[:, cs], li_ref[:, cs]

        def tile_rows(k):
            return pl.ds(pl.multiple_of(k * SUBLANE, SUBLANE), SUBLANE)

        def pass1(k, carry, cs=cs, lr=lr, li=li):
            gr, gi = carry
            rows = tile_rows(k)
            gr, gi = lr * gr - li * gi + xr_s[rows, cs], lr * gi + li * gr + xi_s[rows, cs]
            xr_s[rows, cs] = gr
            xi_s[rows, cs] = gi
            return gr, gi

        zero = jnp.zeros((SUBLANE, cw), F32)
        sr, si = lax.fori_loop(0, seg, pass1, (zero, zero))

        for k, s in enumerate((1, 2, 4)):
            ar, ai = pw_s[2 * k, :, cs], pw_s[2 * k + 1, :, cs]
            qr, qi = pltpu.roll(sr, s, 0), pltpu.roll(si, s, 0)
            sr, si = sr + ar * qr - ai * qi, si + ar * qi + ai * qr
        ar, ai = pw_s[6, :, cs], pw_s[7, :, cs]
        c0r, c0i = cr_s[:, cs], ci_s[:, cs]
        sr, si = sr + ar * c0r - ai * c0i, si + ar * c0i + ai * c0r
        first = _row_iota((SUBLANE, cw)) == 0
        jr = jnp.where(first, c0r, pltpu.roll(sr, 1, 0))
        ji = jnp.where(first, c0i, pltpu.roll(si, 1, 0))
        cr_s[:, cs] = _last_row(sr)
        ci_s[:, cs] = _last_row(si)

        def pass2(k, carry, cs=cs, lr=lr, li=li):
            jr, ji = carry
            rows = tile_rows(k)
            jr, ji = lr * jr - li * ji, lr * ji + li * jr
            xr_s[rows, cs] = xr_s[rows, cs] + jr
            xi_s[rows, cs] = xi_s[rows, cs] + ji
            return jr, ji

        lax.fori_loop(0, seg, pass2, (jr, ji))

    y = _s5_out(xr_s[...], xi_s[...], u, cr_ref, ci_ref, d_ref, wglu_ref, bglu_ref).astype(y_ref.dtype)
    y_ref[...] = _mm(unperm, y).astype(y_ref.dtype)

    @pl.when(c == pl.num_programs(1) - 1)
    def _():
        hlr_ref[...] = cr_s[pl.ds(0, 1), :]
        hli_ref[...] = ci_s[pl.ds(0, 1), :]


def _s5_step_kernel(u_ref, h0r_ref, h0i_ref, bbr_ref, bbi_ref, lr_ref, li_ref, cr_ref, ci_ref, d_ref, wglu_ref,
                    bglu_ref, y_ref, hlr_ref, hli_ref):
    u = u_ref[...]
    xr, _ = _s5_in(u, bbr_ref)
    xi, _ = _s5_in(u, bbi_ref)
    lr, li, sr, si = lr_ref[...], li_ref[...], h0r_ref[...], h0i_ref[...]
    hr = lr * sr - li * si + jnp.concatenate(xr, axis=1)
    hi = lr * si + li * sr + jnp.concatenate(xi, axis=1)
    hlr_ref[...] = hr
    hli_ref[...] = hi
    y_ref[...] = _s5_out(hr, hi, u, cr_ref, ci_ref, d_ref, wglu_ref, bglu_ref).astype(y_ref.dtype)


def _s5(g, uq, h0r, h0i, disc, cr, ci, d, w_glu, b_glu, layer):
    lam_r, lam_i, bbr, bbi = disc
    cw, sw = BRANCH_W // S5_BLOCKS, S5_N // S5_BLOCKS
    sds = jax.ShapeDtypeStruct
    seq = g.L > 1
    lc = min(g.L, SEQ_CHUNK) if seq else g.B
    nc = g.L // lc if seq else 1
    grid = (g.B, nc) if seq else (1, 1)
    row = (lambda b, c: (b * nc + c, 0))
    lay3 = lambda b, c: (layer, 0, 0)
    lay4 = lambda b, c: (layer, 0, 0, 0)
    w_specs = [pl.BlockSpec((None, S5_BLOCKS, cw, sw), lay4), pl.BlockSpec((None, S5_BLOCKS, cw, sw), lay4),
               pl.BlockSpec((None, 1, S5_N), lay3), pl.BlockSpec((None, 1, S5_N), lay3),
               pl.BlockSpec((None, S5_BLOCKS, sw, cw), lay4), pl.BlockSpec((None, S5_BLOCKS, sw, cw), lay4),
               pl.BlockSpec((None, 1, BRANCH_W), lay3),
               pl.BlockSpec((None, BRANCH_W, BRANCH_W), lay3),
               pl.BlockSpec((None, 1, BRANCH_W), lay3)]
    w_args = (bbr, bbi, lam_r, lam_i, cr, ci, d.reshape(-1, 1, BRANCH_W), w_glu, b_glu.reshape(-1, 1, BRANCH_W))
    if seq:
        st_spec = pl.BlockSpec((None, 1, S5_N), lambda b, c: (b, 0, 0))
        y, hlr, hli = pl.pallas_call(
            _s5_seq_kernel,
            grid=grid,
            in_specs=[pl.BlockSpec((lc, BRANCH_W), row)] + w_specs,
            out_specs=(pl.BlockSpec((lc, BRANCH_W), row), st_spec, st_spec),
            out_shape=(sds((g.T, BRANCH_W), MXU_DTYPE), sds((g.B, 1, S5_N), F32), sds((g.B, 1, S5_N), F32)),
            scratch_shapes=[pltpu.VMEM((lc, S5_N), F32), pltpu.VMEM((lc, S5_N), F32),
                            pltpu.VMEM((SUBLANE, S5_N), F32), pltpu.VMEM((SUBLANE, S5_N), F32),
                            pltpu.VMEM((8, SUBLANE, S5_N), F32)],
            compiler_params=_params("arbitrary", "arbitrary"),
            name="s5_ssm",
        )(uq, *w_args)
        return y, hlr.reshape(g.B, S5_N), hli.reshape(g.B, S5_N)
    st_spec = pl.BlockSpec((g.B, S5_N), lambda b, c: (0, 0))
    return pl.pallas_call(
        _s5_step_kernel,
        grid=grid,
        in_specs=[pl.BlockSpec((lc, BRANCH_W), row), st_spec, st_spec] + w_specs,
        out_specs=(pl.BlockSpec((lc, BRANCH_W), row), st_spec, st_spec),
        out_shape=(sds((g.T, BRANCH_W), MXU_DTYPE), sds((g.B, S5_N), F32), sds((g.B, S5_N), F32)),
        compiler_params=_params("arbitrary", "arbitrary"),
        name="s5_ssm",
    )(uq, h0r, h0i, *w_args)


def _lru_gates(xc, wa_ref, ba_ref, wx_ref, bx_ref, lam_ref):
    r = jax.nn.sigmoid(_mm(xc, wa_ref[...]) + ba_ref[...])
    ig = jax.nn.sigmoid(_mm(xc, wx_ref[...]) + bx_ref[...])
    log_a = -LRU_C * r * _softplus(-lam_ref[...])
    return jnp.exp(log_a), jnp.sqrt(1.0 - jnp.exp(2.0 * log_a)), ig


def _lru_seq_kernel(x_ref, g_ref, cw_ref, cbias_ref, wa_ref, ba_ref, wx_ref, bx_ref, lam_ref,
                    y_ref, hl_ref, xbuf, a_s, b_s, h_s):
    c = pl.program_id(1)
    lc = x_ref.shape[0]
    pad = SUBLANE

    @pl.when(c == 0)
    def _():
        xbuf[pl.ds(0, pad), :] = jnp.zeros((pad, LRU_W), F32)
        h_s[...] = jnp.zeros_like(h_s)

    xbuf[pl.ds(pad, lc), :] = x_ref[...]
    xc = cbias_ref[...]
    for j in range(CONV_W):
        xc = xc + cw_ref[pl.ds(j, 1), :] * xbuf[pl.ds(pad - (CONV_W - 1) + j, lc), :]
    new_tail = xbuf[pl.ds(lc, pad), :]
    xbuf[pl.ds(0, pad), :] = new_tail

    a, mult, ig = _lru_gates(xc, wa_ref, ba_ref, wx_ref, bx_ref, lam_ref)
    first = jnp.logical_and(c == 0, _row_iota((lc, LRU_W)) == 0)
    a_s[...] = jnp.where(first, 0.0, a)
    b_s[...] = jnp.where(first, 1.0, mult) * (ig * xc)

    def tile(rows, cs, h0):
        a, b = a_s[rows, cs], b_s[rows, cs]
        row = _row_iota(a.shape)
        for s in (1, 2, 4):
            keep = row >= s
            b = a * jnp.where(keep, pltpu.roll(b, s, 0), 0.0) + b
            a = a * jnp.where(keep, pltpu.roll(a, s, 0), 1.0)
        h = a * h0 + b
        b_s[rows, cs] = h
        return _last_row(h)

    def store(cs, h):
        h_s[:, cs] = h

    _scan_tiles(lc // SUBLANE, LRU_W, LRU_W, tile, lambda cs: h_s[:, cs], store)
    y_ref[...] = (b_s[...] * _gelu_tanh(g_ref[...])).astype(y_ref.dtype)

    @pl.when(c == pl.num_programs(1) - 1)
    def _():
        hl_ref[...] = h_s[pl.ds(0, 1), :]


def _lru_step_kernel(x_ref, g_ref, cs_ref, h0_ref, cw_ref, cbias_ref, wa_ref, ba_ref, wx_ref, bx_ref, lam_ref,
                     y_ref, hl_ref):
    x = x_ref[...]
    xc = cbias_ref[...] + cw_ref[pl.ds(CONV_W - 1, 1), :] * x
    for j in range(CONV_W - 1):
        xc = xc + cw_ref[pl.ds(j, 1), :] * cs_ref[j]
    a, mult, ig = _lru_gates(xc, wa_ref, ba_ref, wx_ref, bx_ref, lam_ref)
    h = a * h0_ref[...] + mult * (ig * xc)
    hl_ref[...] = h
    y_ref[...] = (h * _gelu_tanh(g_ref[...])).astype(y_ref.dtype)


def _block_diag(w):
    depth, nb, k, _ = w.shape
    eye = jnp.eye(nb, dtype=w.dtype)
    return jnp.einsum('lnwv,nm->lnwmv', w, eye).reshape(depth, nb * k, nb * k).astype(MXU_DTYPE)


def _lru(g, rest, conv_state, h0, p, layer, pos0):
    sds = jax.ShapeDtypeStruct
    seq = g.L > 1
    lc = min(g.L, SEQ_CHUNK) if seq else g.B
    nc = g.L // lc if seq else 1
    lay3 = lambda b, c: (layer, 0, 0)
    vec = lambda v: v.reshape(-1, 1, LRU_W)
    w_specs = [pl.BlockSpec((None, CONV_W, LRU_W), lay3), pl.BlockSpec((None, 1, LRU_W), lay3),
               pl.BlockSpec((None, LRU_W, LRU_W), lay3), pl.BlockSpec((None, 1, LRU_W), lay3),
               pl.BlockSpec((None, LRU_W, LRU_W), lay3), pl.BlockSpec((None, 1, LRU_W), lay3),
               pl.BlockSpec((None, 1, LRU_W), lay3)]
    w_args = (p['lru_conv_w'], vec(p['lru_conv_b']), p['lru_wa_bd'], vec(p['lru_b_a']), p['lru_wx_bd'],
              vec(p['lru_b_x']), vec(p['lru_lambda']))
    x_spec = pl.BlockSpec((lc, LRU_W), lambda b, c: (b * nc + c, 1))
    g_spec = pl.BlockSpec((lc, LRU_W), lambda b, c: (b * nc + c, 2))
    y_spec = pl.BlockSpec((lc, LRU_W), lambda b, c: (b * nc + c, 0))
    if seq:
        assert pos0 == 0
        y, hl = pl.pallas_call(
            _lru_seq_kernel,
            grid=(g.B, nc),
            in_specs=[x_spec, g_spec] + w_specs,
            out_specs=(y_spec, pl.BlockSpec((None, 1, LRU_W), lambda b, c: (b, 0, 0))),
            out_shape=(sds((g.T, LRU_W), MXU_DTYPE), sds((g.B, 1, LRU_W), F32)),
            scratch_shapes=[pltpu.VMEM((lc + SUBLANE, LRU_W), F32), pltpu.VMEM((lc, LRU_W), F32),
                            pltpu.VMEM((lc, LRU_W), F32), pltpu.VMEM((SUBLANE, LRU_W), F32)],
            compiler_params=_params("arbitrary", "arbitrary"),
            name="rg_lru",
        )(rest, rest, *w_args)
        return y, hl.reshape(g.B, LRU_W)
    assert pos0 > 0
    st_spec = pl.BlockSpec((g.B, LRU_W), lambda b, c: (0, 0))
    return pl.pallas_call(
        _lru_step_kernel,
        grid=(1, 1),
        in_specs=[x_spec, g_spec, pl.BlockSpec((CONV_W - 1, g.B, LRU_W), lambda b, c: (0, 0, 0)), st_spec] + w_specs,
        out_specs=(y_spec, st_spec),
        out_shape=(sds((g.T, LRU_W), MXU_DTYPE), sds((g.B, LRU_W), F32)),
        compiler_params=_params("arbitrary", "arbitrary"),
        name="rg_lru",
    )(rest, rest, jnp.swapaxes(conv_state, 0, 1), h0, *w_args)


def _l2norm(x):
    return x * lax.rsqrt(jnp.sum(x * x, axis=-1, keepdims=True) + NORM_EPS)


def _dn_gates(ab, prm):
    g = -jnp.exp(prm[0:1, :]) * _softplus(ab + prm[1:2, :])
    return g, jax.nn.sigmoid(ab)


def _dn_seq_kernel(q_ref, k_ref, v_ref, z_ref, ab_ref, cw_ref, prm_ref, ng_ref, y_ref, sfin_ref, xbuf, s_s):
    c = pl.program_id(0)
    bsz, cs, _ = q_ref.shape
    hd = DN_HEAD_DIM
    rows = bsz * cs
    pad = SUBLANE

    @pl.when(c == 0)
    def _():
        xbuf[:, pl.ds(0, pad), :] = jnp.zeros((bsz, pad, 3 * DN_W), F32)
        s_s[...] = jnp.zeros_like(s_s)

    for s, ref in enumerate((q_ref, k_ref, v_ref)):
        xbuf[:, pl.ds(pad, cs), pl.ds(s * DN_W, DN_W)] = ref[...]
    conv = None
    for j in range(CONV_W):
        t = cw_ref[pl.ds(j, 1), :] * xbuf[:, pl.ds(pad - (CONV_W - 1) + j, cs), :]
        conv = t if conv is None else conv + t
    new_tail = xbuf[:, pl.ds(cs, pad), :]
    xbuf[:, pl.ds(0, pad), :] = new_tail
    act = _silu(conv).reshape(rows, 3 * DN_W)

    glog, beta_all = _dn_gates(ab_ref[...].reshape(rows, LANE), prm_ref[...])
    row = lax.broadcasted_iota(jnp.int32, (rows, rows), 0)
    col = lax.broadcasted_iota(jnp.int32, (rows, rows), 1)
    shift = cs.bit_length() - 1
    same = lax.shift_right_logical(row, shift) == lax.shift_right_logical(col, shift)
    causal = jnp.logical_and(same, row >= col)
    strict = jnp.logical_and(same, row > col)
    gc = jnp.dot(causal.astype(F32), glog, precision=lax.Precision.HIGHEST, preferred_element_type=F32)
    gct = gc.T
    eye = (row == col).astype(F32)
    ng = ng_ref[...]
    rblk = lax.shift_right_logical(lax.broadcasted_iota(jnp.int32, (rows, bsz * hd), 0), shift)
    cblk = lax.shift_right_logical(lax.broadcasted_iota(jnp.int32, (rows, bsz * hd), 1), hd.bit_length() - 1)
    own = rblk == cblk

    def diag_blocks(m):
        return jnp.concatenate([m[b * cs:(b + 1) * cs, b * hd:(b + 1) * hd] for b in range(bsz)], axis=0)

    for h in range(DN_HEADS):
        q = _l2norm(act[:, h * hd:(h + 1) * hd]) * (hd ** -0.5)
        k = _l2norm(act[:, DN_W + h * hd:DN_W + (h + 1) * hd])
        v = act[:, 2 * DN_W + h * hd:2 * DN_W + (h + 1) * hd]
        beta = beta_all[:, DN_HEADS + h:DN_HEADS + h + 1]
        gcol = gc[:, h:h + 1]
        decay = jnp.exp(jnp.where(causal, gcol - gct[h:h + 1, :], -jnp.inf))
        kb = k * beta
        kq = _mm_nt(jnp.concatenate([kb, q], axis=0), k)
        nmat = jnp.where(strict, -(kq[:rows] * decay), 0.0)
        att = jnp.where(causal, kq[rows:] * decay, 0.0)
        rinv = eye + nmat
        xpow = _mm(nmat, nmat)
        for _ in range(cs.bit_length() - 3):
            st = _mm(jnp.concatenate([xpow, rinv], axis=0), xpow)
            xpow, rinv = st[:rows], rinv + st[rows:]
        rinv = rinv + _mm(rinv, xpow)
        eg = jnp.exp(gcol)
        uw = _mm(rinv, jnp.concatenate([v * beta, kb * eg], axis=1))
        u, w = uw[:, :hd], uw[:, hd:]
        s = s_s[h]
        ws_qs = _mm(jnp.concatenate([w, q * eg], axis=0), s)
        v_new = u - diag_blocks(ws_qs[:rows])
        o = diag_blocks(ws_qs[rows:]) + _mm(att, v_new)
        glast = [gc[(b + 1) * cs - 1:(b + 1) * cs, h:h + 1] for b in range(bsz)]
        glast_rows = jnp.concatenate([jnp.broadcast_to(t, (cs, 1)) for t in glast], axis=0)
        glast_lanes = jnp.concatenate([jnp.broadcast_to(t, (1, hd)) for t in glast], axis=1)
        v_bd = jnp.where(own, jnp.concatenate([v_new] * bsz, axis=1), 0.0)
        s_s[h] = s * jnp.exp(glast_lanes) + _mm_tn(k * jnp.exp(glast_rows - gcol), v_bd)
        o = _rms(o, ng) * _silu(z_ref[:, :, pl.ds(h * hd, hd)].reshape(rows, hd))
        y_ref[:, :, pl.ds(h * hd, hd)] = o.reshape(bsz, cs, hd).astype(y_ref.dtype)

    @pl.when(c == pl.num_programs(0) - 1)
    def _():
        for b in range(bsz):
            for h in range(DN_HEADS):
                sfin_ref[b, h] = s_s[h, :, pl.ds(b * hd, hd)]


def _dn_seq(g, uq, rest, ab, conv_w, prm, norm_g, layer):
    cs = DN_CHUNK
    assert cs & (cs - 1) == 0 and g.L % cs == 0
    nc = g.L // cs
    hd = DN_HEAD_DIM
    lay3 = lambda c: (layer, 0, 0)
    sds = jax.ShapeDtypeStruct
    sec = lambda s: pl.BlockSpec((g.B, cs, DN_W), lambda c: (0, c, s))
    y, sfin = pl.pallas_call(
        _dn_seq_kernel,
        grid=(nc,),
        in_specs=[sec(1), sec(2), sec(3), sec(0),
                  pl.BlockSpec((g.B, cs, LANE), lambda c: (0, c, 0)),
                  pl.BlockSpec((None, CONV_W, 3 * DN_W), lay3),
                  pl.BlockSpec((None, 2, LANE), lay3),
                  pl.BlockSpec((None, 1, hd), lay3)],
        out_specs=(pl.BlockSpec((g.B, cs, DN_W), lambda c: (0, c, 0)),
                   pl.BlockSpec((g.B, DN_HEADS, hd, hd), lambda c: (0, 0, 0, 0))),
        out_shape=(sds((g.B, g.L, DN_W), MXU_DTYPE), sds((g.B, DN_HEADS, hd, hd), F32)),
        scratch_shapes=[pltpu.VMEM((g.B, cs + SUBLANE, 3 * DN_W), F32),
                        pltpu.VMEM((DN_HEADS, hd, g.B * hd), F32)],
        compiler_params=_params("arbitrary"),
        name="deltanet_seq",
    )(*(3 * [uq.reshape(g.B, g.L, -1)]), rest.reshape(g.B, g.L, -1), ab.reshape(g.B, g.L, LANE), conv_w, prm,
      norm_g.reshape(-1, 1, hd))
    return y.reshape(g.T, DN_W), sfin


def _dn_step_kernel(q_ref, k_ref, v_ref, cq_ref, ck_ref, cv_ref, z_ref, ab_ref, cwq_ref, cwk_ref, cwv_ref,
                    prm_ref, ng_ref, s_ref, y_ref, so_ref, o_s, *, bsz):
    h = pl.program_id(0)
    hd = DN_HEAD_DIM

    def conv(x_ref, c_ref, w_ref):
        acc = w_ref[pl.ds(CONV_W - 1, 1), :] * x_ref[...]
        for j in range(CONV_W - 1):
            acc = acc + w_ref[pl.ds(j, 1), :] * c_ref[j]
        return _silu(acc)

    q = _l2norm(conv(q_ref, cq_ref, cwq_ref)) * (hd ** -0.5)
    k = _l2norm(conv(k_ref, ck_ref, cwk_ref))
    v = conv(v_ref, cv_ref, cwv_ref)
    glog, beta_all = _dn_gates(ab_ref[...], prm_ref[...])
    lane = lax.broadcasted_iota(jnp.int32, (bsz, LANE), 1)
    eg = jnp.exp(jnp.sum(jnp.where(lane == h, glog, 0.0), axis=-1, keepdims=True))
    beta = jnp.sum(jnp.where(lane == DN_HEADS + h, beta_all, 0.0), axis=-1, keepdims=True)
    att = jnp.sum(q * k, axis=-1, keepdims=True)
    kt = k.T
    qt = q.T
    for b in range(bsz):
        s = s_ref[b]
        kcol = kt[:, b:b + 1]
        ks = jnp.sum(s * kcol, axis=0, keepdims=True)
        qs = jnp.sum(s * qt[:, b:b + 1], axis=0, keepdims=True)
        eg_b, beta_b = eg[b:b + 1, :], beta[b:b + 1, :]
        v_new = beta_b * v[b:b + 1, :] - (beta_b * eg_b) * ks
        o_s[pl.ds(b, 1), :] = eg_b * qs + att[b:b + 1, :] * v_new
        so_ref[b] = s * eg_b + kcol * v_new
    y_ref[...] = (_rms(o_s[...], ng_ref[...]) * _silu(z_ref[...])).astype(y_ref.dtype)


def _dn_step(g, uq, conv_state, rest, ab, conv_w, prm, norm_g, s0_all, layer):
    bsz = g.B
    hd = DN_HEAD_DIM
    conv_t = jnp.swapaxes(conv_state, 0, 1)
    sec = lambda s: (lambda h: (0, (s + 1) * DN_HEADS + h))
    sec3 = lambda s: (lambda h: (0, 0, s * DN_HEADS + h))
    secw = lambda s: (lambda h: (layer, 0, s * DN_HEADS + h))
    lay3 = lambda h: (layer, 0, 0)
    sds = jax.ShapeDtypeStruct
    return pl.pallas_call(
        functools.partial(_dn_step_kernel, bsz=bsz),
        grid=(DN_HEADS,),
        in_specs=[pl.BlockSpec((bsz, hd), sec(s)) for s in range(3)]
                 + [pl.BlockSpec((CONV_W - 1, bsz, hd), sec3(s)) for s in range(3)]
                 + [pl.BlockSpec((bsz, hd), lambda h: (0, h)),
                    pl.BlockSpec((bsz, LANE), lambda h: (0, 0))]
                 + [pl.BlockSpec((None, CONV_W, hd), secw(s)) for s in range(3)]
                 + [pl.BlockSpec((None, 2, LANE), lay3),
                    pl.BlockSpec((None, 1, hd), lay3),
                    pl.BlockSpec((None, bsz, None, hd, hd), lambda h: (layer, 0, h, 0, 0))],
        out_specs=(pl.BlockSpec((bsz, hd), lambda h: (0, h)),
                   pl.BlockSpec((bsz, None, hd, hd), lambda h: (0, h, 0, 0))),
        out_shape=(sds((bsz, DN_W), MXU_DTYPE), sds((bsz, DN_HEADS, hd, hd), F32)),
        scratch_shapes=[pltpu.VMEM((bsz, hd), F32)],
        compiler_params=_params("arbitrary"),
        name="deltanet_step",
    )(uq, uq, uq, conv_t, conv_t, conv_t, rest, ab, conv_w, conv_w, conv_w, prm,
      norm_g.reshape(-1, 1, hd), s0_all)


def _router_kernel(x_ref, g_ref, sc_ref, sh_ref, wr_ref, br_ref, h_ref, r_ref):
    h = _rms(x_ref[...], g_ref[...]) * (1.0 + sc_ref[...]) + sh_ref[...]
    h_ref[...] = h
    logits = _mm(h, wr_ref[...]) + br_ref[...]
    lane = lax.broadcasted_iota(jnp.int32, logits.shape, 1).astype(F32)
    ninf = -jnp.inf
    big = float(LANE)
    lg = jnp.where(lane < N_GROUPS, logits, ninf)
    gmax = jnp.max(lg, axis=-1, keepdims=True)
    gsel = jnp.min(jnp.where(lg == gmax, lane, big), axis=-1, keepdims=True)
    pg = 1.0 / jnp.sum(jnp.where(lane < N_GROUPS, jnp.exp(logits - gmax), 0.0), axis=-1, keepdims=True)
    lo = N_GROUPS + EXPERTS_PER_GROUP * gsel
    le = jnp.where(jnp.logical_and(lane >= lo, lane < lo + EXPERTS_PER_GROUP), logits, ninf)
    v1 = jnp.max(le, axis=-1, keepdims=True)
    i1 = jnp.min(jnp.where(le == v1, lane, big), axis=-1, keepdims=True)
    le2 = jnp.where(lane == i1, ninf, le)
    v2 = jnp.max(le2, axis=-1, keepdims=True)
    i2 = jnp.min(jnp.where(le2 == v2, lane, big), axis=-1, keepdims=True)
    e2 = jnp.exp(v2 - v1)
    w1 = pg / (1.0 + e2)
    w2 = pg * e2 / (1.0 + e2)
    r_ref[...] = jnp.where(lane == 0, i1 - N_GROUPS,
                           jnp.where(lane == 1, i2 - N_GROUPS,
                                     jnp.where(lane == 2, w1, jnp.where(lane == 3, w2, 0.0))))


def _router(g, x, gain, mod, w_route, b_route, layer):
    gm = g._replace(tm=min(g.tm, 256))
    sds = jax.ShapeDtypeStruct
    return pl.pallas_call(
        _router_kernel,
        grid=(gm.n_tiles, 1),
        in_specs=[_nat_spec(gm, D_MODEL),
                  pl.BlockSpec((1, D_MODEL), lambda i, j: (0, 0)),
                  _mod_spec(gm, 4, D_MODEL),
                  _mod_spec(gm, 3, D_MODEL),
                  pl.BlockSpec((None, D_MODEL, LANE), lambda i, j: (layer, 0, 0)),
                  pl.BlockSpec((None, 1, LANE), lambda i, j: (layer, 0, 0))],
        out_specs=(_nat_spec(gm, D_MODEL), _nat_spec(gm, LANE)),
        out_shape=(sds((g.T, D_MODEL), F32), sds((g.T, LANE), F32)),
        compiler_params=_params("arbitrary", "arbitrary"),
        name="moe_router",
    )(x, gain.reshape(1, D_MODEL), mod, mod, w_route, b_route)


def _row_copy(src_hbm, row, dst, dst_row, sem):
    return pltpu.make_async_copy(src_hbm.at[pl.ds(row, 1)], dst.at[pl.ds(dst_row, 1)], sem)


def _experts_kernel(be_ref, tok_ref, nact_ref, first_ref, nxt_ref, par_ref, h_hbm, wg_hbm, wu_hbm, wd_hbm, y_ref,
                    xbuf, gsem, wst_g, wst_u, wst_d, wsem, wgt, wut, wdt, *, layer):
    b = pl.program_id(0)
    nact = nact_ref[0]
    rb = MOE_ROWS

    def gather(blk, slot):
        for r in range(rb):
            _row_copy(h_hbm, tok_ref[blk * rb + r], xbuf.at[slot], r, gsem.at[slot]).start(priority=1)

    def weight_copies(e, slot):
        return (pltpu.make_async_copy(wg_hbm.at[layer, e], wst_g.at[slot], wsem.at[slot]),
                pltpu.make_async_copy(wu_hbm.at[layer, e], wst_u.at[slot], wsem.at[slot]),
                pltpu.make_async_copy(wd_hbm.at[layer, e], wst_d.at[slot], wsem.at[slot]))

    @pl.when(b == 0)
    def _():
        for cp in weight_copies(be_ref[0], par_ref[0]):
            cp.start()
        gather(0, 0)

    @pl.when(b + 1 < nact)
    def _():
        gather(b + 1, (b + 1) % 2)

    @pl.when(jnp.logical_and(b < nact, first_ref[b] == 1))
    def _():
        slot = par_ref[b]
        for cp in weight_copies(be_ref[b], slot):
            cp.wait()

        @pl.when(nxt_ref[b] >= 0)
        def _():
            for cp in weight_copies(nxt_ref[b], 1 - slot):
                cp.start()

        ck = XPOSE_CHUNK
        for j in range(D_MODEL // ck):
            wgt[:, pl.ds(j * ck, ck)] = wst_g[slot, pl.ds(j * ck, ck), :].T.astype(MXU_DTYPE)
            wut[:, pl.ds(j * ck, ck)] = wst_u[slot, pl.ds(j * ck, ck), :].T.astype(MXU_DTYPE)
            wdt[pl.ds(j * ck, ck), :] = wst_d[slot, :, pl.ds(j * ck, ck)].T.astype(MXU_DTYPE)

    @pl.when(b < nact)
    def _():
        slot = b % 2
        for r in range(rb):
            _row_copy(h_hbm, 0, xbuf.at[slot], r, gsem.at[slot]).wait()
        x = xbuf[slot].astype(MXU_DTYPE)
        hid_t = _silu(_mm_nt(wgt[...], x)) * _mm_nt(wut[...], x)
        y_ref[...] = _mm(wdt[...], hid_t).T

    @pl.when(b >= nact)
    def _():
        y_ref[...] = jnp.zeros_like(y_ref)


def _experts(h_all, plan, w_gate, w_up, w_down, layer):
    block_e, row_token, n_active, first, nxt, par = plan
    n_blocks = block_e.shape[0]
    rb = MOE_ROWS
    any_spec = pl.BlockSpec(memory_space=pl.ANY)
    return pl.pallas_call(
        functools.partial(_experts_kernel, layer=layer),
        grid_spec=pltpu.PrefetchScalarGridSpec(
            num_scalar_prefetch=6,
            grid=(n_blocks,),
            in_specs=[any_spec, any_spec, any_spec, any_spec],
            out_specs=pl.BlockSpec((rb, D_MODEL), lambda b, *_: (b, 0)),
            scratch_shapes=[pltpu.VMEM((2, rb, D_MODEL), F32), pltpu.SemaphoreType.DMA((2,)),
                            pltpu.VMEM((2, D_MODEL, D_EXPERT), F32), pltpu.VMEM((2, D_MODEL, D_EXPERT), F32),
                            pltpu.VMEM((2, D_EXPERT, D_MODEL), F32), pltpu.SemaphoreType.DMA((2,)),
                            pltpu.VMEM((D_EXPERT, D_MODEL), MXU_DTYPE), pltpu.VMEM((D_EXPERT, D_MODEL), MXU_DTYPE),
                            pltpu.VMEM((D_MODEL, D_EXPERT), MXU_DTYPE)]),
        out_shape=jax.ShapeDtypeStruct((n_blocks * rb, D_MODEL), F32),
        compiler_params=_params("arbitrary"),
        name="moe_experts",
    )(block_e, row_token, n_active, first, nxt, par, h_all, w_gate, w_up, w_down)


def _combine_kernel(pos_ref, y_hbm, x_ref, r_ref, g_ref, o_ref, buf, sem, *, tok0, tm):
    i = pl.program_id(0)
    n = pl.num_programs(0)

    def gather(tile, slot):
        for r in range(tm):
            for k in range(TOP_K):
                p = pos_ref[(tok0 + tile * tm + r) * TOP_K + k]
                _row_copy(y_hbm, p, buf.at[slot], k * tm + r, sem.at[slot]).start()

    @pl.when(i == 0)
    def _():
        gather(0, 0)

    @pl.when(i + 1 < n)
    def _():
        gather(i + 1, (i + 1) % 2)

    slot = i % 2
    for r in range(TOP_K * tm):
        _row_copy(y_hbm, 0, buf.at[slot], r, sem.at[slot]).wait()
    route = r_ref[...]
    moe = buf[slot, pl.ds(0, tm), :] * route[:, 2:3] + buf[slot, pl.ds(tm, tm), :] * route[:, 3:4]
    o_ref[...] = x_ref[...] + g_ref[...] * moe


def _combine(g, tok0, pos, y_rows, x, route, mod):
    tm = min(COMBINE_ROWS, g.tm)
    gc = g._replace(tm=tm)
    nt = gc.nt
    if g.L == 1:
        gate_spec = pl.BlockSpec((tm, D_MODEL), lambda i, pos: (i, 5))
    else:
        gate_spec = pl.BlockSpec((None, 1, D_MODEL), lambda i, pos: (i // nt, 0, 5))
    return pl.pallas_call(
        functools.partial(_combine_kernel, tok0=tok0, tm=tm),
        grid_spec=pltpu.PrefetchScalarGridSpec(
            num_scalar_prefetch=1,
            grid=(gc.n_tiles,),
            in_specs=[pl.BlockSpec(memory_space=pl.ANY),
                      pl.BlockSpec((tm, D_MODEL), lambda i, pos: (i, 0)),
                      pl.BlockSpec((tm, LANE), lambda i, pos: (i, 0)),
                      gate_spec],
            out_specs=pl.BlockSpec((tm, D_MODEL), lambda i, pos: (i, 0)),
            scratch_shapes=[pltpu.VMEM((2, TOP_K * tm, D_MODEL), F32), pltpu.SemaphoreType.DMA((2,))]),
        out_shape=jax.ShapeDtypeStruct((g.T, D_MODEL), F32),
        compiler_params=_params("arbitrary"),
        name="moe_combine",
    )(pos, y_rows, x, route, mod)


def _dispatch_plan(route_all):
    rb = MOE_ROWS
    n_tok = route_all.shape[0]
    n_assign = n_tok * TOP_K
    n_blocks = -(-(n_assign + N_EXPERTS * (rb - 1)) // rb)
    flat_e = route_all[:, :TOP_K].astype(jnp.int32).reshape(-1)
    order = jnp.argsort(flat_e).astype(jnp.int32)
    bounds = jnp.searchsorted(flat_e[order], jnp.arange(N_EXPERTS + 1, dtype=jnp.int32)).astype(jnp.int32)
    start, counts = bounds[:-1], bounds[1:] - bounds[:-1]
    rank = jnp.argsort(order).astype(jnp.int32) - start[flat_e]
    padded = (counts + rb - 1) // rb * rb
    pad_end = jnp.cumsum(padded)
    pad_start = pad_end - padded
    pos = pad_start[flat_e] + rank
    n_active = (pad_end[-1] // rb).astype(jnp.int32)
    blk = jnp.minimum(jnp.arange(n_blocks, dtype=jnp.int32), n_active - 1)
    block_e = jnp.minimum(jnp.searchsorted(pad_end, blk * rb, side='right'), N_EXPERTS - 1).astype(jnp.int32)
    row = jnp.arange(n_blocks * rb, dtype=jnp.int32)
    row_e = jnp.repeat(block_e, rb)
    idx = row - pad_start[row_e]
    src = jnp.clip(start[row_e] + idx, 0, n_assign - 1)
    row_token = jnp.where(jnp.logical_and(idx < counts[row_e], row < pad_end[-1]), order[src] // TOP_K, 0)
    bidx = jnp.arange(n_blocks, dtype=jnp.int32)
    first = jnp.logical_and(bidx < n_active, jnp.logical_or(bidx == 0, block_e != jnp.roll(block_e, 1)))
    par = (jnp.cumsum(first.astype(jnp.int32)) - 1) % 2
    nxt_first = lax.cummin(jnp.where(first, bidx, n_blocks)[::-1])[::-1]
    nxt_idx = jnp.concatenate([nxt_first[1:], jnp.full((1,), n_blocks, jnp.int32)])
    nxt = jnp.where(nxt_idx < n_blocks, block_e[jnp.minimum(nxt_idx, n_blocks - 1)], -1)
    plan = (block_e, row_token.astype(jnp.int32), n_active.reshape(1), first.astype(jnp.int32),
            nxt.astype(jnp.int32), par.astype(jnp.int32))
    return plan, pos.astype(jnp.int32)


def _layer(layer, groups, xs, mods, states, p, state_dn):
    hs, new_states = [], []
    for g, x, mod, st in zip(groups, xs, mods, states):
        s5_re, s5_im, dn_conv, lru_h, lru_conv = st
        single = g.L == 1
        h = _norm_mod(g, x, p['norm1_g'][layer], mod, 0, 1)
        uq = _proj(g, h, p['w_pack'], layer, 0, BRANCH_W + 3 * DN_W, 1024)
        ab = _proj(g, h, p['w_pack'], layer, COL_AB, LANE, LANE)
        rest = _proj(g, h, p['w_pack'], layer, PACK_REST, REST_GATE, 512)

        ya, n_re, n_im = _s5(g, uq, s5_re, s5_im, p['s5_disc'], p['s5_cr'], p['s5_ci'], p['s5_d'],
                             p['s5_w_glu'], p['s5_b_glu'], layer)
        if single:
            qkv = uq[:, BRANCH_W:]
            lru_x = rest[:, DN_W:DN_W + LRU_W]
            yb, n_dn = _dn_step(g, uq, dn_conv, rest, ab, p['dn_conv_w'], p['dn_prm'], p['dn_norm_g'], state_dn, layer)
            n_dn_conv = jnp.concatenate([dn_conv[:, 1:], qkv[:, None, :]], axis=1)
            n_lru_conv = jnp.concatenate([lru_conv[:, 1:], lru_x[:, None, :]], axis=1)
            yc, n_lru = _lru(g, rest, lru_conv, lru_h, p, layer, pos0=p['past_len'])
        else:
            yb, n_dn = _dn_seq(g, uq, rest, ab, p['dn_conv_w'], p['dn_prm'], p['dn_norm_g'], layer)
            tail = lambda a: a.reshape(g.B, g.L, -1)[:, g.L - (CONV_W - 1):]
            n_dn_conv = tail(uq)[:, :, BRANCH_W:]
            n_lru_conv = tail(rest)[:, :, DN_W:DN_W + LRU_W]
            yc, n_lru = _lru(g, rest, None, None, p, layer, pos0=0)

        merged = _merge(g, h, ya, yb, yc, p['w_pack'], p['w_branch'], layer)
        x1 = _out_proj(g, merged, p['w_out'], layer, x, mod, 2)
        h2, route = _router(g, x1, p['norm2_g'][layer], mod, p['w_route'], p['b_route'], layer)
        hs.append((x1, h2, route))
        new_states.append((n_re.reshape(g.B, S5_GROUPS, S5_STATE), n_im.reshape(g.B, S5_GROUPS, S5_STATE),
                           n_dn, n_dn_conv, n_lru, n_lru_conv))

    h_all = jnp.concatenate([t[1] for t in hs], axis=0)
    route_all = jnp.concatenate([t[2] for t in hs], axis=0)
    plan, pos = _dispatch_plan(route_all)
    y_rows = _experts(h_all, plan, p['w_e_gate'], p['w_e_up'], p['w_e_down'], layer)
    outs, tok0 = [], 0
    for g, (x1, _, route), mod in zip(groups, hs, mods):
        outs.append(_combine(g, tok0, pos, y_rows, x1, route, mod))
        tok0 += g.T
    return outs, new_states


def kernel(x_prompt, x_sample, c_prompt, c_sample, state_s5_re, state_s5_im, state_dn, state_dn_conv, state_lru, state_lru_conv, final_g, norm1_g, norm2_g, w_ada, b_ada, w_in, s5_log_dt, s5_a_re, s5_a_im, s5_b_re, s5_b_im, s5_c_re, s5_c_im, s5_d, s5_w_glu, s5_b_glu, dn_conv_w, dn_a_log, dn_dt_bias, dn_norm_g, lru_conv_w, lru_conv_b, lru_w_a, lru_b_a, lru_w_x, lru_b_x, lru_lambda, w_branch, w_out, w_rg, b_rg, w_re, b_re, w_e_gate, w_e_up, w_e_down):
    depth = w_in.shape[0]
    bp, lp, _ = x_prompt.shape
    bs, ls, _ = x_sample.shape
    assert ls == 1 and bs % SUBLANE == 0 and lp % SEQ_CHUNK == 0
    gp = _Group(bp, lp, min(lp, 1024))
    gs = _Group(bs, 1, bs)
    groups = (gp, gs)

    pad = (-bp) % SUBLANE
    c_all = jnp.concatenate([c_prompt, jnp.zeros((pad, D_MODEL), F32), c_sample], axis=0)
    mod_all = _ada_mod(c_all, w_ada, b_ada)

    prm = jnp.zeros((depth, 2, LANE), F32)
    prm = prm.at[:, 0, :DN_HEADS].set(dn_a_log).at[:, 1, :DN_HEADS].set(dn_dt_bias)
    w_route = jnp.concatenate([w_rg, w_re, jnp.zeros((depth, D_MODEL, LANE - N_GROUPS - N_EXPERTS), F32)], axis=-1)
    b_route = jnp.concatenate([b_rg, b_re, jnp.zeros((depth, LANE - N_GROUPS - N_EXPERTS), F32)], axis=-1)
    p = dict(norm1_g=norm1_g, norm2_g=norm2_g,
             w_pack=jnp.concatenate([w_in[:, :, :COL_AB + LANE],
                                     jnp.zeros((depth, D_MODEL, PACK_REST - COL_AB - LANE), F32),
                                     w_in[:, :, COL_REST:]], axis=-1).astype(MXU_DTYPE),
             s5_disc=_s5_discretize(s5_log_dt, s5_a_re, s5_a_im, s5_b_re, s5_b_im),
             s5_cr=_s5_out_blocks(s5_c_re), s5_ci=_s5_out_blocks(s5_c_im),
             s5_d=s5_d, s5_w_glu=s5_w_glu, s5_b_glu=s5_b_glu,
             dn_conv_w=dn_conv_w, dn_prm=prm, dn_norm_g=dn_norm_g,
             lru_conv_w=lru_conv_w, lru_conv_b=lru_conv_b, lru_wa_bd=_block_diag(lru_w_a), lru_b_a=lru_b_a,
             lru_wx_bd=_block_diag(lru_w_x), lru_b_x=lru_b_x, lru_lambda=lru_lambda,
             w_branch=w_branch, w_out=w_out, w_route=w_route, b_route=b_route.reshape(depth, 1, LANE),
             w_e_gate=w_e_gate, w_e_up=w_e_up, w_e_down=w_e_down, past_len=PAST_LEN)

    xs = [x_prompt.reshape(gp.T, D_MODEL), x_sample.reshape(gs.T, D_MODEL)]
    per_layer = []
    for l in range(depth):
        mods = [mod_all[l, :bp].reshape(bp, 1, 6 * D_MODEL), mod_all[l, bp + pad:]]
        states = [(None, None, None, None, None),
                  (state_s5_re[l].reshape(bs, S5_N), state_s5_im[l].reshape(bs, S5_N), state_dn_conv[l],
                   state_lru[l], state_lru_conv[l])]
        xs, new_states = _layer(l, groups, xs, mods, states, p, state_dn)
        per_layer.append(new_states)

    y_prompt = _final_norm(gp, xs[0], final_g).reshape(bp, lp, D_MODEL)
    y_sample = _final_norm(gs, xs[1], final_g).reshape(bs, 1, D_MODEL)
    stack = lambda gi: tuple(jnp.stack([per_layer[l][gi][k] for l in range(depth)]) for k in range(6))
    return (y_prompt, y_sample) + stack(0) + stack(1)
```

```python
import functools
import math
from typing import NamedTuple

import jax
import jax.numpy as jnp
from jax import lax
from jax.experimental import pallas as pl
from jax.experimental.pallas import tpu as pltpu

F32 = jnp.float32
MXU_DTYPE = jnp.bfloat16

D_MODEL = 2048
BRANCH_W = D_MODEL // 4
S5_GROUP = 16
S5_GROUPS = BRANCH_W // S5_GROUP
S5_STATE = 64
S5_N = S5_GROUPS * S5_STATE
S5_BLOCKS = 4
DN_HEADS = 4
DN_HEAD_DIM = 128
DN_W = DN_HEADS * DN_HEAD_DIM
DN_CHUNK = 64
CONV_W = 4
LRU_W = BRANCH_W
LRU_BLOCKS = 4
LRU_C = 8.0
N_GROUPS = 4
EXPERTS_PER_GROUP = 8
N_EXPERTS = N_GROUPS * EXPERTS_PER_GROUP
TOP_K = 2
D_EXPERT = D_MODEL // 4
NORM_EPS = 1e-6
PAST_LEN = 16384

COL_AB = BRANCH_W + 3 * DN_W
COL_REST = COL_AB + 2 * DN_HEADS
REST_GATE = DN_W + 2 * LRU_W
PACK_REST = 2560

LANE = 128
SUBLANE = 8
VMEM_LIMIT = 56 * 1024 * 1024
SEQ_CHUNK = 256
MOE_ROWS = 256
COMBINE_ROWS = 128
XPOSE_CHUNK = 256
ROUTER_ROWS = 128


class _Group(NamedTuple):
    B: int
    L: int
    tm: int

    @property
    def T(self):
        return self.B * self.L

    @property
    def nt(self):
        return max(self.L // self.tm, 1)

    @property
    def n_tiles(self):
        return self.T // self.tm


def _params(*sem):
    return pltpu.CompilerParams(dimension_semantics=sem, vmem_limit_bytes=VMEM_LIMIT)


def _mm(a, b):
    return jnp.dot(a.astype(MXU_DTYPE), b.astype(MXU_DTYPE), preferred_element_type=F32)


def _mm_nt(a, b):
    return lax.dot_general(a.astype(MXU_DTYPE), b.astype(MXU_DTYPE), (((1,), (1,)), ((), ())),
                           preferred_element_type=F32)


def _mm_tn(a, b):
    return lax.dot_general(a.astype(MXU_DTYPE), b.astype(MXU_DTYPE), (((0,), (0,)), ((), ())),
                           preferred_element_type=F32)


def _silu(x):
    return x * jax.nn.sigmoid(x)


def _gelu_tanh(x):
    return 0.5 * x * (1.0 + jnp.tanh(math.sqrt(2.0 / math.pi) * (x + 0.044715 * (x * x * x))))


def _softplus(x):
    return jnp.maximum(x, 0.0) + jnp.log1p(jnp.exp(-jnp.abs(x)))


def _rms(x, gain):
    return x * lax.rsqrt(jnp.mean(x * x, axis=-1, keepdims=True) + NORM_EPS) * gain


def _row_iota(shape):
    return lax.broadcasted_iota(jnp.int32, shape, len(shape) - 2)


def _last_row(x):
    return jnp.broadcast_to(x[SUBLANE - 1:SUBLANE], x.shape)


def _nat_spec(g, tn, col0=0):
    return pl.BlockSpec((g.tm, tn), lambda i, j: (i, col0 + j))


def _mod_spec(g, k, tn):
    nb, nt = D_MODEL // tn, g.nt
    if g.L == 1:
        return pl.BlockSpec((g.tm, tn), lambda i, j: (i, k * nb + j))
    return pl.BlockSpec((None, 1, tn), lambda i, j: (i // nt, 0, k * nb + j))


def _ada_kernel(c_ref, w_ref, b_ref, o_ref):
    o_ref[...] = _mm(_silu(c_ref[...]), w_ref[...]) + b_ref[...]


def _ada_mod(c_all, w_ada, b_ada):
    depth, _, n6 = w_ada.shape
    rows = c_all.shape[0]
    tn = 1024
    return pl.pallas_call(
        _ada_kernel,
        grid=(depth, n6 // tn),
        in_specs=[pl.BlockSpec((rows, D_MODEL), lambda l, j: (0, 0)),
                  pl.BlockSpec((None, D_MODEL, tn), lambda l, j: (l, 0, j)),
                  pl.BlockSpec((None, 1, tn), lambda l, j: (l, 0, j))],
        out_specs=pl.BlockSpec((None, rows, tn), lambda l, j: (l, 0, j)),
        out_shape=jax.ShapeDtypeStruct((depth, rows, n6), F32),
        compiler_params=_params("arbitrary", "arbitrary"),
        name="ada_mod",
    )(c_all, w_ada, b_ada.reshape(depth, 1, n6))


def _norm_mod_kernel(x_ref, g_ref, sc_ref, sh_ref, o_ref):
    y = _rms(x_ref[...], g_ref[...])
    o_ref[...] = (y * (1.0 + sc_ref[...]) + sh_ref[...]).astype(o_ref.dtype)


def _norm_mod(g, x, gain, mod, k_shift, k_scale):
    gm = g._replace(tm=min(g.tm, 512))
    return pl.pallas_call(
        _norm_mod_kernel,
        grid=(gm.n_tiles, 1),
        in_specs=[_nat_spec(gm, D_MODEL),
                  pl.BlockSpec((1, D_MODEL), lambda i, j: (0, 0)),
                  _mod_spec(gm, k_scale, D_MODEL),
                  _mod_spec(gm, k_shift, D_MODEL)],
        out_specs=_nat_spec(gm, D_MODEL),
        out_shape=jax.ShapeDtypeStruct((g.T, D_MODEL), MXU_DTYPE),
        compiler_params=_params("arbitrary", "arbitrary"),
        name="norm_mod",
    )(x, gain.reshape(1, D_MODEL), mod, mod)


def _final_norm_kernel(x_ref, g_ref, o_ref):
    o_ref[...] = _rms(x_ref[...], g_ref[...])


def _final_norm(g, x, gain):
    gm = g._replace(tm=min(g.tm, 512))
    return pl.pallas_call(
        _final_norm_kernel,
        grid=(gm.n_tiles, 1),
        in_specs=[_nat_spec(gm, D_MODEL), pl.BlockSpec((1, D_MODEL), lambda i, j: (0, 0))],
        out_specs=_nat_spec(gm, D_MODEL),
        out_shape=jax.ShapeDtypeStruct((g.T, D_MODEL), F32),
        compiler_params=_params("arbitrary", "arbitrary"),
        name="final_norm",
    )(x, gain.reshape(1, D_MODEL))


def _proj_kernel(a_ref, w_ref, o_ref):
    o_ref[...] = _mm(a_ref[...], w_ref[...]).astype(o_ref.dtype)


def _proj(g, a, w, layer, col0, n, tn):
    cb = col0 // tn
    return pl.pallas_call(
        _proj_kernel,
        grid=(g.n_tiles, n // tn),
        in_specs=[pl.BlockSpec((g.tm, D_MODEL), lambda i, j: (i, 0)),
                  pl.BlockSpec((None, D_MODEL, tn), lambda i, j: (layer, 0, cb + j))],
        out_specs=_nat_spec(g, tn),
        out_shape=jax.ShapeDtypeStruct((g.T, n), F32),
        compiler_params=_params("arbitrary", "arbitrary"),
        name="in_proj",
    )(a, w)


def _out_proj_kernel(a_ref, w_ref, x_ref, g_ref, o_ref):
    o_ref[...] = x_ref[...] + g_ref[...] * _mm(a_ref[...], w_ref[...])


def _out_proj(g, merged, w_out, layer, x, mod, k_gate):
    tn = 512
    return pl.pallas_call(
        _out_proj_kernel,
        grid=(g.n_tiles, D_MODEL // tn),
        in_specs=[pl.BlockSpec((g.tm, D_MODEL), lambda i, j: (i, 0)),
                  pl.BlockSpec((None, D_MODEL, tn), lambda i, j: (layer, 0, j)),
                  _nat_spec(g, tn),
                  _mod_spec(g, k_gate, tn)],
        out_specs=_nat_spec(g, tn),
        out_shape=jax.ShapeDtypeStruct((g.T, D_MODEL), F32),
        compiler_params=_params("arbitrary", "arbitrary"),
        name="out_proj",
    )(merged, w_out, x, mod)


def _merge_kernel(h_ref, ya_ref, yb_ref, yc_ref, wg0, wg1, wg2, wb0, wb1, wb2, o_ref):
    h = h_ref[...]
    acc = None
    for y_ref, wg, wb in ((ya_ref, wg0, wb0), (yb_ref, wg1, wb1), (yc_ref, wg2, wb2)):
        t = jax.nn.sigmoid(_mm(h, wg[...])) * _mm(y_ref[...], wb[...])
        acc = t if acc is None else acc + t
    o_ref[...] = acc.astype(o_ref.dtype)


def _merge(g, h, ya, yb, yc, w_pack, w_branch, layer):
    tn = 256
    gate_specs = [pl.BlockSpec((None, D_MODEL, tn),
                               functools.partial(lambda i, j, c: (layer, 0, c + j), c=(PACK_REST + REST_GATE + n * D_MODEL) // tn))
                  for n in range(3)]
    br_specs = [pl.BlockSpec((None, None, BRANCH_W, tn), functools.partial(lambda i, j, n: (layer, n, 0, j), n=n))
                for n in range(3)]
    y_spec = pl.BlockSpec((g.tm, BRANCH_W), lambda i, j: (i, 0))
    return pl.pallas_call(
        _merge_kernel,
        grid=(g.n_tiles, D_MODEL // tn),
        in_specs=[pl.BlockSpec((g.tm, D_MODEL), lambda i, j: (i, 0)), y_spec, y_spec, y_spec] + gate_specs + br_specs,
        out_specs=_nat_spec(g, tn),
        out_shape=jax.ShapeDtypeStruct((g.T, D_MODEL), MXU_DTYPE),
        compiler_params=_params("arbitrary", "arbitrary"),
        name="merge",
    )(h, ya, yb, yc, w_pack, w_pack, w_pack, w_branch, w_branch, w_branch)


def _s5_disc_kernel(ldt_ref, ar_ref, ai_ref, br_ref, bi_ref, lr_ref, li_ref, bbr_ref, bbi_ref):
    dt = jnp.exp(ldt_ref[...])
    ar, ai = ar_ref[...], ai_ref[...]
    mag = jnp.exp(ar * dt)
    lr = mag * jnp.cos(ai * dt)
    li = mag * jnp.sin(ai * dt)
    den = ar * ar + ai * ai
    fr = ((lr - 1.0) * ar + li * ai) / den
    fi = (li * ar - (lr - 1.0) * ai) / den
    lr_ref[...] = lr
    li_ref[...] = li
    br, bi = br_ref[...], bi_ref[...]
    bbr_ref[...] = fr * br - fi * bi
    bbi_ref[...] = fr * bi + fi * br


def _s5_discretize(log_dt, a_re, a_im, b_re, b_im):
    depth = log_dt.shape[0]
    n = depth * S5_GROUPS
    rows = n * S5_GROUP
    rep = lambda v, w: jnp.repeat(v.reshape(n, w), S5_GROUP, axis=0)
    brt = jnp.swapaxes(b_re, -1, -2).reshape(rows, S5_STATE)
    bit = jnp.swapaxes(b_im, -1, -2).reshape(rows, S5_STATE)
    sds = jax.ShapeDtypeStruct
    lr, li, bbr, bbi = pl.pallas_call(
        _s5_disc_kernel,
        out_shape=tuple(sds((rows, S5_STATE), F32) for _ in range(4)),
        name="s5_discretize",
    )(rep(log_dt, 1), rep(a_re, S5_STATE), rep(a_im, S5_STATE), brt, bit)
    gpb = S5_GROUPS // S5_BLOCKS
    eye = jnp.eye(gpb, dtype=F32)

    def blocks_in(m):
        m = m.reshape(depth, S5_BLOCKS, gpb, S5_GROUP, S5_STATE)
        m = jnp.einsum('ljgnp,gh->ljgnhp', m, eye)
        return m.reshape(depth, S5_BLOCKS, gpb * S5_GROUP, gpb * S5_STATE).astype(MXU_DTYPE)

    lam = lambda v: v.reshape(n, S5_GROUP, S5_STATE)[:, 0].reshape(depth, 1, S5_N)
    return lam(lr), lam(li), blocks_in(bbr), blocks_in(bbi)


def _s5_out_blocks(c):
    depth = c.shape[0]
    gpb = S5_GROUPS // S5_BLOCKS
    eye = jnp.eye(gpb, dtype=F32)
    m = jnp.einsum('ljgnp,gh->ljgphn', c.reshape(depth, S5_BLOCKS, gpb, S5_GROUP, S5_STATE), eye)
    return m.reshape(depth, S5_BLOCKS, gpb * S5_STATE, gpb * S5_GROUP).astype(MXU_DTYPE)


def _s5_in(u, bb_ref):
    cw, sw = BRANCH_W // S5_BLOCKS, S5_N // S5_BLOCKS
    return [_mm(u[:, j * cw:(j + 1) * cw], bb_ref[j]) for j in range(S5_BLOCKS)], sw


def _s5_out(hr, hi, u, cr_ref, ci_ref, d_ref, wglu_ref, bglu_ref):
    sw = S5_N // S5_BLOCKS
    y = jnp.concatenate([_mm(hr[:, j * sw:(j + 1) * sw], cr_ref[j]) - _mm(hi[:, j * sw:(j + 1) * sw], ci_ref[j])
                         for j in range(S5_BLOCKS)], axis=1)
    y = _gelu_tanh(y + d_ref[...] * u)
    return y * jax.nn.sigmoid(_mm(y, wglu_ref[...]) + bglu_ref[...])


def _scan_tiles(n_tiles, cw, width, tile_fn, load_state, store_state):
    for c in range(width // cw):
        cs = pl.ds(c * cw, cw)

        def body(k, carry, cs=cs):
            return tile_fn(pl.ds(pl.multiple_of(k * SUBLANE, SUBLANE), SUBLANE), cs, carry)

        store_state(cs, lax.fori_loop(0, n_tiles, body, load_state(cs)))


def _split3(x):
    hi = x.astype(MXU_DTYPE)
    r1 = x - hi.astype(F32)
    mid = r1.astype(MXU_DTYPE)
    return hi, mid, (r1 - mid.astype(F32)).astype(MXU_DTYPE)


def _s5_seq_kernel(u_ref, bbr_ref, bbi_ref, lr_ref, li_ref, cr_ref, ci_ref, d_ref, wglu_ref, bglu_ref,
                   y_ref, hlr_ref, hli_ref, xr_s, xi_s, cr_s, ci_s, pw_s):
    c = pl.program_id(1)
    lc = u_ref.shape[0]
    seg = lc // SUBLANE
    assert seg & (seg - 1) == 0

    @pl.when(c == 0)
    def _():
        cr_s[...] = jnp.zeros_like(cr_s)
        ci_s[...] = jnp.zeros_like(ci_s)
        lr, li = lr_ref[...], li_ref[...]
        for _ in range(seg.bit_length() - 1):
            lr, li = lr * lr - li * li, 2.0 * lr * li
        row = _row_iota((SUBLANE, S5_N))
        pr, pi = lr, li
        acc_r = jnp.zeros((SUBLANE, S5_N), F32)
        acc_i = jnp.zeros((SUBLANE, S5_N), F32)
        for r in range(SUBLANE):
            acc_r = jnp.where(row == r, pr, acc_r)
            acc_i = jnp.where(row == r, pi, acc_i)
            if r + 1 in (1, 2, 4):
                k = (1, 2, 4).index(r + 1)
                pw_s[2 * k] = jnp.where(row >= r + 1, pr, 0.0)
                pw_s[2 * k + 1] = jnp.where(row >= r + 1, pi, 0.0)
            pr, pi = pr * lr - pi * li, pr * li + pi * lr
        pw_s[6] = acc_r
        pw_s[7] = acc_i

    n_i = lax.broadcasted_iota(jnp.int32, (lc, lc), 0)
    t_i = lax.broadcasted_iota(jnp.int32, (lc, lc), 1)
    sh = seg.bit_length() - 1
    perm = (t_i == (n_i & (SUBLANE - 1)) * seg + lax.shift_right_logical(n_i, 3)).astype(MXU_DTYPE)
    unperm = (t_i == (n_i & (seg - 1)) * SUBLANE + lax.shift_right_logical(n_i, sh)).astype(MXU_DTYPE)

    u = sum(_mm(perm, t) for t in _split3(u_ref[...]))
    xr, sw = _s5_in(u, bbr_ref)
    xi, _ = _s5_in(u, bbi_ref)
    for j in range(S5_BLOCKS):
        xr_s[:, pl.ds(j * sw, sw)] = xr[j]
        xi_s[:, pl.ds(j * sw, sw)] = xi[j]

    cw = 512
    for cc in range(S5_N // cw):
        cs = pl.ds(cc * cw, cw)
        lr, li = lr_ref[:, cs], li_ref[:, cs]

        def tile_rows(k):
            return pl.ds(pl.multiple_of(k * SUBLANE, SUBLANE), SUBLANE)

        def pass1(k, carry, cs=cs, lr=lr, li=li):
            gr, gi = carry
            rows = tile_rows(k)
            gr, gi = lr * gr - li * gi + xr_s[rows, cs], lr * gi + li * gr + xi_s[rows, cs]
            xr_s[rows, cs] = gr
            xi_s[rows, cs] = gi
            return gr, gi

        zero = jnp.zeros((SUBLANE, cw), F32)
        sr, si = lax.fori_loop(0, seg, pass1, (zero, zero))

        for k, s in enumerate((1, 2, 4)):
            ar, ai = pw_s[2 * k, :, cs], pw_s[2 * k + 1, :, cs]
            qr, qi = pltpu.roll(sr, s, 0), pltpu.roll(si, s, 0)
            sr, si = sr + ar * qr - ai * qi, si + ar * qi + ai * qr
        ar, ai = pw_s[6, :, cs], pw_s[7, :, cs]
        c0r, c0i = cr_s[:, cs], ci_s[:, cs]
        sr, si = sr + ar * c0r - ai * c0i, si + ar * c0i + ai * c0r
        first = _row_iota((SUBLANE, cw)) == 0
        jr = jnp.where(first, c0r, pltpu.roll(sr, 1, 0))
        ji = jnp.where(first, c0i, pltpu.roll(si, 1, 0))
        cr_s[:, cs] = _last_row(sr)
        ci_s[:, cs] = _last_row(si)

        def pass2(k, carry, cs=cs, lr=lr, li=li):
            jr, ji = carry
            rows = tile_rows(k)
            jr, ji = lr * jr - li * ji, lr * ji + li * jr
            xr_s[rows, cs] = xr_s[rows, cs] + jr
            xi_s[rows, cs] = xi_s[rows, cs] + ji
            return jr, ji

        lax.fori_loop(0, seg, pass2, (jr, ji))

    y = _s5_out(xr_s[...], xi_s[...], u, cr_ref, ci_ref, d_ref, wglu_ref, bglu_ref).astype(y_ref.dtype)
    y_ref[...] = _mm(unperm, y).astype(y_ref.dtype)

    @pl.when(c == pl.num_programs(1) - 1)
    def _():
        hlr_ref[...] = cr_s[pl.ds(0, 1), :]
        hli_ref[...] = ci_s[pl.ds(0, 1), :]


def _s5_step_kernel(u_ref, h0r_ref, h0i_ref, bbr_ref, bbi_ref, lr_ref, li_ref, cr_ref, ci_ref, d_ref, wglu_ref,
                    bglu_ref, y_ref, hlr_ref, hli_ref):
    u = u_ref[...]
    xr, _ = _s5_in(u, bbr_ref)
    xi, _ = _s5_in(u, bbi_ref)
    lr, li, sr, si = lr_ref[...], li_ref[...], h0r_ref[...], h0i_ref[...]
    hr = lr * sr - li * si + jnp.concatenate(xr, axis=1)
    hi = lr * si + li * sr + jnp.concatenate(xi, axis=1)
    hlr_ref[...] = hr
    hli_ref[...] = hi
    y_ref[...] = _s5_out(hr, hi, u, cr_ref, ci_ref, d_ref, wglu_ref, bglu_ref).astype(y_ref.dtype)


def _s5(g, uq, h0r, h0i, disc, cr, ci, d, w_glu, b_glu, layer):
    lam_r, lam_i, bbr, bbi = disc
    cw, sw = BRANCH_W // S5_BLOCKS, S5_N // S5_BLOCKS
    sds = jax.ShapeDtypeStruct
    seq = g.L > 1
    lc = min(g.L, SEQ_CHUNK) if seq else g.B
    nc = g.L // lc if seq else 1
    grid = (g.B, nc) if seq else (1, 1)
    row = (lambda b, c: (b * nc + c, 0))
    lay3 = lambda b, c: (layer, 0, 0)
    lay4 = lambda b, c: (layer, 0, 0, 0)
    w_specs = [pl.BlockSpec((None, S5_BLOCKS, cw, sw), lay4), pl.BlockSpec((None, S5_BLOCKS, cw, sw), lay4),
               pl.BlockSpec((None, 1, S5_N), lay3), pl.BlockSpec((None, 1, S5_N), lay3),
               pl.BlockSpec((None, S5_BLOCKS, sw, cw), lay4), pl.BlockSpec((None, S5_BLOCKS, sw, cw), lay4),
               pl.BlockSpec((None, 1, BRANCH_W), lay3),
               pl.BlockSpec((None, BRANCH_W, BRANCH_W), lay3),
               pl.BlockSpec((None, 1, BRANCH_W), lay3)]
    w_args = (bbr, bbi, lam_r, lam_i, cr, ci, d.reshape(-1, 1, BRANCH_W), w_glu, b_glu.reshape(-1, 1, BRANCH_W))
    if seq:
        st_spec = pl.BlockSpec((None, 1, S5_N), lambda b, c: (b, 0, 0))
        y, hlr, hli = pl.pallas_call(
            _s5_seq_kernel,
            grid=grid,
            in_specs=[pl.BlockSpec((lc, BRANCH_W), row)] + w_specs,
            out_specs=(pl.BlockSpec((lc, BRANCH_W), row), st_spec, st_spec),
            out_shape=(sds((g.T, BRANCH_W), MXU_DTYPE), sds((g.B, 1, S5_N), F32), sds((g.B, 1, S5_N), F32)),
            scratch_shapes=[pltpu.VMEM((lc, S5_N), F32), pltpu.VMEM((lc, S5_N), F32),
                            pltpu.VMEM((SUBLANE, S5_N), F32), pltpu.VMEM((SUBLANE, S5_N), F32),
                            pltpu.VMEM((8, SUBLANE, S5_N), F32)],
            compiler_params=_params("arbitrary", "arbitrary"),
            name="s5_ssm",
        )(uq, *w_args)
        return y, hlr.reshape(g.B, S5_N), hli.reshape(g.B, S5_N)
    st_spec = pl.BlockSpec((g.B, S5_N), lambda b, c: (0, 0))
    return pl.pallas_call(
        _s5_step_kernel,
        grid=grid,
        in_specs=[pl.BlockSpec((lc, BRANCH_W), row), st_spec, st_spec] + w_specs,
        out_specs=(pl.BlockSpec((lc, BRANCH_W), row), st_spec, st_spec),
        out_shape=(sds((g.T, BRANCH_W), MXU_DTYPE), sds((g.B, S5_N), F32), sds((g.B, S5_N), F32)),
        compiler_params=_params("arbitrary", "arbitrary"),
        name="s5_ssm",
    )(uq, h0r, h0i, *w_args)


def _lru_gates(xc, wa_ref, ba_ref, wx_ref, bx_ref, lam_ref):
    r = jax.nn.sigmoid(_mm(xc, wa_ref[...]) + ba_ref[...])
    ig = jax.nn.sigmoid(_mm(xc, wx_ref[...]) + bx_ref[...])
    log_a = -LRU_C * r * _softplus(-lam_ref[...])
    return jnp.exp(log_a), jnp.sqrt(1.0 - jnp.exp(2.0 * log_a)), ig


def _lru_seq_kernel(x_ref, g_ref, cw_ref, cbias_ref, wa_ref, ba_ref, wx_ref, bx_ref, lam_ref,
                    y_ref, hl_ref, xbuf, a_s, b_s, h_s):
    c = pl.program_id(1)
    lc = x_ref.shape[0]
    pad = SUBLANE

    @pl.when(c == 0)
    def _():
        xbuf[pl.ds(0, pad), :] = jnp.zeros((pad, LRU_W), F32)
        h_s[...] = jnp.zeros_like(h_s)

    xbuf[pl.ds(pad, lc), :] = x_ref[...]
    xc = cbias_ref[...]
    for j in range(CONV_W):
        xc = xc + cw_ref[pl.ds(j, 1), :] * xbuf[pl.ds(pad - (CONV_W - 1) + j, lc), :]
    new_tail = xbuf[pl.ds(lc, pad), :]
    xbuf[pl.ds(0, pad), :] = new_tail

    a, mult, ig = _lru_gates(xc, wa_ref, ba_ref, wx_ref, bx_ref, lam_ref)
    first = jnp.logical_and(c == 0, _row_iota((lc, LRU_W)) == 0)
    a_s[...] = jnp.where(first, 0.0, a)
    b_s[...] = jnp.where(first, 1.0, mult) * (ig * xc)

    def tile(rows, cs, h0):
        a, b = a_s[rows, cs], b_s[rows, cs]
        row = _row_iota(a.shape)
        for s in (1, 2, 4):
            keep = row >= s
            b = a * jnp.where(keep, pltpu.roll(b, s, 0), 0.0) + b
            a = a * jnp.where(keep, pltpu.roll(a, s, 0), 1.0)
        h = a * h0 + b
        b_s[rows, cs] = h
        return _last_row(h)

    def store(cs, h):
        h_s[:, cs] = h

    _scan_tiles(lc // SUBLANE, LRU_W, LRU_W, tile, lambda cs: h_s[:, cs], store)
    y_ref[...] = (b_s[...] * _gelu_tanh(g_ref[...])).astype(y_ref.dtype)

    @pl.when(c == pl.num_programs(1) - 1)
    def _():
        hl_ref[...] = h_s[pl.ds(0, 1), :]


def _lru_step_kernel(x_ref, g_ref, cs_ref, h0_ref, cw_ref, cbias_ref, wa_ref, ba_ref, wx_ref, bx_ref, lam_ref,
                     y_ref, hl_ref):
    x = x_ref[...]
    xc = cbias_ref[...] + cw_ref[pl.ds(CONV_W - 1, 1), :] * x
    for j in range(CONV_W - 1):
        xc = xc + cw_ref[pl.ds(j, 1), :] * cs_ref[j]
    a, mult, ig = _lru_gates(xc, wa_ref, ba_ref, wx_ref, bx_ref, lam_ref)
    h = a * h0_ref[...] + mult * (ig * xc)
    hl_ref[...] = h
    y_ref[...] = (h * _gelu_tanh(g_ref[...])).astype(y_ref.dtype)


def _block_diag(w):
    depth, nb, k, _ = w.shape
    eye = jnp.eye(nb, dtype=w.dtype)
    return jnp.einsum('lnwv,nm->lnwmv', w, eye).reshape(depth, nb * k, nb * k).astype(MXU_DTYPE)


def _lru(g, rest, conv_state, h0, p, layer, pos0):
    sds = jax.ShapeDtypeStruct
    seq = g.L > 1
    lc = min(g.L, SEQ_CHUNK) if seq else g.B
    nc = g.L // lc if seq else 1
    lay3 = lambda b, c: (layer, 0, 0)
    vec = lambda v: v.reshape(-1, 1, LRU_W)
    w_specs = [pl.BlockSpec((None, CONV_W, LRU_W), lay3), pl.BlockSpec((None, 1, LRU_W), lay3),
               pl.BlockSpec((None, LRU_W, LRU_W), lay3), pl.BlockSpec((None, 1, LRU_W), lay3),
               pl.BlockSpec((None, LRU_W, LRU_W), lay3), pl.BlockSpec((None, 1, LRU_W), lay3),
               pl.BlockSpec((None, 1, LRU_W), lay3)]
    w_args = (p['lru_conv_w'], vec(p['lru_conv_b']), p['lru_wa_bd'], vec(p['lru_b_a']), p['lru_wx_bd'],
              vec(p['lru_b_x']), vec(p['lru_lambda']))
    x_spec = pl.BlockSpec((lc, LRU_W), lambda b, c: (b * nc + c, 1))
    g_spec = pl.BlockSpec((lc, LRU_W), lambda b, c: (b * nc + c, 2))
    y_spec = pl.BlockSpec((lc, LRU_W), lambda b, c: (b * nc + c, 0))
    if seq:
        assert pos0 == 0
        y, hl = pl.pallas_call(
            _lru_seq_kernel,
            grid=(g.B, nc),
            in_specs=[x_spec, g_spec] + w_specs,
            out_specs=(y_spec, pl.BlockSpec((None, 1, LRU_W), lambda b, c: (b, 0, 0))),
            out_shape=(sds((g.T, LRU_W), MXU_DTYPE), sds((g.B, 1, LRU_W), F32)),
            scratch_shapes=[pltpu.VMEM((lc + SUBLANE, LRU_W), F32), pltpu.VMEM((lc, LRU_W), F32),
                            pltpu.VMEM((lc, LRU_W), F32), pltpu.VMEM((SUBLANE, LRU_W), F32)],
            compiler_params=_params("arbitrary", "arbitrary"),
            name="rg_lru",
        )(rest, rest, *w_args)
        return y, hl.reshape(g.B, LRU_W)
    assert pos0 > 0
    st_spec = pl.BlockSpec((g.B, LRU_W), lambda b, c: (0, 0))
    return pl.pallas_call(
        _lru_step_kernel,
        grid=(1, 1),
        in_specs=[x_spec, g_spec, pl.BlockSpec((CONV_W - 1, g.B, LRU_W), lambda b, c: (0, 0, 0)), st_spec] + w_specs,
        out_specs=(y_spec, st_spec),
        out_shape=(sds((g.T, LRU_W), MXU_DTYPE), sds((g.B, LRU_W), F32)),
        compiler_params=_params("arbitrary", "arbitrary"),
        name="rg_lru",
    )(rest, rest, jnp.swapaxes(conv_state, 0, 1), h0, *w_args)


def _l2norm(x):
    return x * lax.rsqrt(jnp.sum(x * x, axis=-1, keepdims=True) + NORM_EPS)


def _dn_gates(ab, prm):
    g = -jnp.exp(prm[0:1, :]) * _softplus(ab + prm[1:2, :])
    return g, jax.nn.sigmoid(ab)


def _dn_seq_kernel(q_ref, k_ref, v_ref, z_ref, ab_ref, cw_ref, prm_ref, ng_ref, y_ref, sfin_ref, xbuf, s_s):
    c = pl.program_id(0)
    bsz, cs, _ = q_ref.shape
    hd = DN_HEAD_DIM
    rows = bsz * cs
    pad = SUBLANE

    @pl.when(c == 0)
    def _():
        xbuf[:, pl.ds(0, pad), :] = jnp.zeros((bsz, pad, 3 * DN_W), F32)
        s_s[...] = jnp.zeros_like(s_s)

    for s, ref in enumerate((q_ref, k_ref, v_ref)):
        xbuf[:, pl.ds(pad, cs), pl.ds(s * DN_W, DN_W)] = ref[...]
    conv = None
    for j in range(CONV_W):
        t = cw_ref[pl.ds(j, 1), :] * xbuf[:, pl.ds(pad - (CONV_W - 1) + j, cs), :]
        conv = t if conv is None else conv + t
    new_tail = xbuf[:, pl.ds(cs, pad), :]
    xbuf[:, pl.ds(0, pad), :] = new_tail
    act = _silu(conv).reshape(rows, 3 * DN_W)

    glog, beta_all = _dn_gates(ab_ref[...].reshape(rows, LANE), prm_ref[...])
    row = lax.broadcasted_iota(jnp.int32, (rows, rows), 0)
    col = lax.broadcasted_iota(jnp.int32, (rows, rows), 1)
    shift = cs.bit_length() - 1
    same = lax.shift_right_logical(row, shift) == lax.shift_right_logical(col, shift)
    causal = jnp.logical_and(same, row >= col)
    strict = jnp.logical_and(same, row > col)
    gc = jnp.dot(causal.astype(F32), glog, precision=lax.Precision.HIGHEST, preferred_element_type=F32)
    gct = gc.T
    eye = (row == col).astype(F32)
    ng = ng_ref[...]
    rblk = lax.shift_right_logical(lax.broadcasted_iota(jnp.int32, (rows, bsz * hd), 0), shift)
    cblk = lax.shift_right_logical(lax.broadcasted_iota(jnp.int32, (rows, bsz * hd), 1), hd.bit_length() - 1)
    own = rblk == cblk

    def diag_blocks(m):
        return jnp.concatenate([m[b * cs:(b + 1) * cs, b * hd:(b + 1) * hd] for b in range(bsz)], axis=0)

    for h in range(DN_HEADS):
        q = _l2norm(act[:, h * hd:(h + 1) * hd]) * (hd ** -0.5)
        k = _l2norm(act[:, DN_W + h * hd:DN_W + (h + 1) * hd])
        v = act[:, 2 * DN_W + h * hd:2 * DN_W + (h + 1) * hd]
        beta = beta_all[:, DN_HEADS + h:DN_HEADS + h + 1]
        gcol = gc[:, h:h + 1]
        decay = jnp.exp(jnp.where(causal, gcol - gct[h:h + 1, :], -jnp.inf))
        kb = k * beta
        kq = _mm_nt(jnp.concatenate([kb, q], axis=0), k)
        nmat = jnp.where(strict, -(kq[:rows] * decay), 0.0)
        att = jnp.where(causal, kq[rows:] * decay, 0.0)
        rinv = eye + nmat
        xpow = _mm(nmat, nmat)
        for _ in range(cs.bit_length() - 3):
            st = _mm(jnp.concatenate([xpow, rinv], axis=0), xpow)
            xpow, rinv = st[:rows], rinv + st[rows:]
        rinv = rinv + _mm(rinv, xpow)
        eg = jnp.exp(gcol)
        uw = _mm(rinv, jnp.concatenate([v * beta, kb * eg], axis=1))
        u, w = uw[:, :hd], uw[:, hd:]
        s = s_s[h]
        ws_qs = _mm(jnp.concatenate([w, q * eg], axis=0), s)
        v_new = u - diag_blocks(ws_qs[:rows])
        o = diag_blocks(ws_qs[rows:]) + _mm(att, v_new)
        glast = [gc[(b + 1) * cs - 1:(b + 1) * cs, h:h + 1] for b in range(bsz)]
        glast_rows = jnp.concatenate([jnp.broadcast_to(t, (cs, 1)) for t in glast], axis=0)
        glast_lanes = jnp.concatenate([jnp.broadcast_to(t, (1, hd)) for t in glast], axis=1)
        v_bd = jnp.where(own, jnp.concatenate([v_new] * bsz, axis=1), 0.0)
        s_s[h] = s * jnp.exp(glast_lanes) + _mm_tn(k * jnp.exp(glast_rows - gcol), v_bd)
        o = _rms(o, ng) * _silu(z_ref[:, :, pl.ds(h * hd, hd)].reshape(rows, hd))
        y_ref[:, :, pl.ds(h * hd, hd)] = o.reshape(bsz, cs, hd).astype(y_ref.dtype)

    @pl.when(c == pl.num_programs(0) - 1)
    def _():
        for b in range(bsz):
            for h in range(DN_HEADS):
                sfin_ref[b, h] = s_s[h, :, pl.ds(b * hd, hd)]


def _dn_seq(g, uq, rest, ab, conv_w, prm, norm_g, layer):
    cs = DN_CHUNK
    assert cs & (cs - 1) == 0 and g.L % cs == 0
    nc = g.L // cs
    hd = DN_HEAD_DIM
    lay3 = lambda c: (layer, 0, 0)
    sds = jax.ShapeDtypeStruct
    sec = lambda s: pl.BlockSpec((g.B, cs, DN_W), lambda c: (0, c, s))
    y, sfin = pl.pallas_call(
        _dn_seq_kernel,
        grid=(nc,),
        in_specs=[sec(1), sec(2), sec(3), sec(0),
                  pl.BlockSpec((g.B, cs, LANE), lambda c: (0, c, 0)),
                  pl.BlockSpec((None, CONV_W, 3 * DN_W), lay3),
                  pl.BlockSpec((None, 2, LANE), lay3),
                  pl.BlockSpec((None, 1, hd), lay3)],
        out_specs=(pl.BlockSpec((g.B, cs, DN_W), lambda c: (0, c, 0)),
                   pl.BlockSpec((g.B, DN_HEADS, hd, hd), lambda c: (0, 0, 0, 0))),
        out_shape=(sds((g.B, g.L, DN_W), MXU_DTYPE), sds((g.B, DN_HEADS, hd, hd), F32)),
        scratch_shapes=[pltpu.VMEM((g.B, cs + SUBLANE, 3 * DN_W), F32),
                        pltpu.VMEM((DN_HEADS, hd, g.B * hd), F32)],
        compiler_params=_params("arbitrary"),
        name="deltanet_seq",
    )(*(3 * [uq.reshape(g.B, g.L, -1)]), rest.reshape(g.B, g.L, -1), ab.reshape(g.B, g.L, LANE), conv_w, prm,
      norm_g.reshape(-1, 1, hd))
    return y.reshape(g.T, DN_W), sfin


def _dn_step_kernel(q_ref, k_ref, v_ref, cq_ref, ck_ref, cv_ref, z_ref, ab_ref, cwq_ref, cwk_ref, cwv_ref,
                    prm_ref, ng_ref, s_ref, y_ref, so_ref, o_s, *, bsz):
    h = pl.program_id(0)
    hd = DN_HEAD_DIM

    def conv(x_ref, c_ref, w_ref):
        acc = w_ref[pl.ds(CONV_W - 1, 1), :] * x_ref[...]
        for j in range(CONV_W - 1):
            acc = acc + w_ref[pl.ds(j, 1), :] * c_ref[j]
        return _silu(acc)

    q = _l2norm(conv(q_ref, cq_ref, cwq_ref)) * (hd ** -0.5)
    k = _l2norm(conv(k_ref, ck_ref, cwk_ref))
    v = conv(v_ref, cv_ref, cwv_ref)
    glog, beta_all = _dn_gates(ab_ref[...], prm_ref[...])
    lane = lax.broadcasted_iota(jnp.int32, (bsz, LANE), 1)
    eg = jnp.exp(jnp.sum(jnp.where(lane == h, glog, 0.0), axis=-1, keepdims=True))
    beta = jnp.sum(jnp.where(lane == DN_HEADS + h, beta_all, 0.0), axis=-1, keepdims=True)
    att = jnp.sum(q * k, axis=-1, keepdims=True)
    kt = k.T
    qt = q.T
    for b in range(bsz):
        s = s_ref[b]
        kcol = kt[:, b:b + 1]
        ks = jnp.sum(s * kcol, axis=0, keepdims=True)
        qs = jnp.sum(s * qt[:, b:b + 1], axis=0, keepdims=True)
        eg_b, beta_b = eg[b:b + 1, :], beta[b:b + 1, :]
        v_new = beta_b * v[b:b + 1, :] - (beta_b * eg_b) * ks
        o_s[pl.ds(b, 1), :] = eg_b * qs + att[b:b + 1, :] * v_new
        so_ref[b] = s * eg_b + kcol * v_new
    y_ref[...] = (_rms(o_s[...], ng_ref[...]) * _silu(z_ref[...])).astype(y_ref.dtype)


def _dn_step(g, uq, conv_state, rest, ab, conv_w, prm, norm_g, s0_all, layer):
    bsz = g.B
    hd = DN_HEAD_DIM
    conv_t = jnp.swapaxes(conv_state, 0, 1)
    sec = lambda s: (lambda h: (0, (s + 1) * DN_HEADS + h))
    sec3 = lambda s: (lambda h: (0, 0, s * DN_HEADS + h))
    secw = lambda s: (lambda h: (layer, 0, s * DN_HEADS + h))
    lay3 = lambda h: (layer, 0, 0)
    sds = jax.ShapeDtypeStruct
    return pl.pallas_call(
        functools.partial(_dn_step_kernel, bsz=bsz),
        grid=(DN_HEADS,),
        in_specs=[pl.BlockSpec((bsz, hd), sec(s)) for s in range(3)]
                 + [pl.BlockSpec((CONV_W - 1, bsz, hd), sec3(s)) for s in range(3)]
                 + [pl.BlockSpec((bsz, hd), lambda h: (0, h)),
                    pl.BlockSpec((bsz, LANE), lambda h: (0, 0))]
                 + [pl.BlockSpec((None, CONV_W, hd), secw(s)) for s in range(3)]
                 + [pl.BlockSpec((None, 2, LANE), lay3),
                    pl.BlockSpec((None, 1, hd), lay3),
                    pl.BlockSpec((None, bsz, None, hd, hd), lambda h: (layer, 0, h, 0, 0))],
        out_specs=(pl.BlockSpec((bsz, hd), lambda h: (0, h)),
                   pl.BlockSpec((bsz, None, hd, hd), lambda h: (0, h, 0, 0))),
        out_shape=(sds((bsz, DN_W), MXU_DTYPE), sds((bsz, DN_HEADS, hd, hd), F32)),
        scratch_shapes=[pltpu.VMEM((bsz, hd), F32)],
        compiler_params=_params("arbitrary"),
        name="deltanet_step",
    )(uq, uq, uq, conv_t, conv_t, conv_t, rest, ab, conv_w, conv_w, conv_w, prm,
      norm_g.reshape(-1, 1, hd), s0_all)


def _pack_pairs(h):
    half = h.shape[1] // 2
    hb = h.astype(MXU_DTYPE).astype(F32)
    lo = lax.shift_right_logical(pltpu.bitcast(hb[:, :half], jnp.uint32), jnp.uint32(16))
    return lo | (pltpu.bitcast(hb[:, half:], jnp.uint32) & jnp.uint32(0xFFFF0000))


def _unpack_pairs(w):
    lo = pltpu.bitcast(lax.shift_left(w, jnp.uint32(16)), F32)
    hi = pltpu.bitcast(w & jnp.uint32(0xFFFF0000), F32)
    return jnp.concatenate([lo, hi], axis=1).astype(MXU_DTYPE)


def _router_kernel(xp_ref, xs_ref, g_ref, scp_ref, shp_ref, scs_ref, shs_ref, wr_ref, br_ref, h_ref, r_ref, *, n_p):
    is_s = pl.program_id(0) >= n_p
    x = jnp.where(is_s, xs_ref[...], xp_ref[...])
    sc = jnp.where(is_s, scs_ref[...], scp_ref[...])
    sh = jnp.where(is_s, shs_ref[...], shp_ref[...])
    h = _rms(x, g_ref[...]) * (1.0 + sc) + sh
    h_ref[...] = _pack_pairs(h)
    logits = _mm(h, wr_ref[...]) + br_ref[...]
    lane = lax.broadcasted_iota(jnp.int32, logits.shape, 1).astype(F32)
    ninf = -jnp.inf
    big = float(LANE)
    lg = jnp.where(lane < N_GROUPS, logits, ninf)
    gmax = jnp.max(lg, axis=-1, keepdims=True)
    gsel = jnp.min(jnp.where(lg == gmax, lane, big), axis=-1, keepdims=True)
    pg = 1.0 / jnp.sum(jnp.where(lane < N_GROUPS, jnp.exp(logits - gmax), 0.0), axis=-1, keepdims=True)
    lo = N_GROUPS + EXPERTS_PER_GROUP * gsel
    le = jnp.where(jnp.logical_and(lane >= lo, lane < lo + EXPERTS_PER_GROUP), logits, ninf)
    v1 = jnp.max(le, axis=-1, keepdims=True)
    i1 = jnp.min(jnp.where(le == v1, lane, big), axis=-1, keepdims=True)
    le2 = jnp.where(lane == i1, ninf, le)
    v2 = jnp.max(le2, axis=-1, keepdims=True)
    i2 = jnp.min(jnp.where(le2 == v2, lane, big), axis=-1, keepdims=True)
    e2 = jnp.exp(v2 - v1)
    w1 = pg / (1.0 + e2)
    w2 = pg * e2 / (1.0 + e2)
    r_ref[...] = jnp.where(lane == 0, i1 - N_GROUPS,
                           jnp.where(lane == 1, i2 - N_GROUPS,
                                     jnp.where(lane == 2, w1, jnp.where(lane == 3, w2, 0.0))))


def _router(groups, xs, gain, mods, w_route, b_route, layer):
    gp, gs = groups
    tm = min(ROUTER_ROWS, gs.T)
    assert gp.L % tm == 0 and gs.T % tm == 0
    n_p, n_s, nt = gp.T // tm, gs.T // tm, gp.L // tm
    t_all = gp.T + gs.T
    p_row = lambda i, j: (jnp.minimum(i, n_p - 1), 0)
    s_row = lambda i, j: (jnp.maximum(i - n_p, 0), 0)
    p_mod = lambda k: pl.BlockSpec((None, 1, D_MODEL), lambda i, j: (jnp.minimum(i, n_p - 1) // nt, 0, k))
    s_mod = lambda k: pl.BlockSpec((tm, D_MODEL), lambda i, j: (jnp.maximum(i - n_p, 0), k))
    sds = jax.ShapeDtypeStruct
    return pl.pallas_call(
        functools.partial(_router_kernel, n_p=n_p),
        grid=(n_p + n_s, 1),
        in_specs=[pl.BlockSpec((tm, D_MODEL), p_row), pl.BlockSpec((tm, D_MODEL), s_row),
                  pl.BlockSpec((1, D_MODEL), lambda i, j: (0, 0)),
                  p_mod(4), p_mod(3), s_mod(4), s_mod(3),
                  pl.BlockSpec((None, D_MODEL, LANE), lambda i, j: (layer, 0, 0)),
                  pl.BlockSpec((None, 1, LANE), lambda i, j: (layer, 0, 0))],
        out_specs=(pl.BlockSpec((tm, D_MODEL // 2), lambda i, j: (i, 0)), pl.BlockSpec((tm, LANE), lambda i, j: (i, 0))),
        out_shape=(sds((t_all, D_MODEL // 2), jnp.uint32), sds((t_all, LANE), F32)),
        compiler_params=_params("arbitrary", "arbitrary"),
        name="moe_router",
    )(xs[0], xs[1], gain.reshape(1, D_MODEL), mods[0], mods[0], mods[1], mods[1], w_route, b_route)


def _row_copy(src_hbm, row, dst, dst_row, sem):
    return pltpu.make_async_copy(src_hbm.at[pl.ds(row, 1)], dst.at[pl.ds(dst_row, 1)], sem)


def _experts_kernel(be_ref, tok_ref, nact_ref, first_ref, nxt_ref, par_ref, h_hbm, wg_hbm, wu_hbm, wd_hbm, y_ref,
                    xbuf, gsem, wst_g, wst_u, wst_d, wsem, wgt, wut, wdt, *, layer):
    b = pl.program_id(0)
    nact = nact_ref[0]
    rb = MOE_ROWS

    def gather(blk, slot):
        for r in range(rb):
            _row_copy(h_hbm, tok_ref[blk * rb + r], xbuf.at[slot], r, gsem.at[slot]).start(priority=1)

    def weight_copies(e, slot):
        return (pltpu.make_async_copy(wg_hbm.at[layer, e], wst_g.at[slot], wsem.at[slot]),
                pltpu.make_async_copy(wu_hbm.at[layer, e], wst_u.at[slot], wsem.at[slot]),
                pltpu.make_async_copy(wd_hbm.at[layer, e], wst_d.at[slot], wsem.at[slot]))

    @pl.when(b == 0)
    def _():
        for cp in weight_copies(be_ref[0], par_ref[0]):
            cp.start()
        gather(0, 0)

    @pl.when(b + 1 < nact)
    def _():
        gather(b + 1, (b + 1) % 2)

    @pl.when(jnp.logical_and(b < nact, first_ref[b] == 1))
    def _():
        slot = par_ref[b]
        for cp in weight_copies(be_ref[b], slot):
            cp.wait()

        @pl.when(nxt_ref[b] >= 0)
        def _():
            for cp in weight_copies(nxt_ref[b], 1 - slot):
                cp.start()

        ck = XPOSE_CHUNK
        for j in range(D_MODEL // ck):
            wgt[:, pl.ds(j * ck, ck)] = wst_g[slot, pl.ds(j * ck, ck), :].T.astype(MXU_DTYPE)
            wut[:, pl.ds(j * ck, ck)] = wst_u[slot, pl.ds(j * ck, ck), :].T.astype(MXU_DTYPE)
            wdt[pl.ds(j * ck, ck), :] = wst_d[slot, :, pl.ds(j * ck, ck)].T.astype(MXU_DTYPE)

    @pl.when(b < nact)
    def _():
        slot = b % 2
        for r in range(rb):
            _row_copy(h_hbm, 0, xbuf.at[slot], r, gsem.at[slot]).wait()
        x = _unpack_pairs(xbuf[slot])
        hid_t = _silu(_mm_nt(wgt[...], x)) * _mm_nt(wut[...], x)
        y_ref[...] = _mm(wdt[...], hid_t).T

    @pl.when(b >= nact)
    def _():
        y_ref[...] = jnp.zeros_like(y_ref)


def _experts(h_all, plan, w_gate, w_up, w_down, layer):
    block_e, row_token, n_active, first, nxt, par = plan
    n_blocks = block_e.shape[0]
    rb = MOE_ROWS
    any_spec = pl.BlockSpec(memory_space=pl.ANY)
    return pl.pallas_call(
        functools.partial(_experts_kernel, layer=layer),
        grid_spec=pltpu.PrefetchScalarGridSpec(
            num_scalar_prefetch=6,
            grid=(n_blocks,),
            in_specs=[any_spec, any_spec, any_spec, any_spec],
            out_specs=pl.BlockSpec((rb, D_MODEL), lambda b, *_: (b, 0)),
            scratch_shapes=[pltpu.VMEM((2, rb, D_MODEL // 2), jnp.uint32), pltpu.SemaphoreType.DMA((2,)),
                            pltpu.VMEM((2, D_MODEL, D_EXPERT), F32), pltpu.VMEM((2, D_MODEL, D_EXPERT), F32),
                            pltpu.VMEM((2, D_EXPERT, D_MODEL), F32), pltpu.SemaphoreType.DMA((2,)),
                            pltpu.VMEM((D_EXPERT, D_MODEL), MXU_DTYPE), pltpu.VMEM((D_EXPERT, D_MODEL), MXU_DTYPE),
                            pltpu.VMEM((D_MODEL, D_EXPERT), MXU_DTYPE)]),
        out_shape=jax.ShapeDtypeStruct((n_blocks * rb, D_MODEL), F32),
        compiler_params=_params("arbitrary"),
        name="moe_experts",
    )(block_e, row_token, n_active, first, nxt, par, h_all, w_gate, w_up, w_down)


def _combine_kernel(pos_ref, y_hbm, x_ref, r_ref, g_ref, o_ref, buf, sem, *, tok0, tm):
    i = pl.program_id(0)
    n = pl.num_programs(0)

    def gather(tile, slot):
        for r in range(tm):
            for k in range(TOP_K):
                p = pos_ref[(tok0 + tile * tm + r) * TOP_K + k]
                _row_copy(y_hbm, p, buf.at[slot], k * tm + r, sem.at[slot]).start()

    @pl.when(i == 0)
    def _():
        gather(0, 0)

    @pl.when(i + 1 < n)
    def _():
        gather(i + 1, (i + 1) % 2)

    slot = i % 2
    for r in range(TOP_K * tm):
        _row_copy(y_hbm, 0, buf.at[slot], r, sem.at[slot]).wait()
    route = r_ref[...]
    moe = buf[slot, pl.ds(0, tm), :] * route[:, 2:3] + buf[slot, pl.ds(tm, tm), :] * route[:, 3:4]
    o_ref[...] = x_ref[...] + g_ref[...] * moe


def _combine(g, tok0, pos, y_rows, x, route_all, mod):
    tm = min(COMBINE_ROWS, g.tm)
    gc = g._replace(tm=tm)
    nt = gc.nt
    assert tok0 % tm == 0
    blk0 = tok0 // tm
    if g.L == 1:
        gate_spec = pl.BlockSpec((tm, D_MODEL), lambda i, pos: (i, 5))
    else:
        gate_spec = pl.BlockSpec((None, 1, D_MODEL), lambda i, pos: (i // nt, 0, 5))
    return pl.pallas_call(
        functools.partial(_combine_kernel, tok0=tok0, tm=tm),
        grid_spec=pltpu.PrefetchScalarGridSpec(
            num_scalar_prefetch=1,
            grid=(gc.n_tiles,),
            in_specs=[pl.BlockSpec(memory_space=pl.ANY),
                      pl.BlockSpec((tm, D_MODEL), lambda i, pos: (i, 0)),
                      pl.BlockSpec((tm, LANE), lambda i, pos: (blk0 + i, 0)),
                      gate_spec],
            out_specs=pl.BlockSpec((tm, D_MODEL), lambda i, pos: (i, 0)),
            scratch_shapes=[pltpu.VMEM((2, TOP_K * tm, D_MODEL), F32), pltpu.SemaphoreType.DMA((2,))]),
        out_shape=jax.ShapeDtypeStruct((g.T, D_MODEL), F32),
        compiler_params=_params("arbitrary"),
        name="moe_combine",
    )(pos, y_rows, x, route_all, mod)


def _dispatch_plan(route_all):
    rb = MOE_ROWS
    n_tok = route_all.shape[0]
    n_assign = n_tok * TOP_K
    n_blocks = -(-(n_assign + N_EXPERTS * (rb - 1)) // rb)
    flat_e = route_all[:, :TOP_K].astype(jnp.int32).reshape(-1)
    order = jnp.argsort(flat_e).astype(jnp.int32)
    bounds = jnp.searchsorted(flat_e[order], jnp.arange(N_EXPERTS + 1, dtype=jnp.int32)).astype(jnp.int32)
    start, counts = bounds[:-1], bounds[1:] - bounds[:-1]
    rank = jnp.argsort(order).astype(jnp.int32) - start[flat_e]
    padded = (counts + rb - 1) // rb * rb
    pad_end = jnp.cumsum(padded)
    pad_start = pad_end - padded
    pos = pad_start[flat_e] + rank
    n_active = (pad_end[-1] // rb).astype(jnp.int32)
    blk = jnp.minimum(jnp.arange(n_blocks, dtype=jnp.int32), n_active - 1)
    block_e = jnp.minimum(jnp.searchsorted(pad_end, blk * rb, side='right'), N_EXPERTS - 1).astype(jnp.int32)
    row = jnp.arange(n_blocks * rb, dtype=jnp.int32)
    row_e = jnp.repeat(block_e, rb)
    idx = row - pad_start[row_e]
    src = jnp.clip(start[row_e] + idx, 0, n_assign - 1)
    row_token = jnp.where(jnp.logical_and(idx < counts[row_e], row < pad_end[-1]), order[src] // TOP_K, 0)
    bidx = jnp.arange(n_blocks, dtype=jnp.int32)
    first = jnp.logical_and(bidx < n_active, jnp.logical_or(bidx == 0, block_e != jnp.roll(block_e, 1)))
    par = (jnp.cumsum(first.astype(jnp.int32)) - 1) % 2
    nxt_first = lax.cummin(jnp.where(first, bidx, n_blocks)[::-1])[::-1]
    nxt_idx = jnp.concatenate([nxt_first[1:], jnp.full((1,), n_blocks, jnp.int32)])
    nxt = jnp.where(nxt_idx < n_blocks, block_e[jnp.minimum(nxt_idx, n_blocks - 1)], -1)
    plan = (block_e, row_token.astype(jnp.int32), n_active.reshape(1), first.astype(jnp.int32),
            nxt.astype(jnp.int32), par.astype(jnp.int32))
    return plan, pos.astype(jnp.int32)


def _layer(layer, groups, xs, mods, states, p, state_dn):
    x1s, new_states = [], []
    for g, x, mod, st in zip(groups, xs, mods, states):
        s5_re, s5_im, dn_conv, lru_h, lru_conv = st
        single = g.L == 1
        h = _norm_mod(g, x, p['norm1_g'][layer], mod, 0, 1)
        uq = _proj(g, h, p['w_pack'], layer, 0, BRANCH_W + 3 * DN_W, 1024)
        ab = _proj(g, h, p['w_pack'], layer, COL_AB, LANE, LANE)
        rest = _proj(g, h, p['w_pack'], layer, PACK_REST, REST_GATE, 512)

        ya, n_re, n_im = _s5(g, uq, s5_re, s5_im, p['s5_disc'], p['s5_cr'], p['s5_ci'], p['s5_d'],
                             p['s5_w_glu'], p['s5_b_glu'], layer)
        if single:
            qkv = uq[:, BRANCH_W:]
            lru_x = rest[:, DN_W:DN_W + LRU_W]
            yb, n_dn = _dn_step(g, uq, dn_conv, rest, ab, p['dn_conv_w'], p['dn_prm'], p['dn_norm_g'], state_dn, layer)
            n_dn_conv = jnp.concatenate([dn_conv[:, 1:], qkv[:, None, :]], axis=1)
            n_lru_conv = jnp.concatenate([lru_conv[:, 1:], lru_x[:, None, :]], axis=1)
            yc, n_lru = _lru(g, rest, lru_conv, lru_h, p, layer, pos0=p['past_len'])
        else:
            yb, n_dn = _dn_seq(g, uq, rest, ab, p['dn_conv_w'], p['dn_prm'], p['dn_norm_g'], layer)
            tail = lambda a: a.reshape(g.B, g.L, -1)[:, g.L - (CONV_W - 1):]
            n_dn_conv = tail(uq)[:, :, BRANCH_W:]
            n_lru_conv = tail(rest)[:, :, DN_W:DN_W + LRU_W]
            yc, n_lru = _lru(g, rest, None, None, p, layer, pos0=0)

        merged = _merge(g, h, ya, yb, yc, p['w_pack'], p['w_branch'], layer)
        x1 = _out_proj(g, merged, p['w_out'], layer, x, mod, 2)
        x1s.append(x1)
        new_states.append((n_re.reshape(g.B, S5_GROUPS, S5_STATE), n_im.reshape(g.B, S5_GROUPS, S5_STATE),
                           n_dn, n_dn_conv, n_lru, n_lru_conv))

    h_all, route_all = _router(groups, x1s, p['norm2_g'][layer], mods, p['w_route'], p['b_route'], layer)
    plan, pos = _dispatch_plan(route_all)
    y_rows = _experts(h_all, plan, p['w_e_gate'], p['w_e_up'], p['w_e_down'], layer)
    outs, tok0 = [], 0
    for g, x1, mod in zip(groups, x1s, mods):
        outs.append(_combine(g, tok0, pos, y_rows, x1, route_all, mod))
        tok0 += g.T
    return outs, new_states


def kernel(x_prompt, x_sample, c_prompt, c_sample, state_s5_re, state_s5_im, state_dn, state_dn_conv, state_lru, state_lru_conv, final_g, norm1_g, norm2_g, w_ada, b_ada, w_in, s5_log_dt, s5_a_re, s5_a_im, s5_b_re, s5_b_im, s5_c_re, s5_c_im, s5_d, s5_w_glu, s5_b_glu, dn_conv_w, dn_a_log, dn_dt_bias, dn_norm_g, lru_conv_w, lru_conv_b, lru_w_a, lru_b_a, lru_w_x, lru_b_x, lru_lambda, w_branch, w_out, w_rg, b_rg, w_re, b_re, w_e_gate, w_e_up, w_e_down):
    depth = w_in.shape[0]
    bp, lp, _ = x_prompt.shape
    bs, ls, _ = x_sample.shape
    assert ls == 1 and bs % SUBLANE == 0 and lp % SEQ_CHUNK == 0
    gp = _Group(bp, lp, min(lp, 1024))
    gs = _Group(bs, 1, bs)
    groups = (gp, gs)

    pad = (-bp) % SUBLANE
    c_all = jnp.concatenate([c_prompt, jnp.zeros((pad, D_MODEL), F32), c_sample], axis=0)
    mod_all = _ada_mod(c_all, w_ada, b_ada)

    prm = jnp.zeros((depth, 2, LANE), F32)
    prm = prm.at[:, 0, :DN_HEADS].set(dn_a_log).at[:, 1, :DN_HEADS].set(dn_dt_bias)
    w_route = jnp.concatenate([w_rg, w_re, jnp.zeros((depth, D_MODEL, LANE - N_GROUPS - N_EXPERTS), F32)], axis=-1)
    b_route = jnp.concatenate([b_rg, b_re, jnp.zeros((depth, LANE - N_GROUPS - N_EXPERTS), F32)], axis=-1)
    p = dict(norm1_g=norm1_g, norm2_g=norm2_g,
             w_pack=jnp.concatenate([w_in[:, :, :COL_AB + LANE],
                                     jnp.zeros((depth, D_MODEL, PACK_REST - COL_AB - LANE), F32),
                                     w_in[:, :, COL_REST:]], axis=-1).astype(MXU_DTYPE),
             s5_disc=_s5_discretize(s5_log_dt, s5_a_re, s5_a_im, s5_b_re, s5_b_im),
             s5_cr=_s5_out_blocks(s5_c_re), s5_ci=_s5_out_blocks(s5_c_im),
             s5_d=s5_d, s5_w_glu=s5_w_glu, s5_b_glu=s5_b_glu,
             dn_conv_w=dn_conv_w, dn_prm=prm, dn_norm_g=dn_norm_g,
             lru_conv_w=lru_conv_w, lru_conv_b=lru_conv_b, lru_wa_bd=_block_diag(lru_w_a), lru_b_a=lru_b_a,
             lru_wx_bd=_block_diag(lru_w_x), lru_b_x=lru_b_x, lru_lambda=lru_lambda,
             w_branch=w_branch, w_out=w_out, w_route=w_route, b_route=b_route.reshape(depth, 1, LANE),
             w_e_gate=w_e_gate, w_e_up=w_e_up, w_e_down=w_e_down, past_len=PAST_LEN)

    xs = [x_prompt.reshape(gp.T, D_MODEL), x_sample.reshape(gs.T, D_MODEL)]
    per_layer = []
    for l in range(depth):
        mods = [mod_all[l, :bp].reshape(bp, 1, 6 * D_MODEL), mod_all[l, bp + pad:]]
        states = [(None, None, None, None, None),
                  (state_s5_re[l].reshape(bs, S5_N), state_s5_im[l].reshape(bs, S5_N), state_dn_conv[l],
                   state_lru[l], state_lru_conv[l])]
        xs, new_states = _layer(l, groups, xs, mods, states, p, state_dn)
        per_layer.append(new_states)

    y_prompt = _final_norm(gp, xs[0], final_g).reshape(bp, lp, D_MODEL)
    y_sample = _final_norm(gs, xs[1], final_g).reshape(bs, 1, D_MODEL)
    stack = lambda gi: tuple(jnp.stack([per_layer[l][gi][k] for l in range(depth)]) for k in range(6))
    return (y_prompt, y_sample) + stack(0) + stack(1)
```

```python
import functools
import math
from typing import NamedTuple

import jax
import jax.numpy as jnp
from jax import lax
from jax.experimental import pallas as pl
from jax.experimental.pallas import tpu as pltpu

F32 = jnp.float32
MXU_DTYPE = jnp.bfloat16

D_MODEL = 2048
BRANCH_W = D_MODEL // 4
S5_GROUP = 16
S5_GROUPS = BRANCH_W // S5_GROUP
S5_STATE = 64
S5_N = S5_GROUPS * S5_STATE
S5_BLOCKS = 4
DN_HEADS = 4
DN_HEAD_DIM = 128
DN_W = DN_HEADS * DN_HEAD_DIM
DN_CHUNK = 64
CONV_W = 4
LRU_W = BRANCH_W
LRU_BLOCKS = 4
LRU_C = 8.0
N_GROUPS = 4
EXPERTS_PER_GROUP = 8
N_EXPERTS = N_GROUPS * EXPERTS_PER_GROUP
TOP_K = 2
D_EXPERT = D_MODEL // 4
NORM_EPS = 1e-6
PAST_LEN = 16384

COL_AB = BRANCH_W + 3 * DN_W
COL_REST = COL_AB + 2 * DN_HEADS
REST_GATE = DN_W + 2 * LRU_W
PACK_REST = 2560

LANE = 128
SUBLANE = 8
VMEM_LIMIT = 56 * 1024 * 1024
SEQ_CHUNK = 256
MOE_ROWS = 256
COMBINE_ROWS = 128
XPOSE_CHUNK = 256
ROUTER_ROWS = 128


class _Group(NamedTuple):
    B: int
    L: int
    tm: int

    @property
    def T(self):
        return self.B * self.L

    @property
    def nt(self):
        return max(self.L // self.tm, 1)

    @property
    def n_tiles(self):
        return self.T // self.tm


def _params(*sem):
    return pltpu.CompilerParams(dimension_semantics=sem, vmem_limit_bytes=VMEM_LIMIT)


def _mm(a, b):
    return jnp.dot(a.astype(MXU_DTYPE), b.astype(MXU_DTYPE), preferred_element_type=F32)


def _mm_nt(a, b):
    return lax.dot_general(a.astype(MXU_DTYPE), b.astype(MXU_DTYPE), (((1,), (1,)), ((), ())),
                           preferred_element_type=F32)


def _mm_tn(a, b):
    return lax.dot_general(a.astype(MXU_DTYPE), b.astype(MXU_DTYPE), (((0,), (0,)), ((), ())),
                           preferred_element_type=F32)


def _silu(x):
    return x * jax.nn.sigmoid(x)


def _gelu_tanh(x):
    return 0.5 * x * (1.0 + jnp.tanh(math.sqrt(2.0 / math.pi) * (x + 0.044715 * (x * x * x))))


def _softplus(x):
    return jnp.maximum(x, 0.0) + jnp.log1p(jnp.exp(-jnp.abs(x)))


def _rms(x, gain):
    return x * lax.rsqrt(jnp.mean(x * x, axis=-1, keepdims=True) + NORM_EPS) * gain


def _row_iota(shape):
    return lax.broadcasted_iota(jnp.int32, shape, len(shape) - 2)


def _last_row(x):
    return jnp.broadcast_to(x[SUBLANE - 1:SUBLANE], x.shape)


def _nat_spec(g, tn, col0=0):
    return pl.BlockSpec((g.tm, tn), lambda i, j: (i, col0 + j))


def _mod_spec(g, k, tn):
    nb, nt = D_MODEL // tn, g.nt
    if g.L == 1:
        return pl.BlockSpec((g.tm, tn), lambda i, j: (i, k * nb + j))
    return pl.BlockSpec((None, 1, tn), lambda i, j: (i // nt, 0, k * nb + j))


def _ada_kernel(c_ref, w_ref, b_ref, o_ref):
    o_ref[...] = _mm(_silu(c_ref[...]), w_ref[...]) + b_ref[...]


def _ada_mod(c_all, w_ada, b_ada):
    depth, _, n6 = w_ada.shape
    rows = c_all.shape[0]
    tn = 1024
    return pl.pallas_call(
        _ada_kernel,
        grid=(depth, n6 // tn),
        in_specs=[pl.BlockSpec((rows, D_MODEL), lambda l, j: (0, 0)),
                  pl.BlockSpec((None, D_MODEL, tn), lambda l, j: (l, 0, j)),
                  pl.BlockSpec((None, 1, tn), lambda l, j: (l, 0, j))],
        out_specs=pl.BlockSpec((None, rows, tn), lambda l, j: (l, 0, j)),
        out_shape=jax.ShapeDtypeStruct((depth, rows, n6), F32),
        compiler_params=_params("arbitrary", "arbitrary"),
        name="ada_mod",
    )(c_all, w_ada, b_ada.reshape(depth, 1, n6))


def _norm_mod_kernel(x_ref, g_ref, sc_ref, sh_ref, o_ref):
    y = _rms(x_ref[...], g_ref[...])
    o_ref[...] = (y * (1.0 + sc_ref[...]) + sh_ref[...]).astype(o_ref.dtype)


def _norm_mod(g, x, gain, mod, k_shift, k_scale):
    gm = g._replace(tm=min(g.tm, 512))
    return pl.pallas_call(
        _norm_mod_kernel,
        grid=(gm.n_tiles, 1),
        in_specs=[_nat_spec(gm, D_MODEL),
                  pl.BlockSpec((1, D_MODEL), lambda i, j: (0, 0)),
                  _mod_spec(gm, k_scale, D_MODEL),
                  _mod_spec(gm, k_shift, D_MODEL)],
        out_specs=_nat_spec(gm, D_MODEL),
        out_shape=jax.ShapeDtypeStruct((g.T, D_MODEL), MXU_DTYPE),
        compiler_params=_params("arbitrary", "arbitrary"),
        name="norm_mod",
    )(x, gain.reshape(1, D_MODEL), mod, mod)


def _final_norm_kernel(x_ref, g_ref, o_ref):
    o_ref[...] = _rms(x_ref[...], g_ref[...])


def _final_norm(g, x, gain):
    gm = g._replace(tm=min(g.tm, 512))
    return pl.pallas_call(
        _final_norm_kernel,
        grid=(gm.n_tiles, 1),
        in_specs=[_nat_spec(gm, D_MODEL), pl.BlockSpec((1, D_MODEL), lambda i, j: (0, 0))],
        out_specs=_nat_spec(gm, D_MODEL),
        out_shape=jax.ShapeDtypeStruct((g.T, D_MODEL), F32),
        compiler_params=_params("arbitrary", "arbitrary"),
        name="final_norm",
    )(x, gain.reshape(1, D_MODEL))


def _proj_kernel(a_ref, w_ref, o_ref):
    o_ref[...] = _mm(a_ref[...], w_ref[...]).astype(o_ref.dtype)


def _proj(g, a, w, layer, col0, n, tn):
    cb = col0 // tn
    return pl.pallas_call(
        _proj_kernel,
        grid=(g.n_tiles, n // tn),
        in_specs=[pl.BlockSpec((g.tm, D_MODEL), lambda i, j: (i, 0)),
                  pl.BlockSpec((None, D_MODEL, tn), lambda i, j: (layer, 0, cb + j))],
        out_specs=_nat_spec(g, tn),
        out_shape=jax.ShapeDtypeStruct((g.T, n), F32),
        compiler_params=_params("arbitrary", "arbitrary"),
        name="in_proj",
    )(a, w)


def _out_proj_kernel(a_ref, w_ref, x_ref, g_ref, o_ref):
    o_ref[...] = x_ref[...] + g_ref[...] * _mm(a_ref[...], w_ref[...])


def _out_proj(g, merged, w_out, layer, x, mod, k_gate):
    tn = 512
    return pl.pallas_call(
        _out_proj_kernel,
        grid=(g.n_tiles, D_MODEL // tn),
        in_specs=[pl.BlockSpec((g.tm, D_MODEL), lambda i, j: (i, 0)),
                  pl.BlockSpec((None, D_MODEL, tn), lambda i, j: (layer, 0, j)),
                  _nat_spec(g, tn),
                  _mod_spec(g, k_gate, tn)],
        out_specs=_nat_spec(g, tn),
        out_shape=jax.ShapeDtypeStruct((g.T, D_MODEL), F32),
        compiler_params=_params("arbitrary", "arbitrary"),
        name="out_proj",
    )(merged, w_out, x, mod)


def _merge_kernel(h_ref, ya_ref, yb_ref, yc_ref, wg0, wg1, wg2, wb0, wb1, wb2, o_ref):
    h = h_ref[...]
    acc = None
    for y_ref, wg, wb in ((ya_ref, wg0, wb0), (yb_ref, wg1, wb1), (yc_ref, wg2, wb2)):
        t = jax.nn.sigmoid(_mm(h, wg[...])) * _mm(y_ref[...], wb[...])
        acc = t if acc is None else acc + t
    o_ref[...] = acc.astype(o_ref.dtype)


def _merge(g, h, ya, yb, yc, w_pack, w_branch, layer):
    tn = 256
    gate_specs = [pl.BlockSpec((None, D_MODEL, tn),
                               functools.partial(lambda i, j, c: (layer, 0, c + j), c=(PACK_REST + REST_GATE + n * D_MODEL) // tn))
                  for n in range(3)]
    br_specs = [pl.BlockSpec((None, None, BRANCH_W, tn), functools.partial(lambda i, j, n: (layer, n, 0, j), n=n))
                for n in range(3)]
    y_spec = pl.BlockSpec((g.tm, BRANCH_W), lambda i, j: (i, 0))
    return pl.pallas_call(
        _merge_kernel,
        grid=(g.n_tiles, D_MODEL // tn),
        in_specs=[pl.BlockSpec((g.tm, D_MODEL), lambda i, j: (i, 0)), y_spec, y_spec, y_spec] + gate_specs + br_specs,
        out_specs=_nat_spec(g, tn),
        out_shape=jax.ShapeDtypeStruct((g.T, D_MODEL), MXU_DTYPE),
        compiler_params=_params("arbitrary", "arbitrary"),
        name="merge",
    )(h, ya, yb, yc, w_pack, w_pack, w_pack, w_branch, w_branch, w_branch)


def _s5_disc_kernel(ldt_ref, ar_ref, ai_ref, br_ref, bi_ref, lr_ref, li_ref, bbr_ref, bbi_ref):
    dt = jnp.exp(ldt_ref[...])
    ar, ai = ar_ref[...], ai_ref[...]
    mag = jnp.exp(ar * dt)
    lr = mag * jnp.cos(ai * dt)
    li = mag * jnp.sin(ai * dt)
    den = ar * ar + ai * ai
    fr = ((lr - 1.0) * ar + li * ai) / den
    fi = (li * ar - (lr - 1.0) * ai) / den
    lr_ref[...] = lr
    li_ref[...] = li
    br, bi = br_ref[...], bi_ref[...]
    bbr_ref[...] = fr * br - fi * bi
    bbi_ref[...] = fr * bi + fi * br


def _s5_discretize(log_dt, a_re, a_im, b_re, b_im):
    depth = log_dt.shape[0]
    n = depth * S5_GROUPS
    rows = n * S5_GROUP
    rep = lambda v, w: jnp.repeat(v.reshape(n, w), S5_GROUP, axis=0)
    brt = jnp.swapaxes(b_re, -1, -2).reshape(rows, S5_STATE)
    bit = jnp.swapaxes(b_im, -1, -2).reshape(rows, S5_STATE)
    sds = jax.ShapeDtypeStruct
    lr, li, bbr, bbi = pl.pallas_call(
        _s5_disc_kernel,
        out_shape=tuple(sds((rows, S5_STATE), F32) for _ in range(4)),
        name="s5_discretize",
    )(rep(log_dt, 1), rep(a_re, S5_STATE), rep(a_im, S5_STATE), brt, bit)
    gpb = S5_GROUPS // S5_BLOCKS
    eye = jnp.eye(gpb, dtype=F32)

    def blocks_in(m):
        m = m.reshape(depth, S5_BLOCKS, gpb, S5_GROUP, S5_STATE)
        m = jnp.einsum('ljgnp,gh->ljgnhp', m, eye)
        return m.reshape(depth, S5_BLOCKS, gpb * S5_GROUP, gpb * S5_STATE).astype(MXU_DTYPE)

    lam = lambda v: v.reshape(n, S5_GROUP, S5_STATE)[:, 0].reshape(depth, 1, S5_N)
    return lam(lr), lam(li), blocks_in(bbr), blocks_in(bbi)


def _s5_out_blocks(c):
    depth = c.shape[0]
    gpb = S5_GROUPS // S5_BLOCKS
    eye = jnp.eye(gpb, dtype=F32)
    m = jnp.einsum('ljgnp,gh->ljgphn', c.reshape(depth, S5_BLOCKS, gpb, S5_GROUP, S5_STATE), eye)
    return m.reshape(depth, S5_BLOCKS, gpb * S5_STATE, gpb * S5_GROUP).astype(MXU_DTYPE)


def _s5_in(u, bb_ref):
    cw, sw = BRANCH_W // S5_BLOCKS, S5_N // S5_BLOCKS
    return [_mm(u[:, j * cw:(j + 1) * cw], bb_ref[j]) for j in range(S5_BLOCKS)], sw


def _s5_out(hr, hi, u, cr_ref, ci_ref, d_ref, wglu_ref, bglu_ref):
    sw = S5_N // S5_BLOCKS
    y = jnp.concatenate([_mm(hr[:, j * sw:(j + 1) * sw], cr_ref[j]) - _mm(hi[:, j * sw:(j + 1) * sw], ci_ref[j])
                         for j in range(S5_BLOCKS)], axis=1)
    y = _gelu_tanh(y + d_ref[...] * u)
    return y * jax.nn.sigmoid(_mm(y, wglu_ref[...]) + bglu_ref[...])


def _scan_tiles(n_tiles, cw, width, tile_fn, load_state, store_state):
    for c in range(width // cw):
        cs = pl.ds(c * cw, cw)

        def body(k, carry, cs=cs):
            return tile_fn(pl.ds(pl.multiple_of(k * SUBLANE, SUBLANE), SUBLANE), cs, carry)

        store_state(cs, lax.fori_loop(0, n_tiles, body, load_state(cs)))


def _split3(x):
    hi = x.astype(MXU_DTYPE)
    r1 = x - hi.astype(F32)
    mid = r1.astype(MXU_DTYPE)
    return hi, mid, (r1 - mid.astype(F32)).astype(MXU_DTYPE)


def _s5_seq_kernel(u_ref, bbr_ref, bbi_ref, lr_ref, li_ref, cr_ref, ci_ref, d_ref, wglu_ref, bglu_ref,
                   y_ref, hlr_ref, hli_ref, xr_s, xi_s, cr_s, ci_s, pw_s):
    c = pl.program_id(1)
    lc = u_ref.shape[0]
    seg = lc // SUBLANE
    assert seg & (seg - 1) == 0

    @pl.when(c == 0)
    def _():
        cr_s[...] = jnp.zeros_like(cr_s)
        ci_s[...] = jnp.zeros_like(ci_s)
        lr, li = lr_ref[...], li_ref[...]
        for _ in range(seg.bit_length() - 1):
            lr, li = lr * lr - li * li, 2.0 * lr * li
        row = _row_iota((SUBLANE, S5_N))
        pr, pi = lr, li
        acc_r = jnp.zeros((SUBLANE, S5_N), F32)
        acc_i = jnp.zeros((SUBLANE, S5_N), F32)
        for r in range(SUBLANE):
            acc_r = jnp.where(row == r, pr, acc_r)
            acc_i = jnp.where(row == r, pi, acc_i)
            if r + 1 in (1, 2, 4):
                k = (1, 2, 4).index(r + 1)
                pw_s[2 * k] = jnp.where(row >= r + 1, pr, 0.0)
                pw_s[2 * k + 1] = jnp.where(row >= r + 1, pi, 0.0)
            pr, pi = pr * lr - pi * li, pr * li + pi * lr
        pw_s[6] = acc_r
        pw_s[7] = acc_i

    n_i = lax.broadcasted_iota(jnp.int32, (lc, lc), 0)
    t_i = lax.broadcasted_iota(jnp.int32, (lc, lc), 1)
    sh = seg.bit_length() - 1
    perm = (t_i == (n_i & (SUBLANE - 1)) * seg + lax.shift_right_logical(n_i, 3)).astype(MXU_DTYPE)
    unperm = (t_i == (n_i & (seg - 1)) * SUBLANE + lax.shift_right_logical(n_i, sh)).astype(MXU_DTYPE)

    u = sum(_mm(perm, t) for t in _split3(u_ref[...]))
    xr, sw = _s5_in(u, bbr_ref)
    xi, _ = _s5_in(u, bbi_ref)
    for j in range(S5_BLOCKS):
        xr_s[:, pl.ds(j * sw, sw)] = xr[j]
        xi_s[:, pl.ds(j * sw, sw)] = xi[j]

    cw = 512
    for cc in range(S5_N // cw):
        cs = pl.ds(cc * cw, cw)
        lr, li = lr_ref[:, cs], li_ref[:, cs]

        def tile_rows(k):
            return pl.ds(pl.multiple_of(k * SUBLANE, SUBLANE), SUBLANE)

        def pass1(k, carry, cs=cs, lr=lr, li=li):
            gr, gi = carry
            rows = tile_rows(k)
            gr, gi = lr * gr - li * gi + xr_s[rows, cs], lr * gi + li * gr + xi_s[rows, cs]
            xr_s[rows, cs] = gr
            xi_s[rows, cs] = gi
            return gr, gi

        zero = jnp.zeros((SUBLANE, cw), F32)
        sr, si = lax.fori_loop(0, seg, pass1, (zero, zero))

        for k, s in enumerate((1, 2, 4)):
            ar, ai = pw_s[2 * k, :, cs], pw_s[2 * k + 1, :, cs]
            qr, qi = pltpu.roll(sr, s, 0), pltpu.roll(si, s, 0)
            sr, si = sr + ar * qr - ai * qi, si + ar * qi + ai * qr
        ar, ai = pw_s[6, :, cs], pw_s[7, :, cs]
        c0r, c0i = cr_s[:, cs], ci_s[:, cs]
        sr, si = sr + ar * c0r - ai * c0i, si + ar * c0i + ai * c0r
        first = _row_iota((SUBLANE, cw)) == 0
        jr = jnp.where(first, c0r, pltpu.roll(sr, 1, 0))
        ji = jnp.where(first, c0i, pltpu.roll(si, 1, 0))
        cr_s[:, cs] = _last_row(sr)
        ci_s[:, cs] = _last_row(si)

        def pass2(k, carry, cs=cs, lr=lr, li=li):
            jr, ji = carry
            rows = tile_rows(k)
            jr, ji = lr * jr - li * ji, lr * ji + li * jr
            xr_s[rows, cs] = xr_s[rows, cs] + jr
            xi_s[rows, cs] = xi_s[rows, cs] + ji
            return jr, ji

        lax.fori_loop(0, seg, pass2, (jr, ji))

    y = _s5_out(xr_s[...], xi_s[...], u, cr_ref, ci_ref, d_ref, wglu_ref, bglu_ref).astype(y_ref.dtype)
    y_ref[...] = _mm(unperm, y).astype(y_ref.dtype)

    @pl.when(c == pl.num_programs(1) - 1)
    def _():
        hlr_ref[...] = cr_s[pl.ds(0, 1), :]
        hli_ref[...] = ci_s[pl.ds(0, 1), :]


def _s5_step_kernel(u_ref, h0r_ref, h0i_ref, bbr_ref, bbi_ref, lr_ref, li_ref, cr_ref, ci_ref, d_ref, wglu_ref,
                    bglu_ref, y_ref, hlr_ref, hli_ref):
    u = u_ref[...]
    xr, _ = _s5_in(u, bbr_ref)
    xi, _ = _s5_in(u, bbi_ref)
    lr, li, sr, si = lr_ref[...], li_ref[...], h0r_ref[...], h0i_ref[...]
    hr = lr * sr - li * si + jnp.concatenate(xr, axis=1)
    hi = lr * si + li * sr + jnp.concatenate(xi, axis=1)
    hlr_ref[...] = hr
    hli_ref[...] = hi
    y_ref[...] = _s5_out(hr, hi, u, cr_ref, ci_ref, d_ref, wglu_ref, bglu_ref).astype(y_ref.dtype)


def _s5(g, uq, h0r, h0i, disc, cr, ci, d, w_glu, b_glu, layer):
    lam_r, lam_i, bbr, bbi = disc
    cw, sw = BRANCH_W // S5_BLOCKS, S5_N // S5_BLOCKS
    sds = jax.ShapeDtypeStruct
    seq = g.L > 1
    lc = min(g.L, SEQ_CHUNK) if seq else g.B
    nc = g.L // lc if seq else 1
    grid = (g.B, nc) if seq else (1, 1)
    row = (lambda b, c: (b * nc + c, 0))
    lay3 = lambda b, c: (layer, 0, 0)
    lay4 = lambda b, c: (layer, 0, 0, 0)
    w_specs = [pl.BlockSpec((None, S5_BLOCKS, cw, sw), lay4), pl.BlockSpec((None, S5_BLOCKS, cw, sw), lay4),
               pl.BlockSpec((None, 1, S5_N), lay3), pl.BlockSpec((None, 1, S5_N), lay3),
               pl.BlockSpec((None, S5_BLOCKS, sw, cw), lay4), pl.BlockSpec((None, S5_BLOCKS, sw, cw), lay4),
               pl.BlockSpec((None, 1, BRANCH_W), lay3),
               pl.BlockSpec((None, BRANCH_W, BRANCH_W), lay3),
               pl.BlockSpec((None, 1, BRANCH_W), lay3)]
    w_args = (bbr, bbi, lam_r, lam_i, cr, ci, d.reshape(-1, 1, BRANCH_W), w_glu, b_glu.reshape(-1, 1, BRANCH_W))
    if seq:
        st_spec = pl.BlockSpec((None, 1, S5_N), lambda b, c: (b, 0, 0))
        y, hlr, hli = pl.pallas_call(
            _s5_seq_kernel,
            grid=grid,
            in_specs=[pl.BlockSpec((lc, BRANCH_W), row)] + w_specs,
            out_specs=(pl.BlockSpec((lc, BRANCH_W), row), st_spec, st_spec),
            out_shape=(sds((g.T, BRANCH_W), MXU_DTYPE), sds((g.B, 1, S5_N), F32), sds((g.B, 1, S5_N), F32)),
            scratch_shapes=[pltpu.VMEM((lc, S5_N), F32), pltpu.VMEM((lc, S5_N), F32),
                            pltpu.VMEM((SUBLANE, S5_N), F32), pltpu.VMEM((SUBLANE, S5_N), F32),
                            pltpu.VMEM((8, SUBLANE, S5_N), F32)],
            compiler_params=_params("arbitrary", "arbitrary"),
            name="s5_ssm",
        )(uq, *w_args)
        return y, hlr.reshape(g.B, S5_N), hli.reshape(g.B, S5_N)
    st_spec = pl.BlockSpec((g.B, S5_N), lambda b, c: (0, 0))
    return pl.pallas_call(
        _s5_step_kernel,
        grid=grid,
        in_specs=[pl.BlockSpec((lc, BRANCH_W), row), st_spec, st_spec] + w_specs,
        out_specs=(pl.BlockSpec((lc, BRANCH_W), row), st_spec, st_spec),
        out_shape=(sds((g.T, BRANCH_W), MXU_DTYPE), sds((g.B, S5_N), F32), sds((g.B, S5_N), F32)),
        compiler_params=_params("arbitrary", "arbitrary"),
        name="s5_ssm",
    )(uq, h0r, h0i, *w_args)


def _lru_gates(xc, wa_ref, ba_ref, wx_ref, bx_ref, lam_ref):
    r = jax.nn.sigmoid(_mm(xc, wa_ref[...]) + ba_ref[...])
    ig = jax.nn.sigmoid(_mm(xc, wx_ref[...]) + bx_ref[...])
    log_a = -LRU_C * r * _softplus(-lam_ref[...])
    return jnp.exp(log_a), jnp.sqrt(1.0 - jnp.exp(2.0 * log_a)), ig


def _lru_seq_kernel(x_ref, g_ref, cw_ref, cbias_ref, wa_ref, ba_ref, wx_ref, bx_ref, lam_ref,
                    y_ref, hl_ref, xbuf, a_s, b_s, h_s):
    c = pl.program_id(1)
    lc = x_ref.shape[0]
    pad = SUBLANE

    @pl.when(c == 0)
    def _():
        xbuf[pl.ds(0, pad), :] = jnp.zeros((pad, LRU_W), F32)
        h_s[...] = jnp.zeros_like(h_s)

    xbuf[pl.ds(pad, lc), :] = x_ref[...]
    xc = cbias_ref[...]
    for j in range(CONV_W):
        xc = xc + cw_ref[pl.ds(j, 1), :] * xbuf[pl.ds(pad - (CONV_W - 1) + j, lc), :]
    new_tail = xbuf[pl.ds(lc, pad), :]
    xbuf[pl.ds(0, pad), :] = new_tail

    a, mult, ig = _lru_gates(xc, wa_ref, ba_ref, wx_ref, bx_ref, lam_ref)
    first = jnp.logical_and(c == 0, _row_iota((lc, LRU_W)) == 0)
    a_s[...] = jnp.where(first, 0.0, a)
    b_s[...] = jnp.where(first, 1.0, mult) * (ig * xc)

    def tile(rows, cs, h0):
        a, b = a_s[rows, cs], b_s[rows, cs]
        row = _row_iota(a.shape)
        for s in (1, 2, 4):
            keep = row >= s
            b = a * jnp.where(keep, pltpu.roll(b, s, 0), 0.0) + b
            a = a * jnp.where(keep, pltpu.roll(a, s, 0), 1.0)
        h = a * h0 + b
        b_s[rows, cs] = h
        return _last_row(h)

    def store(cs, h):
        h_s[:, cs] = h

    _scan_tiles(lc // SUBLANE, LRU_W, LRU_W, tile, lambda cs: h_s[:, cs], store)
    y_ref[...] = (b_s[...] * _gelu_tanh(g_ref[...])).astype(y_ref.dtype)

    @pl.when(c == pl.num_programs(1) - 1)
    def _():
        hl_ref[...] = h_s[pl.ds(0, 1), :]


def _lru_step_kernel(x_ref, g_ref, cs_ref, h0_ref, cw_ref, cbias_ref, wa_ref, ba_ref, wx_ref, bx_ref, lam_ref,
                     y_ref, hl_ref):
    x = x_ref[...]
    xc = cbias_ref[...] + cw_ref[pl.ds(CONV_W - 1, 1), :] * x
    for j in range(CONV_W - 1):
        xc = xc + cw_ref[pl.ds(j, 1), :] * cs_ref[j]
    a, mult, ig = _lru_gates(xc, wa_ref, ba_ref, wx_ref, bx_ref, lam_ref)
    h = a * h0_ref[...] + mult * (ig * xc)
    hl_ref[...] = h
    y_ref[...] = (h * _gelu_tanh(g_ref[...])).astype(y_ref.dtype)


def _block_diag(w):
    depth, nb, k, _ = w.shape
    eye = jnp.eye(nb, dtype=w.dtype)
    return jnp.einsum('lnwv,nm->lnwmv', w, eye).reshape(depth, nb * k, nb * k).astype(MXU_DTYPE)


def _lru(g, rest, conv_state, h0, p, layer, pos0):
    sds = jax.ShapeDtypeStruct
    seq = g.L > 1
    lc = min(g.L, SEQ_CHUNK) if seq else g.B
    nc = g.L // lc if seq else 1
    lay3 = lambda b, c: (layer, 0, 0)
    vec = lambda v: v.reshape(-1, 1, LRU_W)
    w_specs = [pl.BlockSpec((None, CONV_W, LRU_W), lay3), pl.BlockSpec((None, 1, LRU_W), lay3),
               pl.BlockSpec((None, LRU_W, LRU_W), lay3), pl.BlockSpec((None, 1, LRU_W), lay3),
               pl.BlockSpec((None, LRU_W, LRU_W), lay3), pl.BlockSpec((None, 1, LRU_W), lay3),
               pl.BlockSpec((None, 1, LRU_W), lay3)]
    w_args = (p['lru_conv_w'], vec(p['lru_conv_b']), p['lru_wa_bd'], vec(p['lru_b_a']), p['lru_wx_bd'],
              vec(p['lru_b_x']), vec(p['lru_lambda']))
    x_spec = pl.BlockSpec((lc, LRU_W), lambda b, c: (b * nc + c, 1))
    g_spec = pl.BlockSpec((lc, LRU_W), lambda b, c: (b * nc + c, 2))
    y_spec = pl.BlockSpec((lc, LRU_W), lambda b, c: (b * nc + c, 0))
    if seq:
        assert pos0 == 0
        y, hl = pl.pallas_call(
            _lru_seq_kernel,
            grid=(g.B, nc),
            in_specs=[x_spec, g_spec] + w_specs,
            out_specs=(y_spec, pl.BlockSpec((None, 1, LRU_W), lambda b, c: (b, 0, 0))),
            out_shape=(sds((g.T, LRU_W), MXU_DTYPE), sds((g.B, 1, LRU_W), F32)),
            scratch_shapes=[pltpu.VMEM((lc + SUBLANE, LRU_W), F32), pltpu.VMEM((lc, LRU_W), F32),
                            pltpu.VMEM((lc, LRU_W), F32), pltpu.VMEM((SUBLANE, LRU_W), F32)],
            compiler_params=_params("arbitrary", "arbitrary"),
            name="rg_lru",
        )(rest, rest, *w_args)
        return y, hl.reshape(g.B, LRU_W)
    assert pos0 > 0
    st_spec = pl.BlockSpec((g.B, LRU_W), lambda b, c: (0, 0))
    return pl.pallas_call(
        _lru_step_kernel,
        grid=(1, 1),
        in_specs=[x_spec, g_spec, pl.BlockSpec((CONV_W - 1, g.B, LRU_W), lambda b, c: (0, 0, 0)), st_spec] + w_specs,
        out_specs=(y_spec, st_spec),
        out_shape=(sds((g.T, LRU_W), MXU_DTYPE), sds((g.B, LRU_W), F32)),
        compiler_params=_params("arbitrary", "arbitrary"),
        name="rg_lru",
    )(rest, rest, jnp.swapaxes(conv_state, 0, 1), h0, *w_args)


def _l2norm(x):
    return x * lax.rsqrt(jnp.sum(x * x, axis=-1, keepdims=True) + NORM_EPS)


def _dn_gates(ab, prm):
    g = -jnp.exp(prm[0:1, :]) * _softplus(ab + prm[1:2, :])
    return g, jax.nn.sigmoid(ab)


def _dn_seq_kernel(q_ref, k_ref, v_ref, z_ref, ab_ref, cw_ref, prm_ref, ng_ref, y_ref, sfin_ref, xbuf, s_s):
    c = pl.program_id(0)
    bsz, cs, _ = q_ref.shape
    hd = DN_HEAD_DIM
    rows = bsz * cs
    pad = SUBLANE

    @pl.when(c == 0)
    def _():
        xbuf[:, pl.ds(0, pad), :] = jnp.zeros((bsz, pad, 3 * DN_W), F32)
        s_s[...] = jnp.zeros_like(s_s)

    for s, ref in enumerate((q_ref, k_ref, v_ref)):
        xbuf[:, pl.ds(pad, cs), pl.ds(s * DN_W, DN_W)] = ref[...]
    conv = None
    for j in range(CONV_W):
        t = cw_ref[pl.ds(j, 1), :] * xbuf[:, pl.ds(pad - (CONV_W - 1) + j, cs), :]
        conv = t if conv is None else conv + t
    new_tail = xbuf[:, pl.ds(cs, pad), :]
    xbuf[:, pl.ds(0, pad), :] = new_tail
    act = _silu(conv).reshape(rows, 3 * DN_W)

    glog, beta_all = _dn_gates(ab_ref[...].reshape(rows, LANE), prm_ref[...])
    row = lax.broadcasted_iota(jnp.int32, (rows, rows), 0)
    col = lax.broadcasted_iota(jnp.int32, (rows, rows), 1)
    shift = cs.bit_length() - 1
    same = lax.shift_right_logical(row, shift) == lax.shift_right_logical(col, shift)
    causal = jnp.logical_and(same, row >= col)
    strict = jnp.logical_and(same, row > col)
    gc = jnp.dot(causal.astype(F32), glog, precision=lax.Precision.HIGHEST, preferred_element_type=F32)
    gct = gc.T
    eye = (row == col).astype(F32)
    ng = ng_ref[...]
    rblk = lax.shift_right_logical(lax.broadcasted_iota(jnp.int32, (rows, bsz * hd), 0), shift)
    cblk = lax.shift_right_logical(lax.broadcasted_iota(jnp.int32, (rows, bsz * hd), 1), hd.bit_length() - 1)
    own = rblk == cblk

    def diag_blocks(m):
        return jnp.concatenate([m[b * cs:(b + 1) * cs, b * hd:(b + 1) * hd] for b in range(bsz)], axis=0)

    for h in range(DN_HEADS):
        q = _l2norm(act[:, h * hd:(h + 1) * hd]) * (hd ** -0.5)
        k = _l2norm(act[:, DN_W + h * hd:DN_W + (h + 1) * hd])
        v = act[:, 2 * DN_W + h * hd:2 * DN_W + (h + 1) * hd]
        beta = beta_all[:, DN_HEADS + h:DN_HEADS + h + 1]
        gcol = gc[:, h:h + 1]
        decay = jnp.exp(jnp.where(causal, gcol - gct[h:h + 1, :], -jnp.inf))
        kb = k * beta
        kq = _mm_nt(jnp.concatenate([kb, q], axis=0), k)
        nmat = jnp.where(strict, -(kq[:rows] * decay), 0.0)
        att = jnp.where(causal, kq[rows:] * decay, 0.0)
        rinv = eye + nmat
        xpow = _mm(nmat, nmat)
        for _ in range(cs.bit_length() - 3):
            st = _mm(jnp.concatenate([xpow, rinv], axis=0), xpow)
            xpow, rinv = st[:rows], rinv + st[rows:]
        rinv = rinv + _mm(rinv, xpow)
        eg = jnp.exp(gcol)
        uw = _mm(rinv, jnp.concatenate([v * beta, kb * eg], axis=1))
        u, w = uw[:, :hd], uw[:, hd:]
        s = s_s[h]
        ws_qs = _mm(jnp.concatenate([w, q * eg], axis=0), s)
        v_new = u - diag_blocks(ws_qs[:rows])
        o = diag_blocks(ws_qs[rows:]) + _mm(att, v_new)
        glast = [gc[(b + 1) * cs - 1:(b + 1) * cs, h:h + 1] for b in range(bsz)]
        glast_rows = jnp.concatenate([jnp.broadcast_to(t, (cs, 1)) for t in glast], axis=0)
        glast_lanes = jnp.concatenate([jnp.broadcast_to(t, (1, hd)) for t in glast], axis=1)
        v_bd = jnp.where(own, jnp.concatenate([v_new] * bsz, axis=1), 0.0)
        s_s[h] = s * jnp.exp(glast_lanes) + _mm_tn(k * jnp.exp(glast_rows - gcol), v_bd)
        o = _rms(o, ng) * _silu(z_ref[:, :, pl.ds(h * hd, hd)].reshape(rows, hd))
        y_ref[:, :, pl.ds(h * hd, hd)] = o.reshape(bsz, cs, hd).astype(y_ref.dtype)

    @pl.when(c == pl.num_programs(0) - 1)
    def _():
        for b in range(bsz):
            for h in range(DN_HEADS):
                sfin_ref[b, h] = s_s[h, :, pl.ds(b * hd, hd)]


def _dn_seq(g, uq, rest, ab, conv_w, prm, norm_g, layer):
    cs = DN_CHUNK
    assert cs & (cs - 1) == 0 and g.L % cs == 0
    nc = g.L // cs
    hd = DN_HEAD_DIM
    lay3 = lambda c: (layer, 0, 0)
    sds = jax.ShapeDtypeStruct
    sec = lambda s: pl.BlockSpec((g.B, cs, DN_W), lambda c: (0, c, s))
    y, sfin = pl.pallas_call(
        _dn_seq_kernel,
        grid=(nc,),
        in_specs=[sec(1), sec(2), sec(3), sec(0),
                  pl.BlockSpec((g.B, cs, LANE), lambda c: (0, c, 0)),
                  pl.BlockSpec((None, CONV_W, 3 * DN_W), lay3),
                  pl.BlockSpec((None, 2, LANE), lay3),
                  pl.BlockSpec((None, 1, hd), lay3)],
        out_specs=(pl.BlockSpec((g.B, cs, DN_W), lambda c: (0, c, 0)),
                   pl.BlockSpec((g.B, DN_HEADS, hd, hd), lambda c: (0, 0, 0, 0))),
        out_shape=(sds((g.B, g.L, DN_W), MXU_DTYPE), sds((g.B, DN_HEADS, hd, hd), F32)),
        scratch_shapes=[pltpu.VMEM((g.B, cs + SUBLANE, 3 * DN_W), F32),
                        pltpu.VMEM((DN_HEADS, hd, g.B * hd), F32)],
        compiler_params=_params("arbitrary"),
        name="deltanet_seq",
    )(*(3 * [uq.reshape(g.B, g.L, -1)]), rest.reshape(g.B, g.L, -1), ab.reshape(g.B, g.L, LANE), conv_w, prm,
      norm_g.reshape(-1, 1, hd))
    return y.reshape(g.T, DN_W), sfin


def _dn_step_kernel(q_ref, k_ref, v_ref, cq_ref, ck_ref, cv_ref, z_ref, ab_ref, cwq_ref, cwk_ref, cwv_ref,
                    prm_ref, ng_ref, s_ref, y_ref, so_ref, o_s, *, bsz):
    h = pl.program_id(0)
    hd = DN_HEAD_DIM

    def conv(x_ref, c_ref, w_ref):
        acc = w_ref[pl.ds(CONV_W - 1, 1), :] * x_ref[...]
        for j in range(CONV_W - 1):
            acc = acc + w_ref[pl.ds(j, 1), :] * c_ref[j]
        return _silu(acc)

    q = _l2norm(conv(q_ref, cq_ref, cwq_ref)) * (hd ** -0.5)
    k = _l2norm(conv(k_ref, ck_ref, cwk_ref))
    v = conv(v_ref, cv_ref, cwv_ref)
    glog, beta_all = _dn_gates(ab_ref[...], prm_ref[...])
    lane = lax.broadcasted_iota(jnp.int32, (bsz, LANE), 1)
    eg = jnp.exp(jnp.sum(jnp.where(lane == h, glog, 0.0), axis=-1, keepdims=True))
    beta = jnp.sum(jnp.where(lane == DN_HEADS + h, beta_all, 0.0), axis=-1, keepdims=True)
    att = jnp.sum(q * k, axis=-1, keepdims=True)
    kt = k.T
    qt = q.T
    for b in range(bsz):
        s = s_ref[b]
        kcol = kt[:, b:b + 1]
        ks = jnp.sum(s * kcol, axis=0, keepdims=True)
        qs = jnp.sum(s * qt[:, b:b + 1], axis=0, keepdims=True)
        eg_b, beta_b = eg[b:b + 1, :], beta[b:b + 1, :]
        v_new = beta_b * v[b:b + 1, :] - (beta_b * eg_b) * ks
        o_s[pl.ds(b, 1), :] = eg_b * qs + att[b:b + 1, :] * v_new
        so_ref[b] = s * eg_b + kcol * v_new
    y_ref[...] = (_rms(o_s[...], ng_ref[...]) * _silu(z_ref[...])).astype(y_ref.dtype)


def _dn_step(g, uq, conv_state, rest, ab, conv_w, prm, norm_g, s0_all, layer):
    bsz = g.B
    hd = DN_HEAD_DIM
    conv_t = jnp.swapaxes(conv_state, 0, 1)
    sec = lambda s: (lambda h: (0, (s + 1) * DN_HEADS + h))
    sec3 = lambda s: (lambda h: (0, 0, s * DN_HEADS + h))
    secw = lambda s: (lambda h: (layer, 0, s * DN_HEADS + h))
    lay3 = lambda h: (layer, 0, 0)
    sds = jax.ShapeDtypeStruct
    return pl.pallas_call(
        functools.partial(_dn_step_kernel, bsz=bsz),
        grid=(DN_HEADS,),
        in_specs=[pl.BlockSpec((bsz, hd), sec(s)) for s in range(3)]
                 + [pl.BlockSpec((CONV_W - 1, bsz, hd), sec3(s)) for s in range(3)]
                 + [pl.BlockSpec((bsz, hd), lambda h: (0, h)),
                    pl.BlockSpec((bsz, LANE), lambda h: (0, 0))]
                 + [pl.BlockSpec((None, CONV_W, hd), secw(s)) for s in range(3)]
                 + [pl.BlockSpec((None, 2, LANE), lay3),
                    pl.BlockSpec((None, 1, hd), lay3),
                    pl.BlockSpec((None, bsz, None, hd, hd), lambda h: (layer, 0, h, 0, 0))],
        out_specs=(pl.BlockSpec((bsz, hd), lambda h: (0, h)),
                   pl.BlockSpec((bsz, None, hd, hd), lambda h: (0, h, 0, 0))),
        out_shape=(sds((bsz, DN_W), MXU_DTYPE), sds((bsz, DN_HEADS, hd, hd), F32)),
        scratch_shapes=[pltpu.VMEM((bsz, hd), F32)],
        compiler_params=_params("arbitrary"),
        name="deltanet_step",
    )(uq, uq, uq, conv_t, conv_t, conv_t, rest, ab, conv_w, conv_w, conv_w, prm,
      norm_g.reshape(-1, 1, hd), s0_all)


def _pack_pairs(h):
    half = h.shape[1] // 2
    hb = h.astype(MXU_DTYPE).astype(F32)
    lo = lax.shift_right_logical(pltpu.bitcast(hb[:, :half], jnp.uint32), jnp.uint32(16))
    return lo | (pltpu.bitcast(hb[:, half:], jnp.uint32) & jnp.uint32(0xFFFF0000))


def _unpack_pairs(w):
    lo = pltpu.bitcast(lax.shift_left(w, jnp.uint32(16)), F32)
    hi = pltpu.bitcast(w & jnp.uint32(0xFFFF0000), F32)
    return jnp.concatenate([lo, hi], axis=1).astype(MXU_DTYPE)


def _router_kernel(xp_ref, xs_ref, g_ref, scp_ref, shp_ref, scs_ref, shs_ref, wr_ref, br_ref, h_ref, r_ref, *, n_p):
    is_s = pl.program_id(0) >= n_p
    x = jnp.where(is_s, xs_ref[...], xp_ref[...])
    sc = jnp.where(is_s, scs_ref[...], scp_ref[...])
    sh = jnp.where(is_s, shs_ref[...], shp_ref[...])
    h = _rms(x, g_ref[...]) * (1.0 + sc) + sh
    h_ref[...] = _pack_pairs(h)
    logits = _mm(h, wr_ref[...]) + br_ref[...]
    lane = lax.broadcasted_iota(jnp.int32, logits.shape, 1).astype(F32)
    ninf = -jnp.inf
    big = float(LANE)
    lg = jnp.where(lane < N_GROUPS, logits, ninf)
    gmax = jnp.max(lg, axis=-1, keepdims=True)
    gsel = jnp.min(jnp.where(lg == gmax, lane, big), axis=-1, keepdims=True)
    pg = 1.0 / jnp.sum(jnp.where(lane < N_GROUPS, jnp.exp(logits - gmax), 0.0), axis=-1, keepdims=True)
    lo = N_GROUPS + EXPERTS_PER_GROUP * gsel
    le = jnp.where(jnp.logical_and(lane >= lo, lane < lo + EXPERTS_PER_GROUP), logits, ninf)
    v1 = jnp.max(le, axis=-1, keepdims=True)
    i1 = jnp.min(jnp.where(le == v1, lane, big), axis=-1, keepdims=True)
    le2 = jnp.where(lane == i1, ninf, le)
    v2 = jnp.max(le2, axis=-1, keepdims=True)
    i2 = jnp.min(jnp.where(le2 == v2, lane, big), axis=-1, keepdims=True)
    e2 = jnp.exp(v2 - v1)
    w1 = pg / (1.0 + e2)
    w2 = pg * e2 / (1.0 + e2)
    r_ref[...] = jnp.where(lane == 0, i1 - N_GROUPS,
                           jnp.where(lane == 1, i2 - N_GROUPS,
                                     jnp.where(lane == 2, w1, jnp.where(lane == 3, w2, 0.0))))


def _router(groups, xs, gain, mods, w_route, b_route, layer):
    gp, gs = groups
    tm = min(ROUTER_ROWS, gs.T)
    assert gp.L % tm == 0 and gs.T % tm == 0
    n_p, n_s, nt = gp.T // tm, gs.T // tm, gp.L // tm
    t_all = gp.T + gs.T
    p_row = lambda i, j: (jnp.minimum(i, n_p - 1), 0)
    s_row = lambda i, j: (jnp.maximum(i - n_p, 0), 0)
    p_mod = lambda k: pl.BlockSpec((None, 1, D_MODEL), lambda i, j: (jnp.minimum(i, n_p - 1) // nt, 0, k))
    s_mod = lambda k: pl.BlockSpec((tm, D_MODEL), lambda i, j: (jnp.maximum(i - n_p, 0), k))
    sds = jax.ShapeDtypeStruct
    return pl.pallas_call(
        functools.partial(_router_kernel, n_p=n_p),
        grid=(n_p + n_s, 1),
        in_specs=[pl.BlockSpec((tm, D_MODEL), p_row), pl.BlockSpec((tm, D_MODEL), s_row),
                  pl.BlockSpec((1, D_MODEL), lambda i, j: (0, 0)),
                  p_mod(4), p_mod(3), s_mod(4), s_mod(3),
                  pl.BlockSpec((None, D_MODEL, LANE), lambda i, j: (layer, 0, 0)),
                  pl.BlockSpec((None, 1, LANE), lambda i, j: (layer, 0, 0))],
        out_specs=(pl.BlockSpec((tm, D_MODEL // 2), lambda i, j: (i, 0)), pl.BlockSpec((tm, LANE), lambda i, j: (i, 0))),
        out_shape=(sds((t_all, D_MODEL // 2), jnp.uint32), sds((t_all, LANE), F32)),
        compiler_params=_params("arbitrary", "arbitrary"),
        name="moe_router",
    )(xs[0], xs[1], gain.reshape(1, D_MODEL), mods[0], mods[0], mods[1], mods[1], w_route, b_route)


def _row_copy(src_hbm, row, dst, dst_row, sem):
    return pltpu.make_async_copy(src_hbm.at[pl.ds(row, 1)], dst.at[pl.ds(dst_row, 1)], sem)


def _experts_kernel(be_ref, tok_ref, nact_ref, first_ref, nxt_ref, par_ref, h_hbm, wg_hbm, wu_hbm, wd_hbm, y_ref,
                    xbuf, gsem, wst_g, wst_u, wst_d, wsem, wgt, wut, wdt, *, layer):
    b = pl.program_id(0)
    nact = nact_ref[0]
    rb = MOE_ROWS

    def gather(blk, slot):
        for r in range(rb):
            _row_copy(h_hbm, tok_ref[blk * rb + r], xbuf.at[slot], r, gsem.at[slot]).start(priority=1)

    def weight_copies(e, slot):
        return (pltpu.make_async_copy(wg_hbm.at[layer, e], wst_g.at[slot], wsem.at[slot]),
                pltpu.make_async_copy(wu_hbm.at[layer, e], wst_u.at[slot], wsem.at[slot]),
                pltpu.make_async_copy(wd_hbm.at[layer, e], wst_d.at[slot], wsem.at[slot]))

    @pl.when(b == 0)
    def _():
        for cp in weight_copies(be_ref[0], par_ref[0]):
            cp.start()
        gather(0, 0)

    @pl.when(b + 1 < nact)
    def _():
        gather(b + 1, (b + 1) % 2)

    @pl.when(jnp.logical_and(b < nact, first_ref[b] == 1))
    def _():
        slot = par_ref[b]
        for cp in weight_copies(be_ref[b], slot):
            cp.wait()

        @pl.when(nxt_ref[b] >= 0)
        def _():
            for cp in weight_copies(nxt_ref[b], 1 - slot):
                cp.start()

        ck = XPOSE_CHUNK
        for j in range(D_MODEL // ck):
            wgt[:, pl.ds(j * ck, ck)] = wst_g[slot, pl.ds(j * ck, ck), :].T.astype(MXU_DTYPE)
            wut[:, pl.ds(j * ck, ck)] = wst_u[slot, pl.ds(j * ck, ck), :].T.astype(MXU_DTYPE)
            wdt[pl.ds(j * ck, ck), :] = wst_d[slot, :, pl.ds(j * ck, ck)].T.astype(MXU_DTYPE)

    @pl.when(b < nact)
    def _():
        slot = b % 2
        for r in range(rb):
            _row_copy(h_hbm, 0, xbuf.at[slot], r, gsem.at[slot]).wait()
        x = _unpack_pairs(xbuf[slot])
        hid_t = _silu(_mm_nt(wgt[...], x)) * _mm_nt(wut[...], x)
        y_ref[...] = _pack_pairs(_mm(wdt[...], hid_t).T)

    @pl.when(b >= nact)
    def _():
        y_ref[...] = jnp.zeros_like(y_ref)


def _experts(h_all, plan, w_gate, w_up, w_down, layer):
    block_e, row_token, n_active, first, nxt, par = plan
    n_blocks = block_e.shape[0]
    rb = MOE_ROWS
    any_spec = pl.BlockSpec(memory_space=pl.ANY)
    return pl.pallas_call(
        functools.partial(_experts_kernel, layer=layer),
        grid_spec=pltpu.PrefetchScalarGridSpec(
            num_scalar_prefetch=6,
            grid=(n_blocks,),
            in_specs=[any_spec, any_spec, any_spec, any_spec],
            out_specs=pl.BlockSpec((rb, D_MODEL // 2), lambda b, *_: (b, 0)),
            scratch_shapes=[pltpu.VMEM((2, rb, D_MODEL // 2), jnp.uint32), pltpu.SemaphoreType.DMA((2,)),
                            pltpu.VMEM((2, D_MODEL, D_EXPERT), F32), pltpu.VMEM((2, D_MODEL, D_EXPERT), F32),
                            pltpu.VMEM((2, D_EXPERT, D_MODEL), F32), pltpu.SemaphoreType.DMA((2,)),
                            pltpu.VMEM((D_EXPERT, D_MODEL), MXU_DTYPE), pltpu.VMEM((D_EXPERT, D_MODEL), MXU_DTYPE),
                            pltpu.VMEM((D_MODEL, D_EXPERT), MXU_DTYPE)]),
        out_shape=jax.ShapeDtypeStruct((n_blocks * rb, D_MODEL // 2), jnp.uint32),
        compiler_params=_params("arbitrary"),
        name="moe_experts",
    )(block_e, row_token, n_active, first, nxt, par, h_all, w_gate, w_up, w_down)


def _combine_kernel(pos_ref, y_hbm, x_ref, r_ref, g_ref, o_ref, buf, sem, *, tok0, tm):
    i = pl.program_id(0)
    n = pl.num_programs(0)

    def gather(tile, slot):
        for r in range(tm):
            for k in range(TOP_K):
                p = pos_ref[(tok0 + tile * tm + r) * TOP_K + k]
                _row_copy(y_hbm, p, buf.at[slot], k * tm + r, sem.at[slot]).start()

    @pl.when(i == 0)
    def _():
        gather(0, 0)

    @pl.when(i + 1 < n)
    def _():
        gather(i + 1, (i + 1) % 2)

    slot = i % 2
    for r in range(TOP_K * tm):
        _row_copy(y_hbm, 0, buf.at[slot], r, sem.at[slot]).wait()
    route = r_ref[...]
    w1, w2 = route[:, 2:3], route[:, 3:4]
    y1, y2 = buf[slot, pl.ds(0, tm), :], buf[slot, pl.ds(tm, tm), :]
    half = D_MODEL // 2
    lo = lambda w: pltpu.bitcast(lax.shift_left(w, jnp.uint32(16)), F32)
    hi = lambda w: pltpu.bitcast(w & jnp.uint32(0xFFFF0000), F32)
    for part, cols in ((lo, pl.ds(0, half)), (hi, pl.ds(half, half))):
        o_ref[:, cols] = x_ref[:, cols] + g_ref[:, cols] * (part(y1) * w1 + part(y2) * w2)


def _combine(g, tok0, pos, y_rows, x, route_all, mod):
    tm = min(COMBINE_ROWS, g.tm)
    gc = g._replace(tm=tm)
    nt = gc.nt
    assert tok0 % tm == 0
    blk0 = tok0 // tm
    if g.L == 1:
        gate_spec = pl.BlockSpec((tm, D_MODEL), lambda i, pos: (i, 5))
    else:
        gate_spec = pl.BlockSpec((None, 1, D_MODEL), lambda i, pos: (i // nt, 0, 5))
    return pl.pallas_call(
        functools.partial(_combine_kernel, tok0=tok0, tm=tm),
        grid_spec=pltpu.PrefetchScalarGridSpec(
            num_scalar_prefetch=1,
            grid=(gc.n_tiles,),
            in_specs=[pl.BlockSpec(memory_space=pl.ANY),
                      pl.BlockSpec((tm, D_MODEL), lambda i, pos: (i, 0)),
                      pl.BlockSpec((tm, LANE), lambda i, pos: (blk0 + i, 0)),
                      gate_spec],
            out_specs=pl.BlockSpec((tm, D_MODEL), lambda i, pos: (i, 0)),
            scratch_shapes=[pltpu.VMEM((2, TOP_K * tm, D_MODEL // 2), jnp.uint32), pltpu.SemaphoreType.DMA((2,))]),
        out_shape=jax.ShapeDtypeStruct((g.T, D_MODEL), F32),
        compiler_params=_params("arbitrary"),
        name="moe_combine",
    )(pos, y_rows, x, route_all, mod)


def _dispatch_plan(route_all):
    rb = MOE_ROWS
    n_tok = route_all.shape[0]
    n_assign = n_tok * TOP_K
    n_blocks = -(-(n_assign + N_EXPERTS * (rb - 1)) // rb)
    flat_e = route_all[:, :TOP_K].astype(jnp.int32).reshape(-1)
    order = jnp.argsort(flat_e).astype(jnp.int32)
    bounds = jnp.searchsorted(flat_e[order], jnp.arange(N_EXPERTS + 1, dtype=jnp.int32)).astype(jnp.int32)
    start, counts = bounds[:-1], bounds[1:] - bounds[:-1]
    rank = jnp.argsort(order).astype(jnp.int32) - start[flat_e]
    padded = (counts + rb - 1) // rb * rb
    pad_end = jnp.cumsum(padded)
    pad_start = pad_end - padded
    pos = pad_start[flat_e] + rank
    n_active = (pad_end[-1] // rb).astype(jnp.int32)
    blk = jnp.minimum(jnp.arange(n_blocks, dtype=jnp.int32), n_active - 1)
    block_e = jnp.minimum(jnp.searchsorted(pad_end, blk * rb, side='right'), N_EXPERTS - 1).astype(jnp.int32)
    row = jnp.arange(n_blocks * rb, dtype=jnp.int32)
    row_e = jnp.repeat(block_e, rb)
    idx = row - pad_start[row_e]
    src = jnp.clip(start[row_e] + idx, 0, n_assign - 1)
    row_token = jnp.where(jnp.logical_and(idx < counts[row_e], row < pad_end[-1]), order[src] // TOP_K, 0)
    bidx = jnp.arange(n_blocks, dtype=jnp.int32)
    first = jnp.logical_and(bidx < n_active, jnp.logical_or(bidx == 0, block_e != jnp.roll(block_e, 1)))
    par = (jnp.cumsum(first.astype(jnp.int32)) - 1) % 2
    nxt_first = lax.cummin(jnp.where(first, bidx, n_blocks)[::-1])[::-1]
    nxt_idx = jnp.concatenate([nxt_first[1:], jnp.full((1,), n_blocks, jnp.int32)])
    nxt = jnp.where(nxt_idx < n_blocks, block_e[jnp.minimum(nxt_idx, n_blocks - 1)], -1)
    plan = (block_e, row_token.astype(jnp.int32), n_active.reshape(1), first.astype(jnp.int32),
            nxt.astype(jnp.int32), par.astype(jnp.int32))
    return plan, pos.astype(jnp.int32)


def _layer(layer, groups, xs, mods, states, p, state_dn):
    x1s, new_states = [], []
    for g, x, mod, st in zip(groups, xs, mods, states):
        s5_re, s5_im, dn_conv, lru_h, lru_conv = st
        single = g.L == 1
        h = _norm_mod(g, x, p['norm1_g'][layer], mod, 0, 1)
        uq = _proj(g, h, p['w_pack'], layer, 0, BRANCH_W + 3 * DN_W, 1024)
        ab = _proj(g, h, p['w_pack'], layer, COL_AB, LANE, LANE)
        rest = _proj(g, h, p['w_pack'], layer, PACK_REST, REST_GATE, 512)

        ya, n_re, n_im = _s5(g, uq, s5_re, s5_im, p['s5_disc'], p['s5_cr'], p['s5_ci'], p['s5_d'],
                             p['s5_w_glu'], p['s5_b_glu'], layer)
        if single:
            qkv = uq[:, BRANCH_W:]
            lru_x = rest[:, DN_W:DN_W + LRU_W]
            yb, n_dn = _dn_step(g, uq, dn_conv, rest, ab, p['dn_conv_w'], p['dn_prm'], p['dn_norm_g'], state_dn, layer)
            n_dn_conv = jnp.concatenate([dn_conv[:, 1:], qkv[:, None, :]], axis=1)
            n_lru_conv = jnp.concatenate([lru_conv[:, 1:], lru_x[:, None, :]], axis=1)
            yc, n_lru = _lru(g, rest, lru_conv, lru_h, p, layer, pos0=p['past_len'])
        else:
            yb, n_dn = _dn_seq(g, uq, rest, ab, p['dn_conv_w'], p['dn_prm'], p['dn_norm_g'], layer)
            tail = lambda a: a.reshape(g.B, g.L, -1)[:, g.L - (CONV_W - 1):]
            n_dn_conv = tail(uq)[:, :, BRANCH_W:]
            n_lru_conv = tail(rest)[:, :, DN_W:DN_W + LRU_W]
            yc, n_lru = _lru(g, rest, None, None, p, layer, pos0=0)

        merged = _merge(g, h, ya, yb, yc, p['w_pack'], p['w_branch'], layer)
        x1 = _out_proj(g, merged, p['w_out'], layer, x, mod, 2)
        x1s.append(x1)
        new_states.append((n_re.reshape(g.B, S5_GROUPS, S5_STATE), n_im.reshape(g.B, S5_GROUPS, S5_STATE),
                           n_dn, n_dn_conv, n_lru, n_lru_conv))

    h_all, route_all = _router(groups, x1s, p['norm2_g'][layer], mods, p['w_route'], p['b_route'], layer)
    plan, pos = _dispatch_plan(route_all)
    y_rows = _experts(h_all, plan, p['w_e_gate'], p['w_e_up'], p['w_e_down'], layer)
    outs, tok0 = [], 0
    for g, x1, mod in zip(groups, x1s, mods):
        outs.append(_combine(g, tok0, pos, y_rows, x1, route_all, mod))
        tok0 += g.T
    return outs, new_states


def kernel(x_prompt, x_sample, c_prompt, c_sample, state_s5_re, state_s5_im, state_dn, state_dn_conv, state_lru, state_lru_conv, final_g, norm1_g, norm2_g, w_ada, b_ada, w_in, s5_log_dt, s5_a_re, s5_a_im, s5_b_re, s5_b_im, s5_c_re, s5_c_im, s5_d, s5_w_glu, s5_b_glu, dn_conv_w, dn_a_log, dn_dt_bias, dn_norm_g, lru_conv_w, lru_conv_b, lru_w_a, lru_b_a, lru_w_x, lru_b_x, lru_lambda, w_branch, w_out, w_rg, b_rg, w_re, b_re, w_e_gate, w_e_up, w_e_down):
    depth = w_in.shape[0]
    bp, lp, _ = x_prompt.shape
    bs, ls, _ = x_sample.shape
    assert ls == 1 and bs % SUBLANE == 0 and lp % SEQ_CHUNK == 0
    gp = _Group(bp, lp, min(lp, 1024))
    gs = _Group(bs, 1, bs)
    groups = (gp, gs)

    pad = (-bp) % SUBLANE
    c_all = jnp.concatenate([c_prompt, jnp.zeros((pad, D_MODEL), F32), c_sample], axis=0)
    mod_all = _ada_mod(c_all, w_ada, b_ada)

    prm = jnp.zeros((depth, 2, LANE), F32)
    prm = prm.at[:, 0, :DN_HEADS].set(dn_a_log).at[:, 1, :DN_HEADS].set(dn_dt_bias)
    w_route = jnp.concatenate([w_rg, w_re, jnp.zeros((depth, D_MODEL, LANE - N_GROUPS - N_EXPERTS), F32)], axis=-1)
    b_route = jnp.concatenate([b_rg, b_re, jnp.zeros((depth, LANE - N_GROUPS - N_EXPERTS), F32)], axis=-1)
    p = dict(norm1_g=norm1_g, norm2_g=norm2_g,
             w_pack=jnp.concatenate([w_in[:, :, :COL_AB + LANE],
                                     jnp.zeros((depth, D_MODEL, PACK_REST - COL_AB - LANE), F32),
                                     w_in[:, :, COL_REST:]], axis=-1).astype(MXU_DTYPE),
             s5_disc=_s5_discretize(s5_log_dt, s5_a_re, s5_a_im, s5_b_re, s5_b_im),
             s5_cr=_s5_out_blocks(s5_c_re), s5_ci=_s5_out_blocks(s5_c_im),
             s5_d=s5_d, s5_w_glu=s5_w_glu, s5_b_glu=s5_b_glu,
             dn_conv_w=dn_conv_w, dn_prm=prm, dn_norm_g=dn_norm_g,
             lru_conv_w=lru_conv_w, lru_conv_b=lru_conv_b, lru_wa_bd=_block_diag(lru_w_a), lru_b_a=lru_b_a,
             lru_wx_bd=_block_diag(lru_w_x), lru_b_x=lru_b_x, lru_lambda=lru_lambda,
             w_branch=w_branch, w_out=w_out, w_route=w_route, b_route=b_route.reshape(depth, 1, LANE),
             w_e_gate=w_e_gate, w_e_up=w_e_up, w_e_down=w_e_down, past_len=PAST_LEN)

    xs = [x_prompt.reshape(gp.T, D_MODEL), x_sample.reshape(gs.T, D_MODEL)]
    per_layer = []
    for l in range(depth):
        mods = [mod_all[l, :bp].reshape(bp, 1, 6 * D_MODEL), mod_all[l, bp + pad:]]
        states = [(None, None, None, None, None),
                  (state_s5_re[l].reshape(bs, S5_N), state_s5_im[l].reshape(bs, S5_N), state_dn_conv[l],
                   state_lru[l], state_lru_conv[l])]
        xs, new_states = _layer(l, groups, xs, mods, states, p, state_dn)
        per_layer.append(new_states)

    y_prompt = _final_norm(gp, xs[0], final_g).reshape(bp, lp, D_MODEL)
    y_sample = _final_norm(gs, xs[1], final_g).reshape(bs, 1, D_MODEL)
    stack = lambda gi: tuple(jnp.stack([per_layer[l][gi][k] for l in range(depth)]) for k in range(6))
    return (y_prompt, y_sample) + stack(0) + stack(1)
```

```python
import functools
import math
from typing import NamedTuple

import jax
import jax.numpy as jnp
from jax import lax
from jax.experimental import pallas as pl
from jax.experimental.pallas import tpu as pltpu

F32 = jnp.float32
MXU_DTYPE = jnp.bfloat16

D_MODEL = 2048
BRANCH_W = D_MODEL // 4
S5_GROUP = 16
S5_GROUPS = BRANCH_W // S5_GROUP
S5_STATE = 64
S5_N = S5_GROUPS * S5_STATE
S5_BLOCKS = 4
DN_HEADS = 4
DN_HEAD_DIM = 128
DN_W = DN_HEADS * DN_HEAD_DIM
DN_CHUNK = 64
CONV_W = 4
LRU_W = BRANCH_W
LRU_BLOCKS = 4
LRU_C = 8.0
N_GROUPS = 4
EXPERTS_PER_GROUP = 8
N_EXPERTS = N_GROUPS * EXPERTS_PER_GROUP
TOP_K = 2
D_EXPERT = D_MODEL // 4
NORM_EPS = 1e-6
PAST_LEN = 16384

COL_AB = BRANCH_W + 3 * DN_W
COL_REST = COL_AB + 2 * DN_HEADS
REST_GATE = DN_W + 2 * LRU_W
PACK_REST = 2560

LANE = 128
SUBLANE = 8
VMEM_LIMIT = 56 * 1024 * 1024
SEQ_CHUNK = 256
MOE_ROWS = 256
COMBINE_ROWS = 256
XPOSE_CHUNK = 256
ROUTER_ROWS = 128


class _Group(NamedTuple):
    B: int
    L: int
    tm: int

    @property
    def T(self):
        return self.B * self.L

    @property
    def nt(self):
        return max(self.L // self.tm, 1)

    @property
    def n_tiles(self):
        return self.T // self.tm


def _params(*sem):
    return pltpu.CompilerParams(dimension_semantics=sem, vmem_limit_bytes=VMEM_LIMIT)


def _mm(a, b):
    return jnp.dot(a.astype(MXU_DTYPE), b.astype(MXU_DTYPE), preferred_element_type=F32)


def _mm_nt(a, b):
    return lax.dot_general(a.astype(MXU_DTYPE), b.astype(MXU_DTYPE), (((1,), (1,)), ((), ())),
                           preferred_element_type=F32)


def _mm_tn(a, b):
    return lax.dot_general(a.astype(MXU_DTYPE), b.astype(MXU_DTYPE), (((0,), (0,)), ((), ())),
                           preferred_element_type=F32)


def _silu(x):
    return x * jax.nn.sigmoid(x)


def _gelu_tanh(x):
    return 0.5 * x * (1.0 + jnp.tanh(math.sqrt(2.0 / math.pi) * (x + 0.044715 * (x * x * x))))


def _softplus(x):
    return jnp.maximum(x, 0.0) + jnp.log1p(jnp.exp(-jnp.abs(x)))


def _rms(x, gain):
    return x * lax.rsqrt(jnp.mean(x * x, axis=-1, keepdims=True) + NORM_EPS) * gain


def _row_iota(shape):
    return lax.broadcasted_iota(jnp.int32, shape, len(shape) - 2)


def _last_row(x):
    return jnp.broadcast_to(x[SUBLANE - 1:SUBLANE], x.shape)


def _nat_spec(g, tn, col0=0):
    return pl.BlockSpec((g.tm, tn), lambda i, j: (i, col0 + j))


def _mod_spec(g, k, tn):
    nb, nt = D_MODEL // tn, g.nt
    if g.L == 1:
        return pl.BlockSpec((g.tm, tn), lambda i, j: (i, k * nb + j))
    return pl.BlockSpec((None, 1, tn), lambda i, j: (i // nt, 0, k * nb + j))


def _ada_kernel(c_ref, w_ref, b_ref, o_ref):
    o_ref[...] = _mm(_silu(c_ref[...]), w_ref[...]) + b_ref[...]


def _ada_mod(c_all, w_ada, b_ada):
    depth, _, n6 = w_ada.shape
    rows = c_all.shape[0]
    tn = 1024
    return pl.pallas_call(
        _ada_kernel,
        grid=(depth, n6 // tn),
        in_specs=[pl.BlockSpec((rows, D_MODEL), lambda l, j: (0, 0)),
                  pl.BlockSpec((None, D_MODEL, tn), lambda l, j: (l, 0, j)),
                  pl.BlockSpec((None, 1, tn), lambda l, j: (l, 0, j))],
        out_specs=pl.BlockSpec((None, rows, tn), lambda l, j: (l, 0, j)),
        out_shape=jax.ShapeDtypeStruct((depth, rows, n6), F32),
        compiler_params=_params("arbitrary", "arbitrary"),
        name="ada_mod",
    )(c_all, w_ada, b_ada.reshape(depth, 1, n6))


def _norm_mod_kernel(x_ref, g_ref, sc_ref, sh_ref, o_ref):
    y = _rms(x_ref[...], g_ref[...])
    o_ref[...] = (y * (1.0 + sc_ref[...]) + sh_ref[...]).astype(o_ref.dtype)


def _norm_mod(g, x, gain, mod, k_shift, k_scale):
    gm = g._replace(tm=min(g.tm, 512))
    return pl.pallas_call(
        _norm_mod_kernel,
        grid=(gm.n_tiles, 1),
        in_specs=[_nat_spec(gm, D_MODEL),
                  pl.BlockSpec((1, D_MODEL), lambda i, j: (0, 0)),
                  _mod_spec(gm, k_scale, D_MODEL),
                  _mod_spec(gm, k_shift, D_MODEL)],
        out_specs=_nat_spec(gm, D_MODEL),
        out_shape=jax.ShapeDtypeStruct((g.T, D_MODEL), MXU_DTYPE),
        compiler_params=_params("arbitrary", "arbitrary"),
        name="norm_mod",
    )(x, gain.reshape(1, D_MODEL), mod, mod)


def _final_norm_kernel(x_ref, g_ref, o_ref):
    o_ref[...] = _rms(x_ref[...], g_ref[...])


def _final_norm(g, x, gain):
    gm = g._replace(tm=min(g.tm, 512))
    return pl.pallas_call(
        _final_norm_kernel,
        grid=(gm.n_tiles, 1),
        in_specs=[_nat_spec(gm, D_MODEL), pl.BlockSpec((1, D_MODEL), lambda i, j: (0, 0))],
        out_specs=_nat_spec(gm, D_MODEL),
        out_shape=jax.ShapeDtypeStruct((g.T, D_MODEL), F32),
        compiler_params=_params("arbitrary", "arbitrary"),
        name="final_norm",
    )(x, gain.reshape(1, D_MODEL))


def _proj_kernel(a_ref, w_ref, o_ref):
    o_ref[...] = _mm(a_ref[...], w_ref[...]).astype(o_ref.dtype)


def _proj(g, a, w, layer, col0, n, tn):
    cb = col0 // tn
    return pl.pallas_call(
        _proj_kernel,
        grid=(g.n_tiles, n // tn),
        in_specs=[pl.BlockSpec((g.tm, D_MODEL), lambda i, j: (i, 0)),
                  pl.BlockSpec((None, D_MODEL, tn), lambda i, j: (layer, 0, cb + j))],
        out_specs=_nat_spec(g, tn),
        out_shape=jax.ShapeDtypeStruct((g.T, n), F32),
        compiler_params=_params("arbitrary", "arbitrary"),
        name="in_proj",
    )(a, w)


def _out_proj_kernel(a_ref, w_ref, x_ref, g_ref, o_ref):
    o_ref[...] = x_ref[...] + g_ref[...] * _mm(a_ref[...], w_ref[...])


def _out_proj(g, merged, w_out, layer, x, mod, k_gate):
    tn = 512
    return pl.pallas_call(
        _out_proj_kernel,
        grid=(g.n_tiles, D_MODEL // tn),
        in_specs=[pl.BlockSpec((g.tm, D_MODEL), lambda i, j: (i, 0)),
                  pl.BlockSpec((None, D_MODEL, tn), lambda i, j: (layer, 0, j)),
                  _nat_spec(g, tn),
                  _mod_spec(g, k_gate, tn)],
        out_specs=_nat_spec(g, tn),
        out_shape=jax.ShapeDtypeStruct((g.T, D_MODEL), F32),
        compiler_params=_params("arbitrary", "arbitrary"),
        name="out_proj",
    )(merged, w_out, x, mod)


def _merge_kernel(h_ref, ya_ref, yb_ref, yc_ref, wg0, wg1, wg2, wb0, wb1, wb2, o_ref):
    h = h_ref[...]
    acc = None
    for y_ref, wg, wb in ((ya_ref, wg0, wb0), (yb_ref, wg1, wb1), (yc_ref, wg2, wb2)):
        t = jax.nn.sigmoid(_mm(h, wg[...])) * _mm(y_ref[...], wb[...])
        acc = t if acc is None else acc + t
    o_ref[...] = acc.astype(o_ref.dtype)


def _merge(g, h, ya, yb, yc, w_pack, w_branch, layer):
    tn = 256
    gate_specs = [pl.BlockSpec((None, D_MODEL, tn),
                               functools.partial(lambda i, j, c: (layer, 0, c + j), c=(PACK_REST + REST_GATE + n * D_MODEL) // tn))
                  for n in range(3)]
    br_specs = [pl.BlockSpec((None, None, BRANCH_W, tn), functools.partial(lambda i, j, n: (layer, n, 0, j), n=n))
                for n in range(3)]
    y_spec = pl.BlockSpec((g.tm, BRANCH_W), lambda i, j: (i, 0))
    return pl.pallas_call(
        _merge_kernel,
        grid=(g.n_tiles, D_MODEL // tn),
        in_specs=[pl.BlockSpec((g.tm, D_MODEL), lambda i, j: (i, 0)), y_spec, y_spec, y_spec] + gate_specs + br_specs,
        out_specs=_nat_spec(g, tn),
        out_shape=jax.ShapeDtypeStruct((g.T, D_MODEL), MXU_DTYPE),
        compiler_params=_params("arbitrary", "arbitrary"),
        name="merge",
    )(h, ya, yb, yc, w_pack, w_pack, w_pack, w_branch, w_branch, w_branch)


def _s5_disc_kernel(ldt_ref, ar_ref, ai_ref, br_ref, bi_ref, lr_ref, li_ref, bbr_ref, bbi_ref):
    dt = jnp.exp(ldt_ref[...])
    ar, ai = ar_ref[...], ai_ref[...]
    mag = jnp.exp(ar * dt)
    lr = mag * jnp.cos(ai * dt)
    li = mag * jnp.sin(ai * dt)
    den = ar * ar + ai * ai
    fr = ((lr - 1.0) * ar + li * ai) / den
    fi = (li * ar - (lr - 1.0) * ai) / den
    lr_ref[...] = lr
    li_ref[...] = li
    br, bi = br_ref[...], bi_ref[...]
    bbr_ref[...] = fr * br - fi * bi
    bbi_ref[...] = fr * bi + fi * br


def _s5_discretize(log_dt, a_re, a_im, b_re, b_im):
    depth = log_dt.shape[0]
    n = depth * S5_GROUPS
    rows = n * S5_GROUP
    rep = lambda v, w: jnp.repeat(v.reshape(n, w), S5_GROUP, axis=0)
    brt = jnp.swapaxes(b_re, -1, -2).reshape(rows, S5_STATE)
    bit = jnp.swapaxes(b_im, -1, -2).reshape(rows, S5_STATE)
    sds = jax.ShapeDtypeStruct
    lr, li, bbr, bbi = pl.pallas_call(
        _s5_disc_kernel,
        out_shape=tuple(sds((rows, S5_STATE), F32) for _ in range(4)),
        name="s5_discretize",
    )(rep(log_dt, 1), rep(a_re, S5_STATE), rep(a_im, S5_STATE), brt, bit)
    gpb = S5_GROUPS // S5_BLOCKS
    eye = jnp.eye(gpb, dtype=F32)

    def blocks_in(m):
        m = m.reshape(depth, S5_BLOCKS, gpb, S5_GROUP, S5_STATE)
        m = jnp.einsum('ljgnp,gh->ljgnhp', m, eye)
        return m.reshape(depth, S5_BLOCKS, gpb * S5_GROUP, gpb * S5_STATE).astype(MXU_DTYPE)

    lam = lambda v: v.reshape(n, S5_GROUP, S5_STATE)[:, 0].reshape(depth, 1, S5_N)
    return lam(lr), lam(li), blocks_in(bbr), blocks_in(bbi)


def _s5_out_blocks(c):
    depth = c.shape[0]
    gpb = S5_GROUPS // S5_BLOCKS
    eye = jnp.eye(gpb, dtype=F32)
    m = jnp.einsum('ljgnp,gh->ljgphn', c.reshape(depth, S5_BLOCKS, gpb, S5_GROUP, S5_STATE), eye)
    return m.reshape(depth, S5_BLOCKS, gpb * S5_STATE, gpb * S5_GROUP).astype(MXU_DTYPE)


def _s5_in(u, bb_ref):
    cw, sw = BRANCH_W // S5_BLOCKS, S5_N // S5_BLOCKS
    return [_mm(u[:, j * cw:(j + 1) * cw], bb_ref[j]) for j in range(S5_BLOCKS)], sw


def _s5_out(hr, hi, u, cr_ref, ci_ref, d_ref, wglu_ref, bglu_ref):
    sw = S5_N // S5_BLOCKS
    y = jnp.concatenate([_mm(hr[:, j * sw:(j + 1) * sw], cr_ref[j]) - _mm(hi[:, j * sw:(j + 1) * sw], ci_ref[j])
                         for j in range(S5_BLOCKS)], axis=1)
    y = _gelu_tanh(y + d_ref[...] * u)
    return y * jax.nn.sigmoid(_mm(y, wglu_ref[...]) + bglu_ref[...])


def _scan_tiles(n_tiles, cw, width, tile_fn, load_state, store_state):
    for c in range(width // cw):
        cs = pl.ds(c * cw, cw)

        def body(k, carry, cs=cs):
            return tile_fn(pl.ds(pl.multiple_of(k * SUBLANE, SUBLANE), SUBLANE), cs, carry)

        store_state(cs, lax.fori_loop(0, n_tiles, body, load_state(cs)))


def _split3(x):
    hi = x.astype(MXU_DTYPE)
    r1 = x - hi.astype(F32)
    mid = r1.astype(MXU_DTYPE)
    return hi, mid, (r1 - mid.astype(F32)).astype(MXU_DTYPE)


def _s5_seq_kernel(u_ref, bbr_ref, bbi_ref, lr_ref, li_ref, cr_ref, ci_ref, d_ref, wglu_ref, bglu_ref,
                   y_ref, hlr_ref, hli_ref, xr_s, xi_s, cr_s, ci_s, pw_s):
    c = pl.program_id(1)
    lc = u_ref.shape[0]
    seg = lc // SUBLANE
    assert seg & (seg - 1) == 0

    @pl.when(c == 0)
    def _():
        cr_s[...] = jnp.zeros_like(cr_s)
        ci_s[...] = jnp.zeros_like(ci_s)
        lr, li = lr_ref[...], li_ref[...]
        for _ in range(seg.bit_length() - 1):
            lr, li = lr * lr - li * li, 2.0 * lr * li
        row = _row_iota((SUBLANE, S5_N))
        pr, pi = lr, li
        acc_r = jnp.zeros((SUBLANE, S5_N), F32)
        acc_i = jnp.zeros((SUBLANE, S5_N), F32)
        for r in range(SUBLANE):
            acc_r = jnp.where(row == r, pr, acc_r)
            acc_i = jnp.where(row == r, pi, acc_i)
            if r + 1 in (1, 2, 4):
                k = (1, 2, 4).index(r + 1)
                pw_s[2 * k] = jnp.where(row >= r + 1, pr, 0.0)
                pw_s[2 * k + 1] = jnp.where(row >= r + 1, pi, 0.0)
            pr, pi = pr * lr - pi * li, pr * li + pi * lr
        pw_s[6] = acc_r
        pw_s[7] = acc_i

    n_i = lax.broadcasted_iota(jnp.int32, (lc, lc), 0)
    t_i = lax.broadcasted_iota(jnp.int32, (lc, lc), 1)
    sh = seg.bit_length() - 1
    perm = (t_i == (n_i & (SUBLANE - 1)) * seg + lax.shift_right_logical(n_i, 3)).astype(MXU_DTYPE)
    unperm = (t_i == (n_i & (seg - 1)) * SUBLANE + lax.shift_right_logical(n_i, sh)).astype(MXU_DTYPE)

    u = sum(_mm(perm, t) for t in _split3(u_ref[...]))
    xr, sw = _s5_in(u, bbr_ref)
    xi, _ = _s5_in(u, bbi_ref)
    for j in range(S5_BLOCKS):
        xr_s[:, pl.ds(j * sw, sw)] = xr[j]
        xi_s[:, pl.ds(j * sw, sw)] = xi[j]

    cw = 512
    for cc in range(S5_N // cw):
        cs = pl.ds(cc * cw, cw)
        lr, li = lr_ref[:, cs], li_ref[:, cs]

        def tile_rows(k):
            return pl.ds(pl.multiple_of(k * SUBLANE, SUBLANE), SUBLANE)

        def pass1(k, carry, cs=cs, lr=lr, li=li):
            gr, gi = carry
            rows = tile_rows(k)
            gr, gi = lr * gr - li * gi + xr_s[rows, cs], lr * gi + li * gr + xi_s[rows, cs]
            xr_s[rows, cs] = gr
            xi_s[rows, cs] = gi
            return gr, gi

        zero = jnp.zeros((SUBLANE, cw), F32)
        sr, si = lax.fori_loop(0, seg, pass1, (zero, zero))

        for k, s in enumerate((1, 2, 4)):
            ar, ai = pw_s[2 * k, :, cs], pw_s[2 * k + 1, :, cs]
            qr, qi = pltpu.roll(sr, s, 0), pltpu.roll(si, s, 0)
            sr, si = sr + ar * qr - ai * qi, si + ar * qi + ai * qr
        ar, ai = pw_s[6, :, cs], pw_s[7, :, cs]
        c0r, c0i = cr_s[:, cs], ci_s[:, cs]
        sr, si = sr + ar * c0r - ai * c0i, si + ar * c0i + ai * c0r
        first = _row_iota((SUBLANE, cw)) == 0
        jr = jnp.where(first, c0r, pltpu.roll(sr, 1, 0))
        ji = jnp.where(first, c0i, pltpu.roll(si, 1, 0))
        cr_s[:, cs] = _last_row(sr)
        ci_s[:, cs] = _last_row(si)

        def pass2(k, carry, cs=cs, lr=lr, li=li):
            jr, ji = carry
            rows = tile_rows(k)
            jr, ji = lr * jr - li * ji, lr * ji + li * jr
            xr_s[rows, cs] = xr_s[rows, cs] + jr
            xi_s[rows, cs] = xi_s[rows, cs] + ji
            return jr, ji

        lax.fori_loop(0, seg, pass2, (jr, ji))

    y = _s5_out(xr_s[...], xi_s[...], u, cr_ref, ci_ref, d_ref, wglu_ref, bglu_ref).astype(y_ref.dtype)
    y_ref[...] = _mm(unperm, y).astype(y_ref.dtype)

    @pl.when(c == pl.num_programs(1) - 1)
    def _():
        hlr_ref[...] = cr_s[pl.ds(0, 1), :]
        hli_ref[...] = ci_s[pl.ds(0, 1), :]


def _s5_step_kernel(u_ref, h0r_ref, h0i_ref, bbr_ref, bbi_ref, lr_ref, li_ref, cr_ref, ci_ref, d_ref, wglu_ref,
                    bglu_ref, y_ref, hlr_ref, hli_ref):
    u = u_ref[...]
    xr, _ = _s5_in(u, bbr_ref)
    xi, _ = _s5_in(u, bbi_ref)
    lr, li, sr, si = lr_ref[...], li_ref[...], h0r_ref[...], h0i_ref[...]
    hr = lr * sr - li * si + jnp.concatenate(xr, axis=1)
    hi = lr * si + li * sr + jnp.concatenate(xi, axis=1)
    hlr_ref[...] = hr
    hli_ref[...] = hi
    y_ref[...] = _s5_out(hr, hi, u, cr_ref, ci_ref, d_ref, wglu_ref, bglu_ref).astype(y_ref.dtype)


def _s5(g, uq, h0r, h0i, disc, cr, ci, d, w_glu, b_glu, layer):
    lam_r, lam_i, bbr, bbi = disc
    cw, sw = BRANCH_W // S5_BLOCKS, S5_N // S5_BLOCKS
    sds = jax.ShapeDtypeStruct
    seq = g.L > 1
    lc = min(g.L, SEQ_CHUNK) if seq else g.B
    nc = g.L // lc if seq else 1
    grid = (g.B, nc) if seq else (1, 1)
    row = (lambda b, c: (b * nc + c, 0))
    lay3 = lambda b, c: (layer, 0, 0)
    lay4 = lambda b, c: (layer, 0, 0, 0)
    w_specs = [pl.BlockSpec((None, S5_BLOCKS, cw, sw), lay4), pl.BlockSpec((None, S5_BLOCKS, cw, sw), lay4),
               pl.BlockSpec((None, 1, S5_N), lay3), pl.BlockSpec((None, 1, S5_N), lay3),
               pl.BlockSpec((None, S5_BLOCKS, sw, cw), lay4), pl.BlockSpec((None, S5_BLOCKS, sw, cw), lay4),
               pl.BlockSpec((None, 1, BRANCH_W), lay3),
               pl.BlockSpec((None, BRANCH_W, BRANCH_W), lay3),
               pl.BlockSpec((None, 1, BRANCH_W), lay3)]
    w_args = (bbr, bbi, lam_r, lam_i, cr, ci, d.reshape(-1, 1, BRANCH_W), w_glu, b_glu.reshape(-1, 1, BRANCH_W))
    if seq:
        st_spec = pl.BlockSpec((None, 1, S5_N), lambda b, c: (b, 0, 0))
        y, hlr, hli = pl.pallas_call(
            _s5_seq_kernel,
            grid=grid,
            in_specs=[pl.BlockSpec((lc, BRANCH_W), row)] + w_specs,
            out_specs=(pl.BlockSpec((lc, BRANCH_W), row), st_spec, st_spec),
            out_shape=(sds((g.T, BRANCH_W), MXU_DTYPE), sds((g.B, 1, S5_N), F32), sds((g.B, 1, S5_N), F32)),
            scratch_shapes=[pltpu.VMEM((lc, S5_N), F32), pltpu.VMEM((lc, S5_N), F32),
                            pltpu.VMEM((SUBLANE, S5_N), F32), pltpu.VMEM((SUBLANE, S5_N), F32),
                            pltpu.VMEM((8, SUBLANE, S5_N), F32)],
            compiler_params=_params("arbitrary", "arbitrary"),
            name="s5_ssm",
        )(uq, *w_args)
        return y, hlr.reshape(g.B, S5_N), hli.reshape(g.B, S5_N)
    st_spec = pl.BlockSpec((g.B, S5_N), lambda b, c: (0, 0))
    return pl.pallas_call(
        _s5_step_kernel,
        grid=grid,
        in_specs=[pl.BlockSpec((lc, BRANCH_W), row), st_spec, st_spec] + w_specs,
        out_specs=(pl.BlockSpec((lc, BRANCH_W), row), st_spec, st_spec),
        out_shape=(sds((g.T, BRANCH_W), MXU_DTYPE), sds((g.B, S5_N), F32), sds((g.B, S5_N), F32)),
        compiler_params=_params("arbitrary", "arbitrary"),
        name="s5_ssm",
    )(uq, h0r, h0i, *w_args)


def _lru_gates(xc, wa_ref, ba_ref, wx_ref, bx_ref, lam_ref):
    r = jax.nn.sigmoid(_mm(xc, wa_ref[...]) + ba_ref[...])
    ig = jax.nn.sigmoid(_mm(xc, wx_ref[...]) + bx_ref[...])
    log_a = -LRU_C * r * _softplus(-lam_ref[...])
    return jnp.exp(log_a), jnp.sqrt(1.0 - jnp.exp(2.0 * log_a)), ig


def _lru_seq_kernel(x_ref, g_ref, cw_ref, cbias_ref, wa_ref, ba_ref, wx_ref, bx_ref, lam_ref,
                    y_ref, hl_ref, xbuf, a_s, b_s, h_s):
    c = pl.program_id(1)
    lc = x_ref.shape[0]
    pad = SUBLANE

    @pl.when(c == 0)
    def _():
        xbuf[pl.ds(0, pad), :] = jnp.zeros((pad, LRU_W), F32)
        h_s[...] = jnp.zeros_like(h_s)

    xbuf[pl.ds(pad, lc), :] = x_ref[...]
    xc = cbias_ref[...]
    for j in range(CONV_W):
        xc = xc + cw_ref[pl.ds(j, 1), :] * xbuf[pl.ds(pad - (CONV_W - 1) + j, lc), :]
    new_tail = xbuf[pl.ds(lc, pad), :]
    xbuf[pl.ds(0, pad), :] = new_tail

    a, mult, ig = _lru_gates(xc, wa_ref, ba_ref, wx_ref, bx_ref, lam_ref)
    first = jnp.logical_and(c == 0, _row_iota((lc, LRU_W)) == 0)
    a_s[...] = jnp.where(first, 0.0, a)
    b_s[...] = jnp.where(first, 1.0, mult) * (ig * xc)

    def tile(rows, cs, h0):
        a, b = a_s[rows, cs], b_s[rows, cs]
        row = _row_iota(a.shape)
        for s in (1, 2, 4):
            keep = row >= s
            b = a * jnp.where(keep, pltpu.roll(b, s, 0), 0.0) + b
            a = a * jnp.where(keep, pltpu.roll(a, s, 0), 1.0)
        h = a * h0 + b
        b_s[rows, cs] = h
        return _last_row(h)

    def store(cs, h):
        h_s[:, cs] = h

    _scan_tiles(lc // SUBLANE, LRU_W, LRU_W, tile, lambda cs: h_s[:, cs], store)
    y_ref[...] = (b_s[...] * _gelu_tanh(g_ref[...])).astype(y_ref.dtype)

    @pl.when(c == pl.num_programs(1) - 1)
    def _():
        hl_ref[...] = h_s[pl.ds(0, 1), :]


def _lru_step_kernel(x_ref, g_ref, cs_ref, h0_ref, cw_ref, cbias_ref, wa_ref, ba_ref, wx_ref, bx_ref, lam_ref,
                     y_ref, hl_ref):
    x = x_ref[...]
    xc = cbias_ref[...] + cw_ref[pl.ds(CONV_W - 1, 1), :] * x
    for j in range(CONV_W - 1):
        xc = xc + cw_ref[pl.ds(j, 1), :] * cs_ref[j]
    a, mult, ig = _lru_gates(xc, wa_ref, ba_ref, wx_ref, bx_ref, lam_ref)
    h = a * h0_ref[...] + mult * (ig * xc)
    hl_ref[...] = h
    y_ref[...] = (h * _gelu_tanh(g_ref[...])).astype(y_ref.dtype)


def _block_diag(w):
    depth, nb, k, _ = w.shape
    eye = jnp.eye(nb, dtype=w.dtype)
    return jnp.einsum('lnwv,nm->lnwmv', w, eye).reshape(depth, nb * k, nb * k).astype(MXU_DTYPE)


def _lru(g, rest, conv_state, h0, p, layer, pos0):
    sds = jax.ShapeDtypeStruct
    seq = g.L > 1
    lc = min(g.L, SEQ_CHUNK) if seq else g.B
    nc = g.L // lc if seq else 1
    lay3 = lambda b, c: (layer, 0, 0)
    vec = lambda v: v.reshape(-1, 1, LRU_W)
    w_specs = [pl.BlockSpec((None, CONV_W, LRU_W), lay3), pl.BlockSpec((None, 1, LRU_W), lay3),
               pl.BlockSpec((None, LRU_W, LRU_W), lay3), pl.BlockSpec((None, 1, LRU_W), lay3),
               pl.BlockSpec((None, LRU_W, LRU_W), lay3), pl.BlockSpec((None, 1, LRU_W), lay3),
               pl.BlockSpec((None, 1, LRU_W), lay3)]
    w_args = (p['lru_conv_w'], vec(p['lru_conv_b']), p['lru_wa_bd'], vec(p['lru_b_a']), p['lru_wx_bd'],
              vec(p['lru_b_x']), vec(p['lru_lambda']))
    x_spec = pl.BlockSpec((lc, LRU_W), lambda b, c: (b * nc + c, 1))
    g_spec = pl.BlockSpec((lc, LRU_W), lambda b, c: (b * nc + c, 2))
    y_spec = pl.BlockSpec((lc, LRU_W), lambda b, c: (b * nc + c, 0))
    if seq:
        assert pos0 == 0
        y, hl = pl.pallas_call(
            _lru_seq_kernel,
            grid=(g.B, nc),
            in_specs=[x_spec, g_spec] + w_specs,
            out_specs=(y_spec, pl.BlockSpec((None, 1, LRU_W), lambda b, c: (b, 0, 0))),
            out_shape=(sds((g.T, LRU_W), MXU_DTYPE), sds((g.B, 1, LRU_W), F32)),
            scratch_shapes=[pltpu.VMEM((lc + SUBLANE, LRU_W), F32), pltpu.VMEM((lc, LRU_W), F32),
                            pltpu.VMEM((lc, LRU_W), F32), pltpu.VMEM((SUBLANE, LRU_W), F32)],
            compiler_params=_params("arbitrary", "arbitrary"),
            name="rg_lru",
        )(rest, rest, *w_args)
        return y, hl.reshape(g.B, LRU_W)
    assert pos0 > 0
    st_spec = pl.BlockSpec((g.B, LRU_W), lambda b, c: (0, 0))
    return pl.pallas_call(
        _lru_step_kernel,
        grid=(1, 1),
        in_specs=[x_spec, g_spec, pl.BlockSpec((CONV_W - 1, g.B, LRU_W), lambda b, c: (0, 0, 0)), st_spec] + w_specs,
        out_specs=(y_spec, st_spec),
        out_shape=(sds((g.T, LRU_W), MXU_DTYPE), sds((g.B, LRU_W), F32)),
        compiler_params=_params("arbitrary", "arbitrary"),
        name="rg_lru",
    )(rest, rest, jnp.swapaxes(conv_state, 0, 1), h0, *w_args)


def _l2norm(x):
    return x * lax.rsqrt(jnp.sum(x * x, axis=-1, keepdims=True) + NORM_EPS)


def _dn_gates(ab, prm):
    g = -jnp.exp(prm[0:1, :]) * _softplus(ab + prm[1:2, :])
    return g, jax.nn.sigmoid(ab)


def _dn_seq_kernel(q_ref, k_ref, v_ref, z_ref, ab_ref, cw_ref, prm_ref, ng_ref, y_ref, sfin_ref, xbuf, s_s):
    c = pl.program_id(0)
    bsz, cs, _ = q_ref.shape
    hd = DN_HEAD_DIM
    rows = bsz * cs
    pad = SUBLANE

    @pl.when(c == 0)
    def _():
        xbuf[:, pl.ds(0, pad), :] = jnp.zeros((bsz, pad, 3 * DN_W), F32)
        s_s[...] = jnp.zeros_like(s_s)

    for s, ref in enumerate((q_ref, k_ref, v_ref)):
        xbuf[:, pl.ds(pad, cs), pl.ds(s * DN_W, DN_W)] = ref[...]
    conv = None
    for j in range(CONV_W):
        t = cw_ref[pl.ds(j, 1), :] * xbuf[:, pl.ds(pad - (CONV_W - 1) + j, cs), :]
        conv = t if conv is None else conv + t
    new_tail = xbuf[:, pl.ds(cs, pad), :]
    xbuf[:, pl.ds(0, pad), :] = new_tail
    act = _silu(conv).reshape(rows, 3 * DN_W)

    glog, beta_all = _dn_gates(ab_ref[...].reshape(rows, LANE), prm_ref[...])
    row = lax.broadcasted_iota(jnp.int32, (rows, rows), 0)
    col = lax.broadcasted_iota(jnp.int32, (rows, rows), 1)
    shift = cs.bit_length() - 1
    same = lax.shift_right_logical(row, shift) == lax.shift_right_logical(col, shift)
    causal = jnp.logical_and(same, row >= col)
    strict = jnp.logical_and(same, row > col)
    gc = jnp.dot(causal.astype(F32), glog, precision=lax.Precision.HIGHEST, preferred_element_type=F32)
    gct = gc.T
    eye = (row == col).astype(F32)
    ng = ng_ref[...]
    rblk = lax.shift_right_logical(lax.broadcasted_iota(jnp.int32, (rows, bsz * hd), 0), shift)
    cblk = lax.shift_right_logical(lax.broadcasted_iota(jnp.int32, (rows, bsz * hd), 1), hd.bit_length() - 1)
    own = rblk == cblk

    def diag_blocks(m):
        return jnp.concatenate([m[b * cs:(b + 1) * cs, b * hd:(b + 1) * hd] for b in range(bsz)], axis=0)

    for h in range(DN_HEADS):
        q = _l2norm(act[:, h * hd:(h + 1) * hd]) * (hd ** -0.5)
        k = _l2norm(act[:, DN_W + h * hd:DN_W + (h + 1) * hd])
        v = act[:, 2 * DN_W + h * hd:2 * DN_W + (h + 1) * hd]
        beta = beta_all[:, DN_HEADS + h:DN_HEADS + h + 1]
        gcol = gc[:, h:h + 1]
        decay = jnp.exp(jnp.where(causal, gcol - gct[h:h + 1, :], -jnp.inf))
        kb = k * beta
        kq = _mm_nt(jnp.concatenate([kb, q], axis=0), k)
        nmat = jnp.where(strict, -(kq[:rows] * decay), 0.0)
        att = jnp.where(causal, kq[rows:] * decay, 0.0)
        rinv = eye + nmat
        xpow = _mm(nmat, nmat)
        for _ in range(cs.bit_length() - 3):
            st = _mm(jnp.concatenate([xpow, rinv], axis=0), xpow)
            xpow, rinv = st[:rows], rinv + st[rows:]
        rinv = rinv + _mm(rinv, xpow)
        eg = jnp.exp(gcol)
        uw = _mm(rinv, jnp.concatenate([v * beta, kb * eg], axis=1))
        u, w = uw[:, :hd], uw[:, hd:]
        s = s_s[h]
        ws_qs = _mm(jnp.concatenate([w, q * eg], axis=0), s)
        v_new = u - diag_blocks(ws_qs[:rows])
        o = diag_blocks(ws_qs[rows:]) + _mm(att, v_new)
        glast = [gc[(b + 1) * cs - 1:(b + 1) * cs, h:h + 1] for b in range(bsz)]
        glast_rows = jnp.concatenate([jnp.broadcast_to(t, (cs, 1)) for t in glast], axis=0)
        glast_lanes = jnp.concatenate([jnp.broadcast_to(t, (1, hd)) for t in glast], axis=1)
        v_bd = jnp.where(own, jnp.concatenate([v_new] * bsz, axis=1), 0.0)
        s_s[h] = s * jnp.exp(glast_lanes) + _mm_tn(k * jnp.exp(glast_rows - gcol), v_bd)
        o = _rms(o, ng) * _silu(z_ref[:, :, pl.ds(h * hd, hd)].reshape(rows, hd))
        y_ref[:, :, pl.ds(h * hd, hd)] = o.reshape(bsz, cs, hd).astype(y_ref.dtype)

    @pl.when(c == pl.num_programs(0) - 1)
    def _():
        for b in range(bsz):
            for h in range(DN_HEADS):
                sfin_ref[b, h] = s_s[h, :, pl.ds(b * hd, hd)]


def _dn_seq(g, uq, rest, ab, conv_w, prm, norm_g, layer):
    cs = DN_CHUNK
    assert cs & (cs - 1) == 0 and g.L % cs == 0
    nc = g.L // cs
    hd = DN_HEAD_DIM
    lay3 = lambda c: (layer, 0, 0)
    sds = jax.ShapeDtypeStruct
    sec = lambda s: pl.BlockSpec((g.B, cs, DN_W), lambda c: (0, c, s))
    y, sfin = pl.pallas_call(
        _dn_seq_kernel,
        grid=(nc,),
        in_specs=[sec(1), sec(2), sec(3), sec(0),
                  pl.BlockSpec((g.B, cs, LANE), lambda c: (0, c, 0)),
                  pl.BlockSpec((None, CONV_W, 3 * DN_W), lay3),
                  pl.BlockSpec((None, 2, LANE), lay3),
                  pl.BlockSpec((None, 1, hd), lay3)],
        out_specs=(pl.BlockSpec((g.B, cs, DN_W), lambda c: (0, c, 0)),
                   pl.BlockSpec((g.B, DN_HEADS, hd, hd), lambda c: (0, 0, 0, 0))),
        out_shape=(sds((g.B, g.L, DN_W), MXU_DTYPE), sds((g.B, DN_HEADS, hd, hd), F32)),
        scratch_shapes=[pltpu.VMEM((g.B, cs + SUBLANE, 3 * DN_W), F32),
                        pltpu.VMEM((DN_HEADS, hd, g.B * hd), F32)],
        compiler_params=_params("arbitrary"),
        name="deltanet_seq",
    )(*(3 * [uq.reshape(g.B, g.L, -1)]), rest.reshape(g.B, g.L, -1), ab.reshape(g.B, g.L, LANE), conv_w, prm,
      norm_g.reshape(-1, 1, hd))
    return y.reshape(g.T, DN_W), sfin


def _dn_step_kernel(q_ref, k_ref, v_ref, cq_ref, ck_ref, cv_ref, z_ref, ab_ref, cwq_ref, cwk_ref, cwv_ref,
                    prm_ref, ng_ref, s_ref, y_ref, so_ref, o_s, *, bsz):
    h = pl.program_id(0)
    hd = DN_HEAD_DIM

    def conv(x_ref, c_ref, w_ref):
        acc = w_ref[pl.ds(CONV_W - 1, 1), :] * x_ref[...]
        for j in range(CONV_W - 1):
            acc = acc + w_ref[pl.ds(j, 1), :] * c_ref[j]
        return _silu(acc)

    q = _l2norm(conv(q_ref, cq_ref, cwq_ref)) * (hd ** -0.5)
    k = _l2norm(conv(k_ref, ck_ref, cwk_ref))
    v = conv(v_ref, cv_ref, cwv_ref)
    glog, beta_all = _dn_gates(ab_ref[...], prm_ref[...])
    lane = lax.broadcasted_iota(jnp.int32, (bsz, LANE), 1)
    eg = jnp.exp(jnp.sum(jnp.where(lane == h, glog, 0.0), axis=-1, keepdims=True))
    beta = jnp.sum(jnp.where(lane == DN_HEADS + h, beta_all, 0.0), axis=-1, keepdims=True)
    att = jnp.sum(q * k, axis=-1, keepdims=True)
    kt = k.T
    qt = q.T
    for b in range(bsz):
        s = s_ref[b]
        kcol = kt[:, b:b + 1]
        ks = jnp.sum(s * kcol, axis=0, keepdims=True)
        qs = jnp.sum(s * qt[:, b:b + 1], axis=0, keepdims=True)
        eg_b, beta_b = eg[b:b + 1, :], beta[b:b + 1, :]
        v_new = beta_b * v[b:b + 1, :] - (beta_b * eg_b) * ks
        o_s[pl.ds(b, 1), :] = eg_b * qs + att[b:b + 1, :] * v_new
        so_ref[b] = s * eg_b + kcol * v_new
    y_ref[...] = (_rms(o_s[...], ng_ref[...]) * _silu(z_ref[...])).astype(y_ref.dtype)


def _dn_step(g, uq, conv_state, rest, ab, conv_w, prm, norm_g, s0_all, layer):
    bsz = g.B
    hd = DN_HEAD_DIM
    conv_t = jnp.swapaxes(conv_state, 0, 1)
    sec = lambda s: (lambda h: (0, (s + 1) * DN_HEADS + h))
    sec3 = lambda s: (lambda h: (0, 0, s * DN_HEADS + h))
    secw = lambda s: (lambda h: (layer, 0, s * DN_HEADS + h))
    lay3 = lambda h: (layer, 0, 0)
    sds = jax.ShapeDtypeStruct
    return pl.pallas_call(
        functools.partial(_dn_step_kernel, bsz=bsz),
        grid=(DN_HEADS,),
        in_specs=[pl.BlockSpec((bsz, hd), sec(s)) for s in range(3)]
                 + [pl.BlockSpec((CONV_W - 1, bsz, hd), sec3(s)) for s in range(3)]
                 + [pl.BlockSpec((bsz, hd), lambda h: (0, h)),
                    pl.BlockSpec((bsz, LANE), lambda h: (0, 0))]
                 + [pl.BlockSpec((None, CONV_W, hd), secw(s)) for s in range(3)]
                 + [pl.BlockSpec((None, 2, LANE), lay3),
                    pl.BlockSpec((None, 1, hd), lay3),
                    pl.BlockSpec((None, bsz, None, hd, hd), lambda h: (layer, 0, h, 0, 0))],
        out_specs=(pl.BlockSpec((bsz, hd), lambda h: (0, h)),
                   pl.BlockSpec((bsz, None, hd, hd), lambda h: (0, h, 0, 0))),
        out_shape=(sds((bsz, DN_W), MXU_DTYPE), sds((bsz, DN_HEADS, hd, hd), F32)),
        scratch_shapes=[pltpu.VMEM((bsz, hd), F32)],
        compiler_params=_params("arbitrary"),
        name="deltanet_step",
    )(uq, uq, uq, conv_t, conv_t, conv_t, rest, ab, conv_w, conv_w, conv_w, prm,
      norm_g.reshape(-1, 1, hd), s0_all)


def _pack_pairs(h):
    half = h.shape[1] // 2
    hb = h.astype(MXU_DTYPE).astype(F32)
    lo = lax.shift_right_logical(pltpu.bitcast(hb[:, :half], jnp.uint32), jnp.uint32(16))
    return lo | (pltpu.bitcast(hb[:, half:], jnp.uint32) & jnp.uint32(0xFFFF0000))


def _unpack_pairs(w):
    lo = pltpu.bitcast(lax.shift_left(w, jnp.uint32(16)), F32)
    hi = pltpu.bitcast(w & jnp.uint32(0xFFFF0000), F32)
    return jnp.concatenate([lo, hi], axis=1).astype(MXU_DTYPE)


def _router_kernel(xp_ref, xs_ref, g_ref, scp_ref, shp_ref, scs_ref, shs_ref, wr_ref, br_ref, h_ref, r_ref, *, n_p):
    is_s = pl.program_id(0) >= n_p
    x = jnp.where(is_s, xs_ref[...], xp_ref[...])
    sc = jnp.where(is_s, scs_ref[...], scp_ref[...])
    sh = jnp.where(is_s, shs_ref[...], shp_ref[...])
    h = _rms(x, g_ref[...]) * (1.0 + sc) + sh
    h_ref[...] = _pack_pairs(h)
    logits = _mm(h, wr_ref[...]) + br_ref[...]
    lane = lax.broadcasted_iota(jnp.int32, logits.shape, 1).astype(F32)
    ninf = -jnp.inf
    big = float(LANE)
    lg = jnp.where(lane < N_GROUPS, logits, ninf)
    gmax = jnp.max(lg, axis=-1, keepdims=True)
    gsel = jnp.min(jnp.where(lg == gmax, lane, big), axis=-1, keepdims=True)
    pg = 1.0 / jnp.sum(jnp.where(lane < N_GROUPS, jnp.exp(logits - gmax), 0.0), axis=-1, keepdims=True)
    lo = N_GROUPS + EXPERTS_PER_GROUP * gsel
    le = jnp.where(jnp.logical_and(lane >= lo, lane < lo + EXPERTS_PER_GROUP), logits, ninf)
    v1 = jnp.max(le, axis=-1, keepdims=True)
    i1 = jnp.min(jnp.where(le == v1, lane, big), axis=-1, keepdims=True)
    le2 = jnp.where(lane == i1, ninf, le)
    v2 = jnp.max(le2, axis=-1, keepdims=True)
    i2 = jnp.min(jnp.where(le2 == v2, lane, big), axis=-1, keepdims=True)
    e2 = jnp.exp(v2 - v1)
    w1 = pg / (1.0 + e2)
    w2 = pg * e2 / (1.0 + e2)
    r_ref[...] = jnp.where(lane == 0, i1 - N_GROUPS,
                           jnp.where(lane == 1, i2 - N_GROUPS,
                                     jnp.where(lane == 2, w1, jnp.where(lane == 3, w2, 0.0))))


def _router(groups, xs, gain, mods, w_route, b_route, layer):
    gp, gs = groups
    tm = min(ROUTER_ROWS, gs.T)
    assert gp.L % tm == 0 and gs.T % tm == 0
    n_p, n_s, nt = gp.T // tm, gs.T // tm, gp.L // tm
    t_all = gp.T + gs.T
    p_row = lambda i, j: (jnp.minimum(i, n_p - 1), 0)
    s_row = lambda i, j: (jnp.maximum(i - n_p, 0), 0)
    p_mod = lambda k: pl.BlockSpec((None, 1, D_MODEL), lambda i, j: (jnp.minimum(i, n_p - 1) // nt, 0, k))
    s_mod = lambda k: pl.BlockSpec((tm, D_MODEL), lambda i, j: (jnp.maximum(i - n_p, 0), k))
    sds = jax.ShapeDtypeStruct
    return pl.pallas_call(
        functools.partial(_router_kernel, n_p=n_p),
        grid=(n_p + n_s, 1),
        in_specs=[pl.BlockSpec((tm, D_MODEL), p_row), pl.BlockSpec((tm, D_MODEL), s_row),
                  pl.BlockSpec((1, D_MODEL), lambda i, j: (0, 0)),
                  p_mod(4), p_mod(3), s_mod(4), s_mod(3),
                  pl.BlockSpec((None, D_MODEL, LANE), lambda i, j: (layer, 0, 0)),
                  pl.BlockSpec((None, 1, LANE), lambda i, j: (layer, 0, 0))],
        out_specs=(pl.BlockSpec((tm, D_MODEL // 2), lambda i, j: (i, 0)), pl.BlockSpec((tm, LANE), lambda i, j: (i, 0))),
        out_shape=(sds((t_all, D_MODEL // 2), jnp.uint32), sds((t_all, LANE), F32)),
        compiler_params=_params("arbitrary", "arbitrary"),
        name="moe_router",
    )(xs[0], xs[1], gain.reshape(1, D_MODEL), mods[0], mods[0], mods[1], mods[1], w_route, b_route)


def _row_copy(src_hbm, row, dst, dst_row, sem):
    return pltpu.make_async_copy(src_hbm.at[pl.ds(row, 1)], dst.at[pl.ds(dst_row, 1)], sem)


def _experts_kernel(be_ref, tok_ref, nact_ref, first_ref, nxt_ref, par_ref, h_hbm, wg_hbm, wu_hbm, wd_hbm, y_ref,
                    xbuf, gsem, wst_g, wst_u, wst_d, wsem, wgt, wut, wdt, *, layer):
    b = pl.program_id(0)
    nact = nact_ref[0]
    rb = MOE_ROWS

    def gather(blk, slot):
        for r in range(rb):
            _row_copy(h_hbm, tok_ref[blk * rb + r], xbuf.at[slot], r, gsem.at[slot]).start(priority=1)

    def weight_copies(e, slot):
        return (pltpu.make_async_copy(wg_hbm.at[layer, e], wst_g.at[slot], wsem.at[slot]),
                pltpu.make_async_copy(wu_hbm.at[layer, e], wst_u.at[slot], wsem.at[slot]),
                pltpu.make_async_copy(wd_hbm.at[layer, e], wst_d.at[slot], wsem.at[slot]))

    @pl.when(b == 0)
    def _():
        for cp in weight_copies(be_ref[0], par_ref[0]):
            cp.start()
        gather(0, 0)

    @pl.when(b + 1 < nact)
    def _():
        gather(b + 1, (b + 1) % 2)

    @pl.when(jnp.logical_and(b < nact, first_ref[b] == 1))
    def _():
        slot = par_ref[b]
        for cp in weight_copies(be_ref[b], slot):
            cp.wait()

        @pl.when(nxt_ref[b] >= 0)
        def _():
            for cp in weight_copies(nxt_ref[b], 1 - slot):
                cp.start()

        ck = XPOSE_CHUNK
        for j in range(D_MODEL // ck):
            wgt[:, pl.ds(j * ck, ck)] = wst_g[slot, pl.ds(j * ck, ck), :].T.astype(MXU_DTYPE)
            wut[:, pl.ds(j * ck, ck)] = wst_u[slot, pl.ds(j * ck, ck), :].T.astype(MXU_DTYPE)
            wdt[pl.ds(j * ck, ck), :] = wst_d[slot, :, pl.ds(j * ck, ck)].T.astype(MXU_DTYPE)

    @pl.when(b < nact)
    def _():
        slot = b % 2
        for r in range(rb):
            _row_copy(h_hbm, 0, xbuf.at[slot], r, gsem.at[slot]).wait()
        x = _unpack_pairs(xbuf[slot])
        hid_t = _silu(_mm_nt(wgt[...], x)) * _mm_nt(wut[...], x)
        y_ref[...] = _pack_pairs(_mm(wdt[...], hid_t).T)

    @pl.when(b >= nact)
    def _():
        y_ref[...] = jnp.zeros_like(y_ref)


def _experts(h_all, plan, w_gate, w_up, w_down, layer):
    block_e, row_token, n_active, first, nxt, par = plan
    n_blocks = block_e.shape[0]
    rb = MOE_ROWS
    any_spec = pl.BlockSpec(memory_space=pl.ANY)
    return pl.pallas_call(
        functools.partial(_experts_kernel, layer=layer),
        grid_spec=pltpu.PrefetchScalarGridSpec(
            num_scalar_prefetch=6,
            grid=(n_blocks,),
            in_specs=[any_spec, any_spec, any_spec, any_spec],
            out_specs=pl.BlockSpec((rb, D_MODEL // 2), lambda b, *_: (b, 0)),
            scratch_shapes=[pltpu.VMEM((2, rb, D_MODEL // 2), jnp.uint32), pltpu.SemaphoreType.DMA((2,)),
                            pltpu.VMEM((2, D_MODEL, D_EXPERT), F32), pltpu.VMEM((2, D_MODEL, D_EXPERT), F32),
                            pltpu.VMEM((2, D_EXPERT, D_MODEL), F32), pltpu.SemaphoreType.DMA((2,)),
                            pltpu.VMEM((D_EXPERT, D_MODEL), MXU_DTYPE), pltpu.VMEM((D_EXPERT, D_MODEL), MXU_DTYPE),
                            pltpu.VMEM((D_MODEL, D_EXPERT), MXU_DTYPE)]),
        out_shape=jax.ShapeDtypeStruct((n_blocks * rb, D_MODEL // 2), jnp.uint32),
        compiler_params=_params("arbitrary"),
        name="moe_experts",
    )(block_e, row_token, n_active, first, nxt, par, h_all, w_gate, w_up, w_down)


def _combine_kernel(pos_ref, y_hbm, x_ref, r_ref, g_ref, o_ref, buf, sem, *, tok0, tm):
    i = pl.program_id(0)
    n = pl.num_programs(0)

    def gather(tile, slot):
        for r in range(tm):
            for k in range(TOP_K):
                p = pos_ref[(tok0 + tile * tm + r) * TOP_K + k]
                _row_copy(y_hbm, p, buf.at[slot], k * tm + r, sem.at[slot]).start()

    @pl.when(i == 0)
    def _():
        gather(0, 0)

    @pl.when(i + 1 < n)
    def _():
        gather(i + 1, (i + 1) % 2)

    slot = i % 2
    for r in range(TOP_K * tm):
        _row_copy(y_hbm, 0, buf.at[slot], r, sem.at[slot]).wait()
    route = r_ref[...]
    w1, w2 = route[:, 2:3], route[:, 3:4]
    y1, y2 = buf[slot, pl.ds(0, tm), :], buf[slot, pl.ds(tm, tm), :]
    half = D_MODEL // 2
    lo = lambda w: pltpu.bitcast(lax.shift_left(w, jnp.uint32(16)), F32)
    hi = lambda w: pltpu.bitcast(w & jnp.uint32(0xFFFF0000), F32)
    for part, cols in ((lo, pl.ds(0, half)), (hi, pl.ds(half, half))):
        o_ref[:, cols] = x_ref[:, cols] + g_ref[:, cols] * (part(y1) * w1 + part(y2) * w2)


def _combine(g, tok0, pos, y_rows, x, route_all, mod):
    tm = min(COMBINE_ROWS, g.tm)
    gc = g._replace(tm=tm)
    nt = gc.nt
    assert tok0 % tm == 0
    blk0 = tok0 // tm
    if g.L == 1:
        gate_spec = pl.BlockSpec((tm, D_MODEL), lambda i, pos: (i, 5))
    else:
        gate_spec = pl.BlockSpec((None, 1, D_MODEL), lambda i, pos: (i // nt, 0, 5))
    return pl.pallas_call(
        functools.partial(_combine_kernel, tok0=tok0, tm=tm),
        grid_spec=pltpu.PrefetchScalarGridSpec(
            num_scalar_prefetch=1,
            grid=(gc.n_tiles,),
            in_specs=[pl.BlockSpec(memory_space=pl.ANY),
                      pl.BlockSpec((tm, D_MODEL), lambda i, pos: (i, 0)),
                      pl.BlockSpec((tm, LANE), lambda i, pos: (blk0 + i, 0)),
                      gate_spec],
            out_specs=pl.BlockSpec((tm, D_MODEL), lambda i, pos: (i, 0)),
            scratch_shapes=[pltpu.VMEM((2, TOP_K * tm, D_MODEL // 2), jnp.uint32), pltpu.SemaphoreType.DMA((2,))]),
        out_shape=jax.ShapeDtypeStruct((g.T, D_MODEL), F32),
        compiler_params=_params("arbitrary"),
        name="moe_combine",
    )(pos, y_rows, x, route_all, mod)


def _dispatch_plan(route_all):
    rb = MOE_ROWS
    n_tok = route_all.shape[0]
    n_assign = n_tok * TOP_K
    n_blocks = -(-(n_assign + N_EXPERTS * (rb - 1)) // rb)
    flat_e = route_all[:, :TOP_K].astype(jnp.int32).reshape(-1)
    order = jnp.argsort(flat_e).astype(jnp.int32)
    bounds = jnp.searchsorted(flat_e[order], jnp.arange(N_EXPERTS + 1, dtype=jnp.int32)).astype(jnp.int32)
    start, counts = bounds[:-1], bounds[1:] - bounds[:-1]
    rank = jnp.argsort(order).astype(jnp.int32) - start[flat_e]
    padded = (counts + rb - 1) // rb * rb
    pad_end = jnp.cumsum(padded)
    pad_start = pad_end - padded
    pos = pad_start[flat_e] + rank
    n_active = (pad_end[-1] // rb).astype(jnp.int32)
    blk = jnp.minimum(jnp.arange(n_blocks, dtype=jnp.int32), n_active - 1)
    block_e = jnp.minimum(jnp.searchsorted(pad_end, blk * rb, side='right'), N_EXPERTS - 1).astype(jnp.int32)
    row = jnp.arange(n_blocks * rb, dtype=jnp.int32)
    row_e = jnp.repeat(block_e, rb)
    idx = row - pad_start[row_e]
    src = jnp.clip(start[row_e] + idx, 0, n_assign - 1)
    row_token = jnp.where(jnp.logical_and(idx < counts[row_e], row < pad_end[-1]), order[src] // TOP_K, 0)
    bidx = jnp.arange(n_blocks, dtype=jnp.int32)
    first = jnp.logical_and(bidx < n_active, jnp.logical_or(bidx == 0, block_e != jnp.roll(block_e, 1)))
    par = (jnp.cumsum(first.astype(jnp.int32)) - 1) % 2
    nxt_first = lax.cummin(jnp.where(first, bidx, n_blocks)[::-1])[::-1]
    nxt_idx = jnp.concatenate([nxt_first[1:], jnp.full((1,), n_blocks, jnp.int32)])
    nxt = jnp.where(nxt_idx < n_blocks, block_e[jnp.minimum(nxt_idx, n_blocks - 1)], -1)
    plan = (block_e, row_token.astype(jnp.int32), n_active.reshape(1), first.astype(jnp.int32),
            nxt.astype(jnp.int32), par.astype(jnp.int32))
    return plan, pos.astype(jnp.int32)


def _layer(layer, groups, xs, mods, states, p, state_dn):
    x1s, new_states = [], []
    for g, x, mod, st in zip(groups, xs, mods, states):
        s5_re, s5_im, dn_conv, lru_h, lru_conv = st
        single = g.L == 1
        h = _norm_mod(g, x, p['norm1_g'][layer], mod, 0, 1)
        uq = _proj(g, h, p['w_pack'], layer, 0, BRANCH_W + 3 * DN_W, 1024)
        ab = _proj(g, h, p['w_pack'], layer, COL_AB, LANE, LANE)
        rest = _proj(g, h, p['w_pack'], layer, PACK_REST, REST_GATE, 512)

        ya, n_re, n_im = _s5(g, uq, s5_re, s5_im, p['s5_disc'], p['s5_cr'], p['s5_ci'], p['s5_d'],
                             p['s5_w_glu'], p['s5_b_glu'], layer)
        if single:
            qkv = uq[:, BRANCH_W:]
            lru_x = rest[:, DN_W:DN_W + LRU_W]
            yb, n_dn = _dn_step(g, uq, dn_conv, rest, ab, p['dn_conv_w'], p['dn_prm'], p['dn_norm_g'], state_dn, layer)
            n_dn_conv = jnp.concatenate([dn_conv[:, 1:], qkv[:, None, :]], axis=1)
            n_lru_conv = jnp.concatenate([lru_conv[:, 1:], lru_x[:, None, :]], axis=1)
            yc, n_lru = _lru(g, rest, lru_conv, lru_h, p, layer, pos0=p['past_len'])
        else:
            yb, n_dn = _dn_seq(g, uq, rest, ab, p['dn_conv_w'], p['dn_prm'], p['dn_norm_g'], layer)
            tail = lambda a: a.reshape(g.B, g.L, -1)[:, g.L - (CONV_W - 1):]
            n_dn_conv = tail(uq)[:, :, BRANCH_W:]
            n_lru_conv = tail(rest)[:, :, DN_W:DN_W + LRU_W]
            yc, n_lru = _lru(g, rest, None, None, p, layer, pos0=0)

        merged = _merge(g, h, ya, yb, yc, p['w_pack'], p['w_branch'], layer)
        x1 = _out_proj(g, merged, p['w_out'], layer, x, mod, 2)
        x1s.append(x1)
        new_states.append((n_re.reshape(g.B, S5_GROUPS, S5_STATE), n_im.reshape(g.B, S5_GROUPS, S5_STATE),
                           n_dn, n_dn_conv, n_lru, n_lru_conv))

    h_all, route_all = _router(groups, x1s, p['norm2_g'][layer], mods, p['w_route'], p['b_route'], layer)
    plan, pos = _dispatch_plan(route_all)
    y_rows = _experts(h_all, plan, p['w_e_gate'], p['w_e_up'], p['w_e_down'], layer)
    outs, tok0 = [], 0
    for g, x1, mod in zip(groups, x1s, mods):
        outs.append(_combine(g, tok0, pos, y_rows, x1, route_all, mod))
        tok0 += g.T
    return outs, new_states


def kernel(x_prompt, x_sample, c_prompt, c_sample, state_s5_re, state_s5_im, state_dn, state_dn_conv, state_lru, state_lru_conv, final_g, norm1_g, norm2_g, w_ada, b_ada, w_in, s5_log_dt, s5_a_re, s5_a_im, s5_b_re, s5_b_im, s5_c_re, s5_c_im, s5_d, s5_w_glu, s5_b_glu, dn_conv_w, dn_a_log, dn_dt_bias, dn_norm_g, lru_conv_w, lru_conv_b, lru_w_a, lru_b_a, lru_w_x, lru_b_x, lru_lambda, w_branch, w_out, w_rg, b_rg, w_re, b_re, w_e_gate, w_e_up, w_e_down):
    depth = w_in.shape[0]
    bp, lp, _ = x_prompt.shape
    bs, ls, _ = x_sample.shape
    assert ls == 1 and bs % SUBLANE == 0 and lp % SEQ_CHUNK == 0
    gp = _Group(bp, lp, min(lp, 1024))
    gs = _Group(bs, 1, bs)
    groups = (gp, gs)

    pad = (-bp) % SUBLANE
    c_all = jnp.concatenate([c_prompt, jnp.zeros((pad, D_MODEL), F32), c_sample], axis=0)
    mod_all = _ada_mod(c_all, w_ada, b_ada)

    prm = jnp.zeros((depth, 2, LANE), F32)
    prm = prm.at[:, 0, :DN_HEADS].set(dn_a_log).at[:, 1, :DN_HEADS].set(dn_dt_bias)
    w_route = jnp.concatenate([w_rg, w_re, jnp.zeros((depth, D_MODEL, LANE - N_GROUPS - N_EXPERTS), F32)], axis=-1)
    b_route = jnp.concatenate([b_rg, b_re, jnp.zeros((depth, LANE - N_GROUPS - N_EXPERTS), F32)], axis=-1)
    p = dict(norm1_g=norm1_g, norm2_g=norm2_g,
             w_pack=jnp.concatenate([w_in[:, :, :COL_AB + LANE],
                                     jnp.zeros((depth, D_MODEL, PACK_REST - COL_AB - LANE), F32),
                                     w_in[:, :, COL_REST:]], axis=-1).astype(MXU_DTYPE),
             s5_disc=_s5_discretize(s5_log_dt, s5_a_re, s5_a_im, s5_b_re, s5_b_im),
             s5_cr=_s5_out_blocks(s5_c_re), s5_ci=_s5_out_blocks(s5_c_im),
             s5_d=s5_d, s5_w_glu=s5_w_glu, s5_b_glu=s5_b_glu,
             dn_conv_w=dn_conv_w, dn_prm=prm, dn_norm_g=dn_norm_g,
             lru_conv_w=lru_conv_w, lru_conv_b=lru_conv_b, lru_wa_bd=_block_diag(lru_w_a), lru_b_a=lru_b_a,
             lru_wx_bd=_block_diag(lru_w_x), lru_b_x=lru_b_x, lru_lambda=lru_lambda,
             w_branch=w_branch, w_out=w_out.astype(MXU_DTYPE), w_route=w_route,
             b_route=b_route.reshape(depth, 1, LANE),
             w_e_gate=w_e_gate, w_e_up=w_e_up, w_e_down=w_e_down, past_len=PAST_LEN)

    xs = [x_prompt.reshape(gp.T, D_MODEL), x_sample.reshape(gs.T, D_MODEL)]
    per_layer = []
    for l in range(depth):
        mods = [mod_all[l, :bp].reshape(bp, 1, 6 * D_MODEL), mod_all[l, bp + pad:]]
        states = [(None, None, None, None, None),
                  (state_s5_re[l].reshape(bs, S5_N), state_s5_im[l].reshape(bs, S5_N), state_dn_conv[l],
                   state_lru[l], state_lru_conv[l])]
        xs, new_states = _layer(l, groups, xs, mods, states, p, state_dn)
        per_layer.append(new_states)

    y_prompt = _final_norm(gp, xs[0], final_g).reshape(bp, lp, D_MODEL)
    y_sample = _final_norm(gs, xs[1], final_g).reshape(bs, 1, D_MODEL)
    stack = lambda gi: tuple(jnp.stack([per_layer[l][gi][k] for l in range(depth)]) for k in range(6))
    return (y_prompt, y_sample) + stack(0) + stack(1)
```

```python
import functools
import math
from typing import NamedTuple

import jax
import jax.numpy as jnp
from jax import lax
from jax.experimental import pallas as pl
from jax.experimental.pallas import tpu as pltpu

F32 = jnp.float32
MXU_DTYPE = jnp.bfloat16

D_MODEL = 2048
BRANCH_W = D_MODEL // 4
S5_GROUP = 16
S5_GROUPS = BRANCH_W // S5_GROUP
S5_STATE = 64
S5_N = S5_GROUPS * S5_STATE
S5_BLOCKS = 4
DN_HEADS = 4
DN_HEAD_DIM = 128
DN_W = DN_HEADS * DN_HEAD_DIM
DN_CHUNK = 64
CONV_W = 4
LRU_W = BRANCH_W
LRU_BLOCKS = 4
LRU_C = 8.0
N_GROUPS = 4
EXPERTS_PER_GROUP = 8
N_EXPERTS = N_GROUPS * EXPERTS_PER_GROUP
TOP_K = 2
D_EXPERT = D_MODEL // 4
NORM_EPS = 1e-6
PAST_LEN = 16384

COL_AB = BRANCH_W + 3 * DN_W
COL_REST = COL_AB + 2 * DN_HEADS
REST_GATE = DN_W + 2 * LRU_W
PACK_REST = 2560

LANE = 128
SUBLANE = 8
VMEM_LIMIT = 56 * 1024 * 1024
SEQ_CHUNK = 256
MOE_ROWS = 256
COMBINE_ROWS = 256
XPOSE_CHUNK = 256
ROUTER_ROWS = 128


class _Group(NamedTuple):
    B: int
    L: int
    tm: int

    @property
    def T(self):
        return self.B * self.L

    @property
    def nt(self):
        return max(self.L // self.tm, 1)

    @property
    def n_tiles(self):
        return self.T // self.tm


def _params(*sem):
    return pltpu.CompilerParams(dimension_semantics=sem, vmem_limit_bytes=VMEM_LIMIT)


def _mm(a, b):
    return jnp.dot(a.astype(MXU_DTYPE), b.astype(MXU_DTYPE), preferred_element_type=F32)


def _mm_nt(a, b):
    return lax.dot_general(a.astype(MXU_DTYPE), b.astype(MXU_DTYPE), (((1,), (1,)), ((), ())),
                           preferred_element_type=F32)


def _mm_tn(a, b):
    return lax.dot_general(a.astype(MXU_DTYPE), b.astype(MXU_DTYPE), (((0,), (0,)), ((), ())),
                           preferred_element_type=F32)


def _silu(x):
    return x * jax.nn.sigmoid(x)


def _gelu_tanh(x):
    return 0.5 * x * (1.0 + jnp.tanh(math.sqrt(2.0 / math.pi) * (x + 0.044715 * (x * x * x))))


def _softplus(x):
    return jnp.maximum(x, 0.0) + jnp.log1p(jnp.exp(-jnp.abs(x)))


def _rms(x, gain):
    return x * lax.rsqrt(jnp.mean(x * x, axis=-1, keepdims=True) + NORM_EPS) * gain


def _row_iota(shape):
    return lax.broadcasted_iota(jnp.int32, shape, len(shape) - 2)


def _last_row(x):
    return jnp.broadcast_to(x[SUBLANE - 1:SUBLANE], x.shape)


def _nat_spec(g, tn, col0=0):
    return pl.BlockSpec((g.tm, tn), lambda i, j: (i, col0 + j))


def _mod_spec(g, k, tn):
    nb, nt = D_MODEL // tn, g.nt
    if g.L == 1:
        return pl.BlockSpec((g.tm, tn), lambda i, j: (i, k * nb + j))
    return pl.BlockSpec((None, 1, tn), lambda i, j: (i // nt, 0, k * nb + j))


def _ada_kernel(c_ref, w_ref, b_ref, o_ref):
    o_ref[...] = _mm(_silu(c_ref[...]), w_ref[...]) + b_ref[...]


def _ada_mod(c_all, w_ada, b_ada):
    depth, _, n6 = w_ada.shape
    rows = c_all.shape[0]
    tn = 1024
    return pl.pallas_call(
        _ada_kernel,
        grid=(depth, n6 // tn),
        in_specs=[pl.BlockSpec((rows, D_MODEL), lambda l, j: (0, 0)),
                  pl.BlockSpec((None, D_MODEL, tn), lambda l, j: (l, 0, j)),
                  pl.BlockSpec((None, 1, tn), lambda l, j: (l, 0, j))],
        out_specs=pl.BlockSpec((None, rows, tn), lambda l, j: (l, 0, j)),
        out_shape=jax.ShapeDtypeStruct((depth, rows, n6), F32),
        compiler_params=_params("arbitrary", "arbitrary"),
        name="ada_mod",
    )(c_all, w_ada, b_ada.reshape(depth, 1, n6))


def _norm_mod_kernel(x_ref, g_ref, sc_ref, sh_ref, o_ref):
    y = _rms(x_ref[...], g_ref[...])
    o_ref[...] = (y * (1.0 + sc_ref[...]) + sh_ref[...]).astype(o_ref.dtype)


def _norm_mod(g, x, gain, mod, k_shift, k_scale):
    gm = g._replace(tm=min(g.tm, 512))
    return pl.pallas_call(
        _norm_mod_kernel,
        grid=(gm.n_tiles, 1),
        in_specs=[_nat_spec(gm, D_MODEL),
                  pl.BlockSpec((1, D_MODEL), lambda i, j: (0, 0)),
                  _mod_spec(gm, k_scale, D_MODEL),
                  _mod_spec(gm, k_shift, D_MODEL)],
        out_specs=_nat_spec(gm, D_MODEL),
        out_shape=jax.ShapeDtypeStruct((g.T, D_MODEL), MXU_DTYPE),
        compiler_params=_params("arbitrary", "arbitrary"),
        name="norm_mod",
    )(x, gain.reshape(1, D_MODEL), mod, mod)


def _final_norm_kernel(x_ref, g_ref, o_ref):
    o_ref[...] = _rms(x_ref[...], g_ref[...])


def _final_norm(g, x, gain):
    gm = g._replace(tm=min(g.tm, 512))
    return pl.pallas_call(
        _final_norm_kernel,
        grid=(gm.n_tiles, 1),
        in_specs=[_nat_spec(gm, D_MODEL), pl.BlockSpec((1, D_MODEL), lambda i, j: (0, 0))],
        out_specs=_nat_spec(gm, D_MODEL),
        out_shape=jax.ShapeDtypeStruct((g.T, D_MODEL), F32),
        compiler_params=_params("arbitrary", "arbitrary"),
        name="final_norm",
    )(x, gain.reshape(1, D_MODEL))


def _proj_kernel(a_ref, w_ref, o_ref):
    o_ref[...] = _mm(a_ref[...], w_ref[...]).astype(o_ref.dtype)


def _proj(g, a, w, layer, col0, n, tn):
    cb = col0 // tn
    return pl.pallas_call(
        _proj_kernel,
        grid=(g.n_tiles, n // tn),
        in_specs=[pl.BlockSpec((g.tm, D_MODEL), lambda i, j: (i, 0)),
                  pl.BlockSpec((None, D_MODEL, tn), lambda i, j: (layer, 0, cb + j))],
        out_specs=_nat_spec(g, tn),
        out_shape=jax.ShapeDtypeStruct((g.T, n), F32),
        compiler_params=_params("arbitrary", "arbitrary"),
        name="in_proj",
    )(a, w)


def _out_proj_kernel(a_ref, w_ref, x_ref, g_ref, o_ref):
    o_ref[...] = x_ref[...] + g_ref[...] * _mm(a_ref[...], w_ref[...])


def _out_proj(g, merged, w_out, layer, x, mod, k_gate):
    tn = 512
    return pl.pallas_call(
        _out_proj_kernel,
        grid=(g.n_tiles, D_MODEL // tn),
        in_specs=[pl.BlockSpec((g.tm, D_MODEL), lambda i, j: (i, 0)),
                  pl.BlockSpec((None, D_MODEL, tn), lambda i, j: (layer, 0, j)),
                  _nat_spec(g, tn),
                  _mod_spec(g, k_gate, tn)],
        out_specs=_nat_spec(g, tn),
        out_shape=jax.ShapeDtypeStruct((g.T, D_MODEL), F32),
        compiler_params=_params("arbitrary", "arbitrary"),
        name="out_proj",
    )(merged, w_out, x, mod)


def _merge_kernel(h_ref, ya_ref, yb_ref, yc_ref, wg0, wg1, wg2, wb0, wb1, wb2, o_ref):
    h = h_ref[...]
    acc = None
    for y_ref, wg, wb in ((ya_ref, wg0, wb0), (yb_ref, wg1, wb1), (yc_ref, wg2, wb2)):
        t = jax.nn.sigmoid(_mm(h, wg[...])) * _mm(y_ref[...], wb[...])
        acc = t if acc is None else acc + t
    o_ref[...] = acc.astype(o_ref.dtype)


def _merge(g, h, ya, yb, yc, w_pack, w_branch, layer):
    tn = 256
    gate_specs = [pl.BlockSpec((None, D_MODEL, tn),
                               functools.partial(lambda i, j, c: (layer, 0, c + j), c=(PACK_REST + REST_GATE + n * D_MODEL) // tn))
                  for n in range(3)]
    br_specs = [pl.BlockSpec((None, None, BRANCH_W, tn), functools.partial(lambda i, j, n: (layer, n, 0, j), n=n))
                for n in range(3)]
    y_spec = pl.BlockSpec((g.tm, BRANCH_W), lambda i, j: (i, 0))
    return pl.pallas_call(
        _merge_kernel,
        grid=(g.n_tiles, D_MODEL // tn),
        in_specs=[pl.BlockSpec((g.tm, D_MODEL), lambda i, j: (i, 0)), y_spec, y_spec, y_spec] + gate_specs + br_specs,
        out_specs=_nat_spec(g, tn),
        out_shape=jax.ShapeDtypeStruct((g.T, D_MODEL), MXU_DTYPE),
        compiler_params=_params("arbitrary", "arbitrary"),
        name="merge",
    )(h, ya, yb, yc, w_pack, w_pack, w_pack, w_branch, w_branch, w_branch)


def _s5_disc_kernel(ldt_ref, ar_ref, ai_ref, br_ref, bi_ref, lr_ref, li_ref, bbr_ref, bbi_ref):
    dt = jnp.exp(ldt_ref[...])
    ar, ai = ar_ref[...], ai_ref[...]
    mag = jnp.exp(ar * dt)
    lr = mag * jnp.cos(ai * dt)
    li = mag * jnp.sin(ai * dt)
    den = ar * ar + ai * ai
    fr = ((lr - 1.0) * ar + li * ai) / den
    fi = (li * ar - (lr - 1.0) * ai) / den
    lr_ref[...] = lr
    li_ref[...] = li
    br, bi = br_ref[...], bi_ref[...]
    bbr_ref[...] = fr * br - fi * bi
    bbi_ref[...] = fr * bi + fi * br


def _s5_discretize(log_dt, a_re, a_im, b_re, b_im):
    depth = log_dt.shape[0]
    n = depth * S5_GROUPS
    rows = n * S5_GROUP
    rep = lambda v, w: jnp.repeat(v.reshape(n, w), S5_GROUP, axis=0)
    brt = jnp.swapaxes(b_re, -1, -2).reshape(rows, S5_STATE)
    bit = jnp.swapaxes(b_im, -1, -2).reshape(rows, S5_STATE)
    sds = jax.ShapeDtypeStruct
    lr, li, bbr, bbi = pl.pallas_call(
        _s5_disc_kernel,
        out_shape=tuple(sds((rows, S5_STATE), F32) for _ in range(4)),
        name="s5_discretize",
    )(rep(log_dt, 1), rep(a_re, S5_STATE), rep(a_im, S5_STATE), brt, bit)
    gpb = S5_GROUPS // S5_BLOCKS
    eye = jnp.eye(gpb, dtype=F32)

    def blocks_in(m):
        m = m.reshape(depth, S5_BLOCKS, gpb, S5_GROUP, S5_STATE)
        m = jnp.einsum('ljgnp,gh->ljgnhp', m, eye)
        return m.reshape(depth, S5_BLOCKS, gpb * S5_GROUP, gpb * S5_STATE).astype(MXU_DTYPE)

    lam = lambda v: v.reshape(n, S5_GROUP, S5_STATE)[:, 0].reshape(depth, 1, S5_N)
    return lam(lr), lam(li), blocks_in(bbr), blocks_in(bbi)


def _s5_out_blocks(c):
    depth = c.shape[0]
    gpb = S5_GROUPS // S5_BLOCKS
    eye = jnp.eye(gpb, dtype=F32)
    m = jnp.einsum('ljgnp,gh->ljgphn', c.reshape(depth, S5_BLOCKS, gpb, S5_GROUP, S5_STATE), eye)
    return m.reshape(depth, S5_BLOCKS, gpb * S5_STATE, gpb * S5_GROUP).astype(MXU_DTYPE)


def _s5_in(u, bb_ref):
    cw, sw = BRANCH_W // S5_BLOCKS, S5_N // S5_BLOCKS
    return [_mm(u[:, j * cw:(j + 1) * cw], bb_ref[j]) for j in range(S5_BLOCKS)], sw


def _s5_out(hr, hi, u, cr_ref, ci_ref, d_ref, wglu_ref, bglu_ref):
    sw = S5_N // S5_BLOCKS
    y = jnp.concatenate([_mm(hr[:, j * sw:(j + 1) * sw], cr_ref[j]) - _mm(hi[:, j * sw:(j + 1) * sw], ci_ref[j])
                         for j in range(S5_BLOCKS)], axis=1)
    y = _gelu_tanh(y + d_ref[...] * u)
    return y * jax.nn.sigmoid(_mm(y, wglu_ref[...]) + bglu_ref[...])


def _scan_tiles(n_tiles, cw, width, tile_fn, load_state, store_state):
    for c in range(width // cw):
        cs = pl.ds(c * cw, cw)

        def body(k, carry, cs=cs):
            return tile_fn(pl.ds(pl.multiple_of(k * SUBLANE, SUBLANE), SUBLANE), cs, carry)

        store_state(cs, lax.fori_loop(0, n_tiles, body, load_state(cs)))


def _split3(x):
    hi = x.astype(MXU_DTYPE)
    r1 = x - hi.astype(F32)
    mid = r1.astype(MXU_DTYPE)
    return hi, mid, (r1 - mid.astype(F32)).astype(MXU_DTYPE)


def _s5_seq_kernel(u_ref, bbr_ref, bbi_ref, lr_ref, li_ref, cr_ref, ci_ref, d_ref, wglu_ref, bglu_ref,
                   y_ref, hlr_ref, hli_ref, xr_s, xi_s, cr_s, ci_s, pw_s):
    c = pl.program_id(1)
    lc = u_ref.shape[0]
    seg = lc // SUBLANE
    assert seg & (seg - 1) == 0

    @pl.when(c == 0)
    def _():
        cr_s[...] = jnp.zeros_like(cr_s)
        ci_s[...] = jnp.zeros_like(ci_s)
        lr, li = lr_ref[...], li_ref[...]
        for _ in range(seg.bit_length() - 1):
            lr, li = lr * lr - li * li, 2.0 * lr * li
        row = _row_iota((SUBLANE, S5_N))
        pr, pi = lr, li
        acc_r = jnp.zeros((SUBLANE, S5_N), F32)
        acc_i = jnp.zeros((SUBLANE, S5_N), F32)
        for r in range(SUBLANE):
            acc_r = jnp.where(row == r, pr, acc_r)
            acc_i = jnp.where(row == r, pi, acc_i)
            if r + 1 in (1, 2, 4):
                k = (1, 2, 4).index(r + 1)
                pw_s[2 * k] = jnp.where(row >= r + 1, pr, 0.0)
                pw_s[2 * k + 1] = jnp.where(row >= r + 1, pi, 0.0)
            pr, pi = pr * lr - pi * li, pr * li + pi * lr
        pw_s[6] = acc_r
        pw_s[7] = acc_i

    n_i = lax.broadcasted_iota(jnp.int32, (lc, lc), 0)
    t_i = lax.broadcasted_iota(jnp.int32, (lc, lc), 1)
    sh = seg.bit_length() - 1
    perm = (t_i == (n_i & (SUBLANE - 1)) * seg + lax.shift_right_logical(n_i, 3)).astype(MXU_DTYPE)
    unperm = (t_i == (n_i & (seg - 1)) * SUBLANE + lax.shift_right_logical(n_i, sh)).astype(MXU_DTYPE)

    u = sum(_mm(perm, t) for t in _split3(u_ref[...]))
    xr, sw = _s5_in(u, bbr_ref)
    xi, _ = _s5_in(u, bbi_ref)
    for j in range(S5_BLOCKS):
        xr_s[:, pl.ds(j * sw, sw)] = xr[j]
        xi_s[:, pl.ds(j * sw, sw)] = xi[j]

    cw = 512
    for cc in range(S5_N // cw):
        cs = pl.ds(cc * cw, cw)
        lr, li = lr_ref[:, cs], li_ref[:, cs]

        def tile_rows(k):
            return pl.ds(pl.multiple_of(k * SUBLANE, SUBLANE), SUBLANE)

        def pass1(k, carry, cs=cs, lr=lr, li=li):
            gr, gi = carry
            rows = tile_rows(k)
            gr, gi = lr * gr - li * gi + xr_s[rows, cs], lr * gi + li * gr + xi_s[rows, cs]
            xr_s[rows, cs] = gr
            xi_s[rows, cs] = gi
            return gr, gi

        zero = jnp.zeros((SUBLANE, cw), F32)
        sr, si = lax.fori_loop(0, seg, pass1, (zero, zero))

        for k, s in enumerate((1, 2, 4)):
            ar, ai = pw_s[2 * k, :, cs], pw_s[2 * k + 1, :, cs]
            qr, qi = pltpu.roll(sr, s, 0), pltpu.roll(si, s, 0)
            sr, si = sr + ar * qr - ai * qi, si + ar * qi + ai * qr
        ar, ai = pw_s[6, :, cs], pw_s[7, :, cs]
        c0r, c0i = cr_s[:, cs], ci_s[:, cs]
        sr, si = sr + ar * c0r - ai * c0i, si + ar * c0i + ai * c0r
        first = _row_iota((SUBLANE, cw)) == 0
        jr = jnp.where(first, c0r, pltpu.roll(sr, 1, 0))
        ji = jnp.where(first, c0i, pltpu.roll(si, 1, 0))
        cr_s[:, cs] = _last_row(sr)
        ci_s[:, cs] = _last_row(si)

        def pass2(k, carry, cs=cs, lr=lr, li=li):
            jr, ji = carry
            rows = tile_rows(k)
            jr, ji = lr * jr - li * ji, lr * ji + li * jr
            xr_s[rows, cs] = xr_s[rows, cs] + jr
            xi_s[rows, cs] = xi_s[rows, cs] + ji
            return jr, ji

        lax.fori_loop(0, seg, pass2, (jr, ji))

    y = _s5_out(xr_s[...], xi_s[...], u, cr_ref, ci_ref, d_ref, wglu_ref, bglu_ref).astype(y_ref.dtype)
    y_ref[...] = _mm(unperm, y).astype(y_ref.dtype)

    @pl.when(c == pl.num_programs(1) - 1)
    def _():
        hlr_ref[...] = cr_s[pl.ds(0, 1), :]
        hli_ref[...] = ci_s[pl.ds(0, 1), :]


def _s5_step_kernel(u_ref, h0r_ref, h0i_ref, bbr_ref, bbi_ref, lr_ref, li_ref, cr_ref, ci_ref, d_ref, wglu_ref,
                    bglu_ref, y_ref, hlr_ref, hli_ref):
    u = u_ref[...]
    xr, _ = _s5_in(u, bbr_ref)
    xi, _ = _s5_in(u, bbi_ref)
    lr, li, sr, si = lr_ref[...], li_ref[...], h0r_ref[...], h0i_ref[...]
    hr = lr * sr - li * si + jnp.concatenate(xr, axis=1)
    hi = lr * si + li * sr + jnp.concatenate(xi, axis=1)
    hlr_ref[...] = hr
    hli_ref[...] = hi
    y_ref[...] = _s5_out(hr, hi, u, cr_ref, ci_ref, d_ref, wglu_ref, bglu_ref).astype(y_ref.dtype)


def _s5(g, uq, h0r, h0i, disc, cr, ci, d, w_glu, b_glu, layer):
    lam_r, lam_i, bbr, bbi = disc
    cw, sw = BRANCH_W // S5_BLOCKS, S5_N // S5_BLOCKS
    sds = jax.ShapeDtypeStruct
    seq = g.L > 1
    lc = min(g.L, SEQ_CHUNK) if seq else g.B
    nc = g.L // lc if seq else 1
    grid = (g.B, nc) if seq else (1, 1)
    row = (lambda b, c: (b * nc + c, 0))
    lay3 = lambda b, c: (layer, 0, 0)
    lay4 = lambda b, c: (layer, 0, 0, 0)
    w_specs = [pl.BlockSpec((None, S5_BLOCKS, cw, sw), lay4), pl.BlockSpec((None, S5_BLOCKS, cw, sw), lay4),
               pl.BlockSpec((None, 1, S5_N), lay3), pl.BlockSpec((None, 1, S5_N), lay3),
               pl.BlockSpec((None, S5_BLOCKS, sw, cw), lay4), pl.BlockSpec((None, S5_BLOCKS, sw, cw), lay4),
               pl.BlockSpec((None, 1, BRANCH_W), lay3),
               pl.BlockSpec((None, BRANCH_W, BRANCH_W), lay3),
               pl.BlockSpec((None, 1, BRANCH_W), lay3)]
    w_args = (bbr, bbi, lam_r, lam_i, cr, ci, d.reshape(-1, 1, BRANCH_W), w_glu, b_glu.reshape(-1, 1, BRANCH_W))
    if seq:
        st_spec = pl.BlockSpec((None, 1, S5_N), lambda b, c: (b, 0, 0))
        y, hlr, hli = pl.pallas_call(
            _s5_seq_kernel,
            grid=grid,
            in_specs=[pl.BlockSpec((lc, BRANCH_W), row)] + w_specs,
            out_specs=(pl.BlockSpec((lc, BRANCH_W), row), st_spec, st_spec),
            out_shape=(sds((g.T, BRANCH_W), MXU_DTYPE), sds((g.B, 1, S5_N), F32), sds((g.B, 1, S5_N), F32)),
            scratch_shapes=[pltpu.VMEM((lc, S5_N), F32), pltpu.VMEM((lc, S5_N), F32),
                            pltpu.VMEM((SUBLANE, S5_N), F32), pltpu.VMEM((SUBLANE, S5_N), F32),
                            pltpu.VMEM((8, SUBLANE, S5_N), F32)],
            compiler_params=_params("arbitrary", "arbitrary"),
            name="s5_ssm",
        )(uq, *w_args)
        return y, hlr.reshape(g.B, S5_N), hli.reshape(g.B, S5_N)
    st_spec = pl.BlockSpec((g.B, S5_N), lambda b, c: (0, 0))
    return pl.pallas_call(
        _s5_step_kernel,
        grid=grid,
        in_specs=[pl.BlockSpec((lc, BRANCH_W), row), st_spec, st_spec] + w_specs,
        out_specs=(pl.BlockSpec((lc, BRANCH_W), row), st_spec, st_spec),
        out_shape=(sds((g.T, BRANCH_W), MXU_DTYPE), sds((g.B, S5_N), F32), sds((g.B, S5_N), F32)),
        compiler_params=_params("arbitrary", "arbitrary"),
        name="s5_ssm",
    )(uq, h0r, h0i, *w_args)


def _lru_gates(xc, wa_ref, ba_ref, wx_ref, bx_ref, lam_ref):
    r = jax.nn.sigmoid(_mm(xc, wa_ref[...]) + ba_ref[...])
    ig = jax.nn.sigmoid(_mm(xc, wx_ref[...]) + bx_ref[...])
    log_a = -LRU_C * r * _softplus(-lam_ref[...])
    return jnp.exp(log_a), jnp.sqrt(1.0 - jnp.exp(2.0 * log_a)), ig


def _lru_seq_kernel(x_ref, g_ref, cw_ref, cbias_ref, wa_ref, ba_ref, wx_ref, bx_ref, lam_ref,
                    y_ref, hl_ref, xbuf, a_s, b_s, h_s):
    c = pl.program_id(1)
    lc = x_ref.shape[0]
    pad = SUBLANE

    @pl.when(c == 0)
    def _():
        xbuf[pl.ds(0, pad), :] = jnp.zeros((pad, LRU_W), F32)
        h_s[...] = jnp.zeros_like(h_s)

    xbuf[pl.ds(pad, lc), :] = x_ref[...]
    xc = cbias_ref[...]
    for j in range(CONV_W):
        xc = xc + cw_ref[pl.ds(j, 1), :] * xbuf[pl.ds(pad - (CONV_W - 1) + j, lc), :]
    new_tail = xbuf[pl.ds(lc, pad), :]
    xbuf[pl.ds(0, pad), :] = new_tail

    a, mult, ig = _lru_gates(xc, wa_ref, ba_ref, wx_ref, bx_ref, lam_ref)
    first = jnp.logical_and(c == 0, _row_iota((lc, LRU_W)) == 0)
    a_s[...] = jnp.where(first, 0.0, a)
    b_s[...] = jnp.where(first, 1.0, mult) * (ig * xc)

    def tile(rows, cs, h0):
        a, b = a_s[rows, cs], b_s[rows, cs]
        row = _row_iota(a.shape)
        for s in (1, 2, 4):
            keep = row >= s
            b = a * jnp.where(keep, pltpu.roll(b, s, 0), 0.0) + b
            a = a * jnp.where(keep, pltpu.roll(a, s, 0), 1.0)
        h = a * h0 + b
        b_s[rows, cs] = h
        return _last_row(h)

    def store(cs, h):
        h_s[:, cs] = h

    _scan_tiles(lc // SUBLANE, LRU_W, LRU_W, tile, lambda cs: h_s[:, cs], store)
    y_ref[...] = (b_s[...] * _gelu_tanh(g_ref[...])).astype(y_ref.dtype)

    @pl.when(c == pl.num_programs(1) - 1)
    def _():
        hl_ref[...] = h_s[pl.ds(0, 1), :]


def _lru_step_kernel(x_ref, g_ref, cs_ref, h0_ref, cw_ref, cbias_ref, wa_ref, ba_ref, wx_ref, bx_ref, lam_ref,
                     y_ref, hl_ref):
    x = x_ref[...]
    xc = cbias_ref[...] + cw_ref[pl.ds(CONV_W - 1, 1), :] * x
    for j in range(CONV_W - 1):
        xc = xc + cw_ref[pl.ds(j, 1), :] * cs_ref[j]
    a, mult, ig = _lru_gates(xc, wa_ref, ba_ref, wx_ref, bx_ref, lam_ref)
    h = a * h0_ref[...] + mult * (ig * xc)
    hl_ref[...] = h
    y_ref[...] = (h * _gelu_tanh(g_ref[...])).astype(y_ref.dtype)


def _block_diag(w):
    depth, nb, k, _ = w.shape
    eye = jnp.eye(nb, dtype=w.dtype)
    return jnp.einsum('lnwv,nm->lnwmv', w, eye).reshape(depth, nb * k, nb * k).astype(MXU_DTYPE)


def _lru(g, rest, conv_state, h0, p, layer, pos0):
    sds = jax.ShapeDtypeStruct
    seq = g.L > 1
    lc = min(g.L, SEQ_CHUNK) if seq else g.B
    nc = g.L // lc if seq else 1
    lay3 = lambda b, c: (layer, 0, 0)
    vec = lambda v: v.reshape(-1, 1, LRU_W)
    w_specs = [pl.BlockSpec((None, CONV_W, LRU_W), lay3), pl.BlockSpec((None, 1, LRU_W), lay3),
               pl.BlockSpec((None, LRU_W, LRU_W), lay3), pl.BlockSpec((None, 1, LRU_W), lay3),
               pl.BlockSpec((None, LRU_W, LRU_W), lay3), pl.BlockSpec((None, 1, LRU_W), lay3),
               pl.BlockSpec((None, 1, LRU_W), lay3)]
    w_args = (p['lru_conv_w'], vec(p['lru_conv_b']), p['lru_wa_bd'], vec(p['lru_b_a']), p['lru_wx_bd'],
              vec(p['lru_b_x']), vec(p['lru_lambda']))
    x_spec = pl.BlockSpec((lc, LRU_W), lambda b, c: (b * nc + c, 1))
    g_spec = pl.BlockSpec((lc, LRU_W), lambda b, c: (b * nc + c, 2))
    y_spec = pl.BlockSpec((lc, LRU_W), lambda b, c: (b * nc + c, 0))
    if seq:
        assert pos0 == 0
        y, hl = pl.pallas_call(
            _lru_seq_kernel,
            grid=(g.B, nc),
            in_specs=[x_spec, g_spec] + w_specs,
            out_specs=(y_spec, pl.BlockSpec((None, 1, LRU_W), lambda b, c: (b, 0, 0))),
            out_shape=(sds((g.T, LRU_W), MXU_DTYPE), sds((g.B, 1, LRU_W), F32)),
            scratch_shapes=[pltpu.VMEM((lc + SUBLANE, LRU_W), F32), pltpu.VMEM((lc, LRU_W), F32),
                            pltpu.VMEM((lc, LRU_W), F32), pltpu.VMEM((SUBLANE, LRU_W), F32)],
            compiler_params=_params("arbitrary", "arbitrary"),
            name="rg_lru",
        )(rest, rest, *w_args)
        return y, hl.reshape(g.B, LRU_W)
    assert pos0 > 0
    st_spec = pl.BlockSpec((g.B, LRU_W), lambda b, c: (0, 0))
    return pl.pallas_call(
        _lru_step_kernel,
        grid=(1, 1),
        in_specs=[x_spec, g_spec, pl.BlockSpec((CONV_W - 1, g.B, LRU_W), lambda b, c: (0, 0, 0)), st_spec] + w_specs,
        out_specs=(y_spec, st_spec),
        out_shape=(sds((g.T, LRU_W), MXU_DTYPE), sds((g.B, LRU_W), F32)),
        compiler_params=_params("arbitrary", "arbitrary"),
        name="rg_lru",
    )(rest, rest, jnp.swapaxes(conv_state, 0, 1), h0, *w_args)


def _l2norm(x):
    return x * lax.rsqrt(jnp.sum(x * x, axis=-1, keepdims=True) + NORM_EPS)


def _dn_gates(ab, prm):
    g = -jnp.exp(prm[0:1, :]) * _softplus(ab + prm[1:2, :])
    return g, jax.nn.sigmoid(ab)


def _dn_seq_kernel(q_ref, k_ref, v_ref, z_ref, ab_ref, cw_ref, prm_ref, ng_ref, y_ref, sfin_ref, xbuf, s_s):
    c = pl.program_id(0)
    bsz, cs, _ = q_ref.shape
    hd = DN_HEAD_DIM
    rows = bsz * cs
    pad = SUBLANE

    @pl.when(c == 0)
    def _():
        xbuf[:, pl.ds(0, pad), :] = jnp.zeros((bsz, pad, 3 * DN_W), F32)
        s_s[...] = jnp.zeros_like(s_s)

    for s, ref in enumerate((q_ref, k_ref, v_ref)):
        xbuf[:, pl.ds(pad, cs), pl.ds(s * DN_W, DN_W)] = ref[...]
    conv = None
    for j in range(CONV_W):
        t = cw_ref[pl.ds(j, 1), :] * xbuf[:, pl.ds(pad - (CONV_W - 1) + j, cs), :]
        conv = t if conv is None else conv + t
    new_tail = xbuf[:, pl.ds(cs, pad), :]
    xbuf[:, pl.ds(0, pad), :] = new_tail
    act = _silu(conv).reshape(rows, 3 * DN_W)

    glog, beta_all = _dn_gates(ab_ref[...].reshape(rows, LANE), prm_ref[...])
    row = lax.broadcasted_iota(jnp.int32, (rows, rows), 0)
    col = lax.broadcasted_iota(jnp.int32, (rows, rows), 1)
    shift = cs.bit_length() - 1
    same = lax.shift_right_logical(row, shift) == lax.shift_right_logical(col, shift)
    causal = jnp.logical_and(same, row >= col)
    strict = jnp.logical_and(same, row > col)
    gc = jnp.dot(causal.astype(F32), glog, precision=lax.Precision.HIGHEST, preferred_element_type=F32)
    gct = gc.T
    eye = (row == col).astype(F32)
    ng = ng_ref[...]
    rblk = lax.shift_right_logical(lax.broadcasted_iota(jnp.int32, (rows, bsz * hd), 0), shift)
    cblk = lax.shift_right_logical(lax.broadcasted_iota(jnp.int32, (rows, bsz * hd), 1), hd.bit_length() - 1)
    own = rblk == cblk

    def diag_blocks(m):
        return jnp.concatenate([m[b * cs:(b + 1) * cs, b * hd:(b + 1) * hd] for b in range(bsz)], axis=0)

    for h in range(DN_HEADS):
        q = _l2norm(act[:, h * hd:(h + 1) * hd]) * (hd ** -0.5)
        k = _l2norm(act[:, DN_W + h * hd:DN_W + (h + 1) * hd])
        v = act[:, 2 * DN_W + h * hd:2 * DN_W + (h + 1) * hd]
        beta = beta_all[:, DN_HEADS + h:DN_HEADS + h + 1]
        gcol = gc[:, h:h + 1]
        decay = jnp.exp(jnp.where(causal, gcol - gct[h:h + 1, :], -jnp.inf))
        kb = k * beta
        kq = _mm_nt(jnp.concatenate([kb, q], axis=0), k)
        nmat = jnp.where(strict, -(kq[:rows] * decay), 0.0)
        att = jnp.where(causal, kq[rows:] * decay, 0.0)
        rinv = eye + nmat
        xpow = _mm(nmat, nmat)
        for _ in range(cs.bit_length() - 3):
            st = _mm(jnp.concatenate([xpow, rinv], axis=0), xpow)
            xpow, rinv = st[:rows], rinv + st[rows:]
        rinv = rinv + _mm(rinv, xpow)
        eg = jnp.exp(gcol)
        uw = _mm(rinv, jnp.concatenate([v * beta, kb * eg], axis=1))
        u, w = uw[:, :hd], uw[:, hd:]
        s = s_s[h]
        ws_qs = _mm(jnp.concatenate([w, q * eg], axis=0), s)
        v_new = u - diag_blocks(ws_qs[:rows])
        o = diag_blocks(ws_qs[rows:]) + _mm(att, v_new)
        glast = [gc[(b + 1) * cs - 1:(b + 1) * cs, h:h + 1] for b in range(bsz)]
        glast_rows = jnp.concatenate([jnp.broadcast_to(t, (cs, 1)) for t in glast], axis=0)
        glast_lanes = jnp.concatenate([jnp.broadcast_to(t, (1, hd)) for t in glast], axis=1)
        v_bd = jnp.where(own, jnp.concatenate([v_new] * bsz, axis=1), 0.0)
        s_s[h] = s * jnp.exp(glast_lanes) + _mm_tn(k * jnp.exp(glast_rows - gcol), v_bd)
        o = _rms(o, ng) * _silu(z_ref[:, :, pl.ds(h * hd, hd)].reshape(rows, hd))
        y_ref[:, :, pl.ds(h * hd, hd)] = o.reshape(bsz, cs, hd).astype(y_ref.dtype)

    @pl.when(c == pl.num_programs(0) - 1)
    def _():
        for b in range(bsz):
            for h in range(DN_HEADS):
                sfin_ref[b, h] = s_s[h, :, pl.ds(b * hd, hd)]


def _dn_seq(g, uq, rest, ab, conv_w, prm, norm_g, layer):
    cs = DN_CHUNK
    assert cs & (cs - 1) == 0 and g.L % cs == 0
    nc = g.L // cs
    hd = DN_HEAD_DIM
    lay3 = lambda c: (layer, 0, 0)
    sds = jax.ShapeDtypeStruct
    sec = lambda s: pl.BlockSpec((g.B, cs, DN_W), lambda c: (0, c, s))
    y, sfin = pl.pallas_call(
        _dn_seq_kernel,
        grid=(nc,),
        in_specs=[sec(1), sec(2), sec(3), sec(0),
                  pl.BlockSpec((g.B, cs, LANE), lambda c: (0, c, 0)),
                  pl.BlockSpec((None, CONV_W, 3 * DN_W), lay3),
                  pl.BlockSpec((None, 2, LANE), lay3),
                  pl.BlockSpec((None, 1, hd), lay3)],
        out_specs=(pl.BlockSpec((g.B, cs, DN_W), lambda c: (0, c, 0)),
                   pl.BlockSpec((g.B, DN_HEADS, hd, hd), lambda c: (0, 0, 0, 0))),
        out_shape=(sds((g.B, g.L, DN_W), MXU_DTYPE), sds((g.B, DN_HEADS, hd, hd), F32)),
        scratch_shapes=[pltpu.VMEM((g.B, cs + SUBLANE, 3 * DN_W), F32),
                        pltpu.VMEM((DN_HEADS, hd, g.B * hd), F32)],
        compiler_params=_params("arbitrary"),
        name="deltanet_seq",
    )(*(3 * [uq.reshape(g.B, g.L, -1)]), rest.reshape(g.B, g.L, -1), ab.reshape(g.B, g.L, LANE), conv_w, prm,
      norm_g.reshape(-1, 1, hd))
    return y.reshape(g.T, DN_W), sfin


def _dn_step_kernel(q_ref, k_ref, v_ref, cq_ref, ck_ref, cv_ref, z_ref, ab_ref, cwq_ref, cwk_ref, cwv_ref,
                    prm_ref, ng_ref, s_ref, y_ref, so_ref, o_s, *, bsz):
    h = pl.program_id(0)
    hd = DN_HEAD_DIM

    def conv(x_ref, c_ref, w_ref):
        acc = w_ref[pl.ds(CONV_W - 1, 1), :] * x_ref[...]
        for j in range(CONV_W - 1):
            acc = acc + w_ref[pl.ds(j, 1), :] * c_ref[j]
        return _silu(acc)

    q = _l2norm(conv(q_ref, cq_ref, cwq_ref)) * (hd ** -0.5)
    k = _l2norm(conv(k_ref, ck_ref, cwk_ref))
    v = conv(v_ref, cv_ref, cwv_ref)
    glog, beta_all = _dn_gates(ab_ref[...], prm_ref[...])
    lane = lax.broadcasted_iota(jnp.int32, (bsz, LANE), 1)
    eg = jnp.exp(jnp.sum(jnp.where(lane == h, glog, 0.0), axis=-1, keepdims=True))
    beta = jnp.sum(jnp.where(lane == DN_HEADS + h, beta_all, 0.0), axis=-1, keepdims=True)
    att = jnp.sum(q * k, axis=-1, keepdims=True)
    kt = k.T
    qt = q.T
    for b in range(bsz):
        s = s_ref[b]
        kcol = kt[:, b:b + 1]
        ks = jnp.sum(s * kcol, axis=0, keepdims=True)
        qs = jnp.sum(s * qt[:, b:b + 1], axis=0, keepdims=True)
        eg_b, beta_b = eg[b:b + 1, :], beta[b:b + 1, :]
        v_new = beta_b * v[b:b + 1, :] - (beta_b * eg_b) * ks
        o_s[pl.ds(b, 1), :] = eg_b * qs + att[b:b + 1, :] * v_new
        so_ref[b] = s * eg_b + kcol * v_new
    y_ref[...] = (_rms(o_s[...], ng_ref[...]) * _silu(z_ref[...])).astype(y_ref.dtype)


def _dn_step(g, uq, conv_state, rest, ab, conv_w, prm, norm_g, s0_all, layer):
    bsz = g.B
    hd = DN_HEAD_DIM
    conv_t = jnp.swapaxes(conv_state, 0, 1)
    sec = lambda s: (lambda h: (0, (s + 1) * DN_HEADS + h))
    sec3 = lambda s: (lambda h: (0, 0, s * DN_HEADS + h))
    secw = lambda s: (lambda h: (layer, 0, s * DN_HEADS + h))
    lay3 = lambda h: (layer, 0, 0)
    sds = jax.ShapeDtypeStruct
    return pl.pallas_call(
        functools.partial(_dn_step_kernel, bsz=bsz),
        grid=(DN_HEADS,),
        in_specs=[pl.BlockSpec((bsz, hd), sec(s)) for s in range(3)]
                 + [pl.BlockSpec((CONV_W - 1, bsz, hd), sec3(s)) for s in range(3)]
                 + [pl.BlockSpec((bsz, hd), lambda h: (0, h)),
                    pl.BlockSpec((bsz, LANE), lambda h: (0, 0))]
                 + [pl.BlockSpec((None, CONV_W, hd), secw(s)) for s in range(3)]
                 + [pl.BlockSpec((None, 2, LANE), lay3),
                    pl.BlockSpec((None, 1, hd), lay3),
                    pl.BlockSpec((None, bsz, None, hd, hd), lambda h: (layer, 0, h, 0, 0))],
        out_specs=(pl.BlockSpec((bsz, hd), lambda h: (0, h)),
                   pl.BlockSpec((bsz, None, hd, hd), lambda h: (0, h, 0, 0))),
        out_shape=(sds((bsz, DN_W), MXU_DTYPE), sds((bsz, DN_HEADS, hd, hd), F32)),
        scratch_shapes=[pltpu.VMEM((bsz, hd), F32)],
        compiler_params=_params("arbitrary"),
        name="deltanet_step",
    )(uq, uq, uq, conv_t, conv_t, conv_t, rest, ab, conv_w, conv_w, conv_w, prm,
      norm_g.reshape(-1, 1, hd), s0_all)


def _pack_pairs(h):
    half = h.shape[1] // 2
    hb = h.astype(MXU_DTYPE).astype(F32)
    lo = lax.shift_right_logical(pltpu.bitcast(hb[:, :half], jnp.uint32), jnp.uint32(16))
    return lo | (pltpu.bitcast(hb[:, half:], jnp.uint32) & jnp.uint32(0xFFFF0000))


def _unpack_pairs(w):
    lo = pltpu.bitcast(lax.shift_left(w, jnp.uint32(16)), F32)
    hi = pltpu.bitcast(w & jnp.uint32(0xFFFF0000), F32)
    return jnp.concatenate([lo, hi], axis=1).astype(MXU_DTYPE)


def _router_kernel(xp_ref, xs_ref, g_ref, scp_ref, shp_ref, scs_ref, shs_ref, wr_ref, br_ref, h_ref, r_ref, *, n_p):
    is_s = pl.program_id(0) >= n_p
    x = jnp.where(is_s, xs_ref[...], xp_ref[...])
    sc = jnp.where(is_s, scs_ref[...], scp_ref[...])
    sh = jnp.where(is_s, shs_ref[...], shp_ref[...])
    h = _rms(x, g_ref[...]) * (1.0 + sc) + sh
    h_ref[...] = _pack_pairs(h)
    logits = _mm(h, wr_ref[...]) + br_ref[...]
    lane = lax.broadcasted_iota(jnp.int32, logits.shape, 1).astype(F32)
    ninf = -jnp.inf
    big = float(LANE)
    lg = jnp.where(lane < N_GROUPS, logits, ninf)
    gmax = jnp.max(lg, axis=-1, keepdims=True)
    gsel = jnp.min(jnp.where(lg == gmax, lane, big), axis=-1, keepdims=True)
    pg = 1.0 / jnp.sum(jnp.where(lane < N_GROUPS, jnp.exp(logits - gmax), 0.0), axis=-1, keepdims=True)
    lo = N_GROUPS + EXPERTS_PER_GROUP * gsel
    le = jnp.where(jnp.logical_and(lane >= lo, lane < lo + EXPERTS_PER_GROUP), logits, ninf)
    v1 = jnp.max(le, axis=-1, keepdims=True)
    i1 = jnp.min(jnp.where(le == v1, lane, big), axis=-1, keepdims=True)
    le2 = jnp.where(lane == i1, ninf, le)
    v2 = jnp.max(le2, axis=-1, keepdims=True)
    i2 = jnp.min(jnp.where(le2 == v2, lane, big), axis=-1, keepdims=True)
    e2 = jnp.exp(v2 - v1)
    w1 = pg / (1.0 + e2)
    w2 = pg * e2 / (1.0 + e2)
    r_ref[...] = jnp.where(lane == 0, i1 - N_GROUPS,
                           jnp.where(lane == 1, i2 - N_GROUPS,
                                     jnp.where(lane == 2, w1, jnp.where(lane == 3, w2, 0.0))))


def _router(groups, xs, gain, mods, w_route, b_route, layer):
    gp, gs = groups
    tm = min(ROUTER_ROWS, gs.T)
    assert gp.L % tm == 0 and gs.T % tm == 0
    n_p, n_s, nt = gp.T // tm, gs.T // tm, gp.L // tm
    t_all = gp.T + gs.T
    p_row = lambda i, j: (jnp.minimum(i, n_p - 1), 0)
    s_row = lambda i, j: (jnp.maximum(i - n_p, 0), 0)
    p_mod = lambda k: pl.BlockSpec((None, 1, D_MODEL), lambda i, j: (jnp.minimum(i, n_p - 1) // nt, 0, k))
    s_mod = lambda k: pl.BlockSpec((tm, D_MODEL), lambda i, j: (jnp.maximum(i - n_p, 0), k))
    sds = jax.ShapeDtypeStruct
    return pl.pallas_call(
        functools.partial(_router_kernel, n_p=n_p),
        grid=(n_p + n_s, 1),
        in_specs=[pl.BlockSpec((tm, D_MODEL), p_row), pl.BlockSpec((tm, D_MODEL), s_row),
                  pl.BlockSpec((1, D_MODEL), lambda i, j: (0, 0)),
                  p_mod(4), p_mod(3), s_mod(4), s_mod(3),
                  pl.BlockSpec((None, D_MODEL, LANE), lambda i, j: (layer, 0, 0)),
                  pl.BlockSpec((None, 1, LANE), lambda i, j: (layer, 0, 0))],
        out_specs=(pl.BlockSpec((tm, D_MODEL // 2), lambda i, j: (i, 0)), pl.BlockSpec((tm, LANE), lambda i, j: (i, 0))),
        out_shape=(sds((t_all, D_MODEL // 2), jnp.uint32), sds((t_all, LANE), F32)),
        compiler_params=_params("arbitrary", "arbitrary"),
        name="moe_router",
    )(xs[0], xs[1], gain.reshape(1, D_MODEL), mods[0], mods[0], mods[1], mods[1], w_route, b_route)


def _row_copy(src_hbm, row, dst, dst_row, sem):
    return pltpu.make_async_copy(src_hbm.at[pl.ds(row, 1)], dst.at[pl.ds(dst_row, 1)], sem)


def _experts_kernel(be_ref, tok_ref, nact_ref, first_ref, nxt_ref, par_ref, h_hbm, wg_hbm, wu_hbm, wd_hbm, y_ref,
                    xbuf, gsem, wst_g, wst_u, wst_d, wsem, wgt, wut, wdt, *, layer):
    b = pl.program_id(0)
    nact = nact_ref[0]
    rb = MOE_ROWS

    def gather(blk, slot):
        for r in range(rb):
            _row_copy(h_hbm, tok_ref[blk * rb + r], xbuf.at[slot], r, gsem.at[slot]).start(priority=1)

    def weight_copies(e, slot):
        return (pltpu.make_async_copy(wg_hbm.at[layer, e], wst_g.at[slot], wsem.at[slot]),
                pltpu.make_async_copy(wu_hbm.at[layer, e], wst_u.at[slot], wsem.at[slot]),
                pltpu.make_async_copy(wd_hbm.at[layer, e], wst_d.at[slot], wsem.at[slot]))

    @pl.when(b == 0)
    def _():
        for cp in weight_copies(be_ref[0], par_ref[0]):
            cp.start()
        gather(0, 0)

    @pl.when(b + 1 < nact)
    def _():
        gather(b + 1, (b + 1) % 2)

    @pl.when(jnp.logical_and(b < nact, first_ref[b] == 1))
    def _():
        slot = par_ref[b]
        for cp in weight_copies(be_ref[b], slot):
            cp.wait()

        @pl.when(nxt_ref[b] >= 0)
        def _():
            for cp in weight_copies(nxt_ref[b], 1 - slot):
                cp.start()

        ck = XPOSE_CHUNK
        for j in range(D_MODEL // ck):
            wgt[:, pl.ds(j * ck, ck)] = wst_g[slot, pl.ds(j * ck, ck), :].T.astype(MXU_DTYPE)
            wut[:, pl.ds(j * ck, ck)] = wst_u[slot, pl.ds(j * ck, ck), :].T.astype(MXU_DTYPE)
            wdt[pl.ds(j * ck, ck), :] = wst_d[slot, :, pl.ds(j * ck, ck)].T.astype(MXU_DTYPE)

    @pl.when(b < nact)
    def _():
        slot = b % 2
        for r in range(rb):
            _row_copy(h_hbm, 0, xbuf.at[slot], r, gsem.at[slot]).wait()
        x = _unpack_pairs(xbuf[slot])
        hid_t = _silu(_mm_nt(wgt[...], x)) * _mm_nt(wut[...], x)
        y_ref[...] = _pack_pairs(_mm(wdt[...], hid_t).T)

    @pl.when(b >= nact)
    def _():
        y_ref[...] = jnp.zeros_like(y_ref)


def _experts(h_all, plan, w_gate, w_up, w_down, layer):
    block_e, row_token, n_active, first, nxt, par = plan
    n_blocks = block_e.shape[0]
    rb = MOE_ROWS
    any_spec = pl.BlockSpec(memory_space=pl.ANY)
    return pl.pallas_call(
        functools.partial(_experts_kernel, layer=layer),
        grid_spec=pltpu.PrefetchScalarGridSpec(
            num_scalar_prefetch=6,
            grid=(n_blocks,),
            in_specs=[any_spec, any_spec, any_spec, any_spec],
            out_specs=pl.BlockSpec((rb, D_MODEL // 2), lambda b, *_: (b, 0)),
            scratch_shapes=[pltpu.VMEM((2, rb, D_MODEL // 2), jnp.uint32), pltpu.SemaphoreType.DMA((2,)),
                            pltpu.VMEM((2, D_MODEL, D_EXPERT), F32), pltpu.VMEM((2, D_MODEL, D_EXPERT), F32),
                            pltpu.VMEM((2, D_EXPERT, D_MODEL), F32), pltpu.SemaphoreType.DMA((2,)),
                            pltpu.VMEM((D_EXPERT, D_MODEL), MXU_DTYPE), pltpu.VMEM((D_EXPERT, D_MODEL), MXU_DTYPE),
                            pltpu.VMEM((D_MODEL, D_EXPERT), MXU_DTYPE)]),
        out_shape=jax.ShapeDtypeStruct((n_blocks * rb, D_MODEL // 2), jnp.uint32),
        compiler_params=_params("arbitrary"),
        name="moe_experts",
    )(block_e, row_token, n_active, first, nxt, par, h_all, w_gate, w_up, w_down)


def _combine_kernel(pos_ref, y_hbm, x_ref, r_ref, g_ref, o_ref, buf, sem, *, tok0, tm):
    i = pl.program_id(0)
    n = pl.num_programs(0)

    def gather(tile, slot):
        for r in range(tm):
            for k in range(TOP_K):
                p = pos_ref[(tok0 + tile * tm + r) * TOP_K + k]
                _row_copy(y_hbm, p, buf.at[slot], k * tm + r, sem.at[slot]).start(priority=(r + k) % 2)

    @pl.when(i == 0)
    def _():
        gather(0, 0)

    @pl.when(i + 1 < n)
    def _():
        gather(i + 1, (i + 1) % 2)

    slot = i % 2
    for r in range(TOP_K * tm):
        _row_copy(y_hbm, 0, buf.at[slot], r, sem.at[slot]).wait()
    route = r_ref[...]
    w1, w2 = route[:, 2:3], route[:, 3:4]
    y1, y2 = buf[slot, pl.ds(0, tm), :], buf[slot, pl.ds(tm, tm), :]
    half = D_MODEL // 2
    lo = lambda w: pltpu.bitcast(lax.shift_left(w, jnp.uint32(16)), F32)
    hi = lambda w: pltpu.bitcast(w & jnp.uint32(0xFFFF0000), F32)
    for part, cols in ((lo, pl.ds(0, half)), (hi, pl.ds(half, half))):
        o_ref[:, cols] = x_ref[:, cols] + g_ref[:, cols] * (part(y1) * w1 + part(y2) * w2)


def _combine(g, tok0, pos, y_rows, x, route_all, mod):
    tm = min(COMBINE_ROWS, g.tm)
    gc = g._replace(tm=tm)
    nt = gc.nt
    assert tok0 % tm == 0
    blk0 = tok0 // tm
    if g.L == 1:
        gate_spec = pl.BlockSpec((tm, D_MODEL), lambda i, pos: (i, 5))
    else:
        gate_spec = pl.BlockSpec((None, 1, D_MODEL), lambda i, pos: (i // nt, 0, 5))
    return pl.pallas_call(
        functools.partial(_combine_kernel, tok0=tok0, tm=tm),
        grid_spec=pltpu.PrefetchScalarGridSpec(
            num_scalar_prefetch=1,
            grid=(gc.n_tiles,),
            in_specs=[pl.BlockSpec(memory_space=pl.ANY),
                      pl.BlockSpec((tm, D_MODEL), lambda i, pos: (i, 0)),
                      pl.BlockSpec((tm, LANE), lambda i, pos: (blk0 + i, 0)),
                      gate_spec],
            out_specs=pl.BlockSpec((tm, D_MODEL), lambda i, pos: (i, 0)),
            scratch_shapes=[pltpu.VMEM((2, TOP_K * tm, D_MODEL // 2), jnp.uint32), pltpu.SemaphoreType.DMA((2,))]),
        out_shape=jax.ShapeDtypeStruct((g.T, D_MODEL), F32),
        compiler_params=_params("arbitrary"),
        name="moe_combine",
    )(pos, y_rows, x, route_all, mod)


def _dispatch_plan(route_all):
    rb = MOE_ROWS
    n_tok = route_all.shape[0]
    n_assign = n_tok * TOP_K
    n_blocks = -(-(n_assign + N_EXPERTS * (rb - 1)) // rb)
    flat_e = route_all[:, :TOP_K].astype(jnp.int32).reshape(-1)
    order = jnp.argsort(flat_e).astype(jnp.int32)
    bounds = jnp.searchsorted(flat_e[order], jnp.arange(N_EXPERTS + 1, dtype=jnp.int32)).astype(jnp.int32)
    start, counts = bounds[:-1], bounds[1:] - bounds[:-1]
    rank = jnp.argsort(order).astype(jnp.int32) - start[flat_e]
    padded = (counts + rb - 1) // rb * rb
    pad_end = jnp.cumsum(padded)
    pad_start = pad_end - padded
    pos = pad_start[flat_e] + rank
    n_active = (pad_end[-1] // rb).astype(jnp.int32)
    blk = jnp.minimum(jnp.arange(n_blocks, dtype=jnp.int32), n_active - 1)
    block_e = jnp.minimum(jnp.searchsorted(pad_end, blk * rb, side='right'), N_EXPERTS - 1).astype(jnp.int32)
    row = jnp.arange(n_blocks * rb, dtype=jnp.int32)
    row_e = jnp.repeat(block_e, rb)
    idx = row - pad_start[row_e]
    src = jnp.clip(start[row_e] + idx, 0, n_assign - 1)
    row_token = jnp.where(jnp.logical_and(idx < counts[row_e], row < pad_end[-1]), order[src] // TOP_K, 0)
    bidx = jnp.arange(n_blocks, dtype=jnp.int32)
    first = jnp.logical_and(bidx < n_active, jnp.logical_or(bidx == 0, block_e != jnp.roll(block_e, 1)))
    par = (jnp.cumsum(first.astype(jnp.int32)) - 1) % 2
    nxt_first = lax.cummin(jnp.where(first, bidx, n_blocks)[::-1])[::-1]
    nxt_idx = jnp.concatenate([nxt_first[1:], jnp.full((1,), n_blocks, jnp.int32)])
    nxt = jnp.where(nxt_idx < n_blocks, block_e[jnp.minimum(nxt_idx, n_blocks - 1)], -1)
    plan = (block_e, row_token.astype(jnp.int32), n_active.reshape(1), first.astype(jnp.int32),
            nxt.astype(jnp.int32), par.astype(jnp.int32))
    return plan, pos.astype(jnp.int32)


def _layer(layer, groups, xs, mods, states, p, state_dn):
    x1s, new_states = [], []
    for g, x, mod, st in zip(groups, xs, mods, states):
        s5_re, s5_im, dn_conv, lru_h, lru_conv = st
        single = g.L == 1
        h = _norm_mod(g, x, p['norm1_g'][layer], mod, 0, 1)
        uq = _proj(g, h, p['w_pack'], layer, 0, BRANCH_W + 3 * DN_W, 1024)
        ab = _proj(g, h, p['w_pack'], layer, COL_AB, LANE, LANE)
        rest = _proj(g, h, p['w_pack'], layer, PACK_REST, REST_GATE, 512)

        ya, n_re, n_im = _s5(g, uq, s5_re, s5_im, p['s5_disc'], p['s5_cr'], p['s5_ci'], p['s5_d'],
                             p['s5_w_glu'], p['s5_b_glu'], layer)
        if single:
            qkv = uq[:, BRANCH_W:]
            lru_x = rest[:, DN_W:DN_W + LRU_W]
            yb, n_dn = _dn_step(g, uq, dn_conv, rest, ab, p['dn_conv_w'], p['dn_prm'], p['dn_norm_g'], state_dn, layer)
            n_dn_conv = jnp.concatenate([dn_conv[:, 1:], qkv[:, None, :]], axis=1)
            n_lru_conv = jnp.concatenate([lru_conv[:, 1:], lru_x[:, None, :]], axis=1)
            yc, n_lru = _lru(g, rest, lru_conv, lru_h, p, layer, pos0=p['past_len'])
        else:
            yb, n_dn = _dn_seq(g, uq, rest, ab, p['dn_conv_w'], p['dn_prm'], p['dn_norm_g'], layer)
            tail = lambda a: a.reshape(g.B, g.L, -1)[:, g.L - (CONV_W - 1):]
            n_dn_conv = tail(uq)[:, :, BRANCH_W:]
            n_lru_conv = tail(rest)[:, :, DN_W:DN_W + LRU_W]
            yc, n_lru = _lru(g, rest, None, None, p, layer, pos0=0)

        merged = _merge(g, h, ya, yb, yc, p['w_pack'], p['w_branch'], layer)
        x1 = _out_proj(g, merged, p['w_out'], layer, x, mod, 2)
        x1s.append(x1)
        new_states.append((n_re.reshape(g.B, S5_GROUPS, S5_STATE), n_im.reshape(g.B, S5_GROUPS, S5_STATE),
                           n_dn, n_dn_conv, n_lru, n_lru_conv))

    h_all, route_all = _router(groups, x1s, p['norm2_g'][layer], mods, p['w_route'], p['b_route'], layer)
    plan, pos = _dispatch_plan(route_all)
    y_rows = _experts(h_all, plan, p['w_e_gate'], p['w_e_up'], p['w_e_down'], layer)
    outs, tok0 = [], 0
    for g, x1, mod in zip(groups, x1s, mods):
        outs.append(_combine(g, tok0, pos, y_rows, x1, route_all, mod))
        tok0 += g.T
    return outs, new_states


def kernel(x_prompt, x_sample, c_prompt, c_sample, state_s5_re, state_s5_im, state_dn, state_dn_conv, state_lru, state_lru_conv, final_g, norm1_g, norm2_g, w_ada, b_ada, w_in, s5_log_dt, s5_a_re, s5_a_im, s5_b_re, s5_b_im, s5_c_re, s5_c_im, s5_d, s5_w_glu, s5_b_glu, dn_conv_w, dn_a_log, dn_dt_bias, dn_norm_g, lru_conv_w, lru_conv_b, lru_w_a, lru_b_a, lru_w_x, lru_b_x, lru_lambda, w_branch, w_out, w_rg, b_rg, w_re, b_re, w_e_gate, w_e_up, w_e_down):
    depth = w_in.shape[0]
    bp, lp, _ = x_prompt.shape
    bs, ls, _ = x_sample.shape
    assert ls == 1 and bs % SUBLANE == 0 and lp % SEQ_CHUNK == 0
    gp = _Group(bp, lp, min(lp, 1024))
    gs = _Group(bs, 1, bs)
    groups = (gp, gs)

    pad = (-bp) % SUBLANE
    c_all = jnp.concatenate([c_prompt, jnp.zeros((pad, D_MODEL), F32), c_sample], axis=0)
    mod_all = _ada_mod(c_all, w_ada, b_ada)

    prm = jnp.zeros((depth, 2, LANE), F32)
    prm = prm.at[:, 0, :DN_HEADS].set(dn_a_log).at[:, 1, :DN_HEADS].set(dn_dt_bias)
    w_route = jnp.concatenate([w_rg, w_re, jnp.zeros((depth, D_MODEL, LANE - N_GROUPS - N_EXPERTS), F32)], axis=-1)
    b_route = jnp.concatenate([b_rg, b_re, jnp.zeros((depth, LANE - N_GROUPS - N_EXPERTS), F32)], axis=-1)
    p = dict(norm1_g=norm1_g, norm2_g=norm2_g,
             w_pack=jnp.concatenate([w_in[:, :, :COL_AB + LANE],
                                     jnp.zeros((depth, D_MODEL, PACK_REST - COL_AB - LANE), F32),
                                     w_in[:, :, COL_REST:]], axis=-1).astype(MXU_DTYPE),
             s5_disc=_s5_discretize(s5_log_dt, s5_a_re, s5_a_im, s5_b_re, s5_b_im),
             s5_cr=_s5_out_blocks(s5_c_re), s5_ci=_s5_out_blocks(s5_c_im),
             s5_d=s5_d, s5_w_glu=s5_w_glu, s5_b_glu=s5_b_glu,
             dn_conv_w=dn_conv_w, dn_prm=prm, dn_norm_g=dn_norm_g,
             lru_conv_w=lru_conv_w, lru_conv_b=lru_conv_b, lru_wa_bd=_block_diag(lru_w_a), lru_b_a=lru_b_a,
             lru_wx_bd=_block_diag(lru_w_x), lru_b_x=lru_b_x, lru_lambda=lru_lambda,
             w_branch=w_branch, w_out=w_out.astype(MXU_DTYPE), w_route=w_route,
             b_route=b_route.reshape(depth, 1, LANE),
             w_e_gate=w_e_gate, w_e_up=w_e_up, w_e_down=w_e_down, past_len=PAST_LEN)

    xs = [x_prompt.reshape(gp.T, D_MODEL), x_sample.reshape(gs.T, D_MODEL)]
    per_layer = []
    for l in range(depth):
        mods = [mod_all[l, :bp].reshape(bp, 1, 6 * D_MODEL), mod_all[l, bp + pad:]]
        states = [(None, None, None, None, None),
                  (state_s5_re[l].reshape(bs, S5_N), state_s5_im[l].reshape(bs, S5_N), state_dn_conv[l],
                   state_lru[l], state_lru_conv[l])]
        xs, new_states = _layer(l, groups, xs, mods, states, p, state_dn)
        per_layer.append(new_states)

    y_prompt = _final_norm(gp, xs[0], final_g).reshape(bp, lp, D_MODEL)
    y_sample = _final_norm(gs, xs[1], final_g).reshape(bs, 1, D_MODEL)
    stack = lambda gi: tuple(jnp.stack([per_layer[l][gi][k] for l in range(depth)]) for k in range(6))
    return (y_prompt, y_sample) + stack(0) + stack(1)
```
